```python
import jax, jax.numpy as jnp
from jax import lax
import numpy as np

D_MODEL = 4096
BATCH = 4
SEQ = 4096
DEPTH = 4

CHUNK = 64
N_META = 16
Q_BLOCK = 128
RMS_EPS = 1e-6
D_FF = 3 * D_MODEL // 2
MIX_WIDTH = D_MODEL
HALF_MIX = MIX_WIDTH // 2

POOL_WINDOWS = (2, 4, 8, 16)
POOL_WIDTH = HALF_MIX
POOL_GROUP = POOL_WIDTH // len(POOL_WINDOWS)

MLA_NOPE = 128
MLA_ROPE = 64
MLA_V = 128
MLA_HEADS = HALF_MIX // MLA_V
MLA_Q_RANK = 1024
MLA_KV_RANK = 512
ROPE_BASE = 10000.0

FOX_HEAD_DIM = 128
FOX_HEADS = HALF_MIX // FOX_HEAD_DIM

GLA_HEADS = 4
GLA_DV = HALF_MIX // GLA_HEADS
GLA_DK = GLA_DV // 2
GLA_GATE_RANK = 16
GLA_TAU = 16.0
GLA_CHUNK = 64

AB_SPLITS = (POOL_WIDTH, MLA_Q_RANK, MLA_KV_RANK, MLA_ROPE)
AB_IN = sum(AB_SPLITS)
CD_SPLITS = (FOX_HEADS * FOX_HEAD_DIM, FOX_HEADS * FOX_HEAD_DIM, FOX_HEADS * FOX_HEAD_DIM, FOX_HEADS,
             GLA_HEADS * GLA_DK, GLA_HEADS * GLA_DK, GLA_HEADS * GLA_DV, GLA_GATE_RANK, GLA_HEADS * GLA_DV)
CD_IN = sum(CD_SPLITS)

kernel_name = 'hybrid_pool_mla_fox_gla_macaron'


def _split(t, sizes):
    idx, acc = [], 0
    for s in sizes[:-1]:
        acc += s
        idx.append(acc)
    return jnp.split(t, idx, axis=-1)


def _chunk_id(pos):
    return (pos + CHUNK - N_META) // CHUNK


def _chunk_end(pos):
    return CHUNK * (_chunk_id(pos) + 1) - (CHUNK - N_META)


def _rms_norm(x, gain):
    xf = x.astype(jnp.float32)
    y = xf * lax.rsqrt(jnp.mean(xf * xf, axis=-1, keepdims=True) + RMS_EPS)
    return (y * gain.astype(jnp.float32)).astype(x.dtype)


def _swiglu(x, w_gate, w_up, w_down):
    return (jax.nn.silu(x @ w_gate) * (x @ w_up)) @ w_down


def _rotary(x, cos, sin):
    xf = x.astype(jnp.float32)
    half = xf.shape[-1] // 2
    x1, x2 = xf[..., :half], xf[..., half:]
    return jnp.concatenate([x1 * cos - x2 * sin, x2 * cos + x1 * sin], axis=-1).astype(x.dtype)


def _pool_mixer(xp, pool_w, pool_scale):
    B, L, _ = xp.shape
    xf = xp.reshape(B, L, len(POOL_WINDOWS), POOL_GROUP).astype(jnp.float32)
    cs = jnp.cumsum(xf, axis=1)
    pos = jnp.arange(L)
    diffs = []
    for g, w in enumerate(POOL_WINDOWS):
        c = cs[:, :, g]
        c_prev = jnp.pad(c, ((0, 0), (w, 0), (0, 0)))[:, :L]
        cnt = jnp.minimum(pos + 1, w).astype(jnp.float32)[None, :, None]
        diffs.append((c - c_prev) / cnt - xf[:, :, g])
    d = jnp.stack(diffs, axis=2).astype(xp.dtype)
    y = jnp.einsum('blgc,gcd->blgd', d, pool_w).reshape(B, L, POOL_WIDTH)
    return y * pool_scale


def _mla_attention(q_nope, q_pe, k_nope, k_pe, v):
    L = q_nope.shape[1]
    scale = (MLA_NOPE + MLA_ROPE) ** -0.5
    cid = _chunk_id(jnp.arange(L))
    outs = []
    for q0 in range(0, L, Q_BLOCK):
        q1 = q0 + Q_BLOCK
        kend = min(L, _chunk_end(q1 - 1))
        s = (jnp.einsum('bqhd,bkhd->bhqk', q_nope[:, q0:q1], k_nope[:, :kend])
             + jnp.einsum('bqhr,bkr->bhqk', q_pe[:, q0:q1], k_pe[:, :kend]))
        s = s.astype(jnp.float32) * scale
        mask = cid[None, :kend] <= cid[q0:q1, None]
        p = jax.nn.softmax(jnp.where(mask, s, -jnp.inf), axis=-1).astype(v.dtype)
        outs.append(jnp.einsum('bhqk,bkhd->bqhd', p, v[:, :kend]))
    return jnp.concatenate(outs, axis=1)


def _fox_attention(q, k, v, F):
    L = q.shape[1]
    scale = FOX_HEAD_DIM ** -0.5
    Ft = jnp.transpose(F, (0, 2, 1))
    outs = []
    for q0 in range(0, L, Q_BLOCK):
        q1 = min(L, q0 + Q_BLOCK)
        kend = q1
        s = jnp.einsum('bqhd,bkhd->bhqk', q[:, q0:q1], k[:, :kend]).astype(jnp.float32) * scale
        decay = Ft[:, :, q0:q1, None] - Ft[:, :, None, :kend]
        mask = jnp.arange(kend)[None, :] <= (q0 + jnp.arange(q1 - q0))[:, None]
        p = jax.nn.softmax(jnp.where(mask, s + decay, -jnp.inf), axis=-1).astype(v.dtype)
        outs.append(jnp.einsum('bhqk,bkhd->bqhd', p, v[:, :kend]))
    return jnp.concatenate(outs, axis=1)


def _gla_attention(q, k, v, g_log):
    B, L, H, DK = q.shape
    DV = v.shape[-1]
    N = L // GLA_CHUNK

    def to_chunks(t):
        return t.reshape(B, N, GLA_CHUNK, H, t.shape[-1]).transpose(0, 3, 1, 2, 4).astype(jnp.float32)

    qc = to_chunks(q) * (DK ** -0.5)
    kc, vc, gc = to_chunks(k), to_chunks(v), to_chunks(g_log)
    b = jnp.cumsum(gc, axis=3)
    b_last = b[:, :, :, -1:, :]
    q_dec = qc * jnp.exp(b)
    k_inv = kc * jnp.exp(-b)
    k_end = kc * jnp.exp(b_last - b)
    causal = jnp.tril(jnp.ones((GLA_CHUNK, GLA_CHUNK), dtype=bool))
    a = jnp.where(causal, jnp.einsum('bhncd,bhnsd->bhncs', q_dec, k_inv), 0.0)
    o_intra = jnp.einsum('bhncs,bhnse->bhnce', a, vc)

    def step(S, xs):
        q_n, k_n, v_n, bl_n = xs
        o = jnp.einsum('bhcd,bhde->bhce', q_n, S)
        S = S * jnp.exp(bl_n)[..., None] + jnp.einsum('bhcd,bhce->bhde', k_n, v_n)
        return S, o

    xs = (jnp.moveaxis(q_dec, 2, 0), jnp.moveaxis(k_end, 2, 0), jnp.moveaxis(vc, 2, 0),
          jnp.moveaxis(b_last[:, :, :, 0, :], 2, 0))
    S0 = jnp.zeros((B, H, DK, DV), jnp.float32)
    _, o_inter = lax.scan(step, S0, xs)
    o = o_intra + jnp.moveaxis(o_inter, 0, 2)
    return o.transpose(0, 2, 3, 1, 4).reshape(B, L, H, DV)


def _pool_mla_mixer(hn, w_in, pool_w, pool_scale, q_norm, w_q_up, kv_norm, w_kv_up,
                    q_gain, k_gain, w_out, cos, sin):
    B, L, _ = hn.shape
    xp, c_q, c_kv, k_pe = _split(hn @ w_in, AB_SPLITS)
    y_pool = _pool_mixer(xp, pool_w, pool_scale)
    q = (_rms_norm(c_q, q_norm) @ w_q_up).reshape(B, L, MLA_HEADS, MLA_NOPE + MLA_ROPE)
    kv = (_rms_norm(c_kv, kv_norm) @ w_kv_up).reshape(B, L, MLA_HEADS, MLA_NOPE + MLA_V)
    q_nope = _rms_norm(q[..., :MLA_NOPE], q_gain[:MLA_NOPE])
    q_pe = _rotary(_rms_norm(q[..., MLA_NOPE:], q_gain[MLA_NOPE:]), cos[:, None], sin[:, None])
    k_nope = _rms_norm(kv[..., :MLA_NOPE], k_gain[:MLA_NOPE])
    v = kv[..., MLA_NOPE:]
    k_pe = _rotary(_rms_norm(k_pe, k_gain[MLA_NOPE:]), cos, sin)
    y_mla = _mla_attention(q_nope, q_pe, k_nope, k_pe, v).reshape(B, L, MLA_HEADS * MLA_V)
    return jnp.concatenate([y_pool, y_mla.astype(hn.dtype)], axis=-1) @ w_out


def _fox_gla_mixer(hn, w_in, fox_q_gain, fox_k_gain, fox_f_bias, gla_w_a2, gla_b_a, gla_o_norm, w_out):
    B, L, _ = hn.shape
    fq, fk, fv, ff, gq, gk, gv, ga, gr = _split(hn @ w_in, CD_SPLITS)
    q = _rms_norm(fq.reshape(B, L, FOX_HEADS, FOX_HEAD_DIM), fox_q_gain)
    k = _rms_norm(fk.reshape(B, L, FOX_HEADS, FOX_HEAD_DIM), fox_k_gain)
    v = fv.reshape(B, L, FOX_HEADS, FOX_HEAD_DIM)
    log_f = jax.nn.log_sigmoid(ff.astype(jnp.float32) + fox_f_bias.astype(jnp.float32))
    F = jnp.cumsum(log_f, axis=1)
    y_fox = _fox_attention(q, k, v, F).reshape(B, L, FOX_HEADS * FOX_HEAD_DIM).astype(hn.dtype)
    g_log = jax.nn.log_sigmoid((ga @ gla_w_a2 + gla_b_a).astype(jnp.float32)) / GLA_TAU
    o = _gla_attention(gq.reshape(B, L, GLA_HEADS, GLA_DK), gk.reshape(B, L, GLA_HEADS, GLA_DK),
                       gv.reshape(B, L, GLA_HEADS, GLA_DV), g_log.reshape(B, L, GLA_HEADS, GLA_DK))
    o = _rms_norm(o, gla_o_norm).reshape(B, L, GLA_HEADS * GLA_DV)
    y_gla = (o * jax.nn.silu(gr.astype(jnp.float32))).astype(hn.dtype)
    return jnp.concatenate([y_fox, y_gla], axis=-1) @ w_out


def setup_inputs(seed: int = 0) -> dict:
    key = jax.random.key(seed)
    ks = iter(jax.random.split(key, 40))
    f32 = jnp.float32
    NE = (DEPTH + 1) // 2
    NO = DEPTH // 2

    def w(shape, fan_in):
        return jax.random.normal(next(ks), shape, f32) * (fan_in ** -0.5)

    def gain(shape):
        return 1.0 + 0.05 * jax.random.normal(next(ks), shape, f32)

    def bias(shape, scale=0.1):
        return scale * jax.random.normal(next(ks), shape, f32)

    return {
        'x': jax.random.normal(next(ks), (BATCH, SEQ, D_MODEL), f32),
        'meta_tokens': jax.random.normal(next(ks), (N_META, D_MODEL), f32),
        'ffn1_norm': gain((DEPTH, D_MODEL)),
        'ffn1_w_gate': w((DEPTH, D_MODEL, D_FF), D_MODEL),
        'ffn1_w_up': w((DEPTH, D_MODEL, D_FF), D_MODEL),
        'ffn1_w_down': w((DEPTH, D_FF, D_MODEL), D_FF),
        'mix_norm': gain((DEPTH, D_MODEL)),
        'ffn2_norm': gain((DEPTH, D_MODEL)),
        'ffn2_w_gate': w((DEPTH, D_MODEL, D_FF), D_MODEL),
        'ffn2_w_up': w((DEPTH, D_MODEL, D_FF), D_MODEL),
        'ffn2_w_down': w((DEPTH, D_FF, D_MODEL), D_FF),
        'ab_w_in': w((NE, D_MODEL, AB_IN), D_MODEL),
        'pool_w': w((NE, len(POOL_WINDOWS), POOL_GROUP, POOL_GROUP), POOL_GROUP),
        'pool_scale': gain((NE, POOL_WIDTH)),
        'mla_q_norm': gain((NE, MLA_Q_RANK)),
        'mla_w_q_up': w((NE, MLA_Q_RANK, MLA_HEADS * (MLA_NOPE + MLA_ROPE)), MLA_Q_RANK),
        'mla_kv_norm': gain((NE, MLA_KV_RANK)),
        'mla_w_kv_up': w((NE, MLA_KV_RANK, MLA_HEADS * (MLA_NOPE + MLA_V)), MLA_KV_RANK),
        'mla_q_gain': gain((NE, MLA_NOPE + MLA_ROPE)),
        'mla_k_gain': gain((NE, MLA_NOPE + MLA_ROPE)),
        'ab_w_out': w((NE, MIX_WIDTH, D_MODEL), MIX_WIDTH),
        'cd_w_in': w((NO, D_MODEL, CD_IN), D_MODEL),
        'fox_q_gain': gain((NO, FOX_HEAD_DIM)),
        'fox_k_gain': gain((NO, FOX_HEAD_DIM)),
        'fox_f_bias': bias((NO, FOX_HEADS)),
        'gla_w_a2': w((NO, GLA_GATE_RANK, GLA_HEADS * GLA_DK), GLA_GATE_RANK),
        'gla_b_a': bias((NO, GLA_HEADS * GLA_DK)),
        'gla_o_norm': gain((NO, GLA_DV)),
        'cd_w_out': w((NO, MIX_WIDTH, D_MODEL), MIX_WIDTH),
    }


def reference(x, meta_tokens, ffn1_norm, ffn1_w_gate, ffn1_w_up, ffn1_w_down, mix_norm,
              ffn2_norm, ffn2_w_gate, ffn2_w_up, ffn2_w_down,
              ab_w_in, pool_w, pool_scale, mla_q_norm, mla_w_q_up, mla_kv_norm, mla_w_kv_up,
              mla_q_gain, mla_k_gain, ab_w_out,
              cd_w_in, fox_q_gain, fox_k_gain, fox_f_bias, gla_w_a2, gla_b_a, gla_o_norm, cd_w_out):
    B, S, _ = x.shape
    L = N_META + S
    Lp = -(-L // Q_BLOCK) * Q_BLOCK
    meta = jnp.broadcast_to(meta_tokens.astype(x.dtype)[None], (B, N_META, D_MODEL))
    h = jnp.concatenate([meta, x, jnp.zeros((B, Lp - L, D_MODEL), x.dtype)], axis=1)

    pos = jnp.arange(Lp, dtype=jnp.float32)
    inv_freq = ROPE_BASE ** (-jnp.arange(0, MLA_ROPE, 2, dtype=jnp.float32) / MLA_ROPE)
    ang = pos[:, None] * inv_freq[None, :]
    cos, sin = jnp.cos(ang), jnp.sin(ang)

    for layer in range(DEPTH):
        i = layer // 2
        h = h + 0.5 * _swiglu(_rms_norm(h, ffn1_norm[layer]), ffn1_w_gate[layer], ffn1_w_up[layer], ffn1_w_down[layer])
        hn = _rms_norm(h, mix_norm[layer])
        if layer % 2 == 0:
            h = h + _pool_mla_mixer(hn, ab_w_in[i], pool_w[i], pool_scale[i], mla_q_norm[i], mla_w_q_up[i],
                                    mla_kv_norm[i], mla_w_kv_up[i], mla_q_gain[i], mla_k_gain[i],
                                    ab_w_out[i], cos, sin)
        else:
            h = h + _fox_gla_mixer(hn, cd_w_in[i], fox_q_gain[i], fox_k_gain[i], fox_f_bias[i],
                                   gla_w_a2[i], gla_b_a[i], gla_o_norm[i], cd_w_out[i])
        h = h + 0.5 * _swiglu(_rms_norm(h, ffn2_norm[layer]), ffn2_w_gate[layer], ffn2_w_up[layer], ffn2_w_down[layer])
    return h[:, N_META:N_META + S]
```

```python
import functools

import jax
import jax.numpy as jnp
from jax import lax
from jax.experimental import pallas as pl
from jax.experimental.pallas import tpu as pltpu

F32 = jnp.float32
BF16 = jnp.bfloat16

N_META = 16
PAD_TO = 128
RMS_EPS = 1e-6
ROPE_BASE = 10000.0
CHUNK = 64

POOL_WINDOWS = (2, 4, 8, 16)
POOL_GROUP = 512
HEADS = 16
HEAD_DIM = 128
MLA_ROPE = 64
MLA_QK = 192
GLA_HEADS = 4
GLA_DK = 256
GLA_DV = 512
GLA_TAU = 16.0
GLA_RANK = 16

VMEM_BIG = 58 * 1024 * 1024
VMEM_MID = 40 * 1024 * 1024

ROW_TILE = 768
SEQ_TILE = 384
NEG_BIG = -1e30


def _silu(x):
    return x / (1.0 + jnp.exp(-x))


def _log_sigmoid(x):
    return jnp.minimum(x, 0.0) - jnp.log(1.0 + jnp.exp(-jnp.abs(x)))


def _rms(x, gain, n=None):
    n = x.shape[-1] if n is None else n
    ss = jnp.sum(x * x, axis=-1, keepdims=True)
    return x * lax.rsqrt(ss / n + RMS_EPS) * gain


def _params(sem, vmem):
    return pltpu.CompilerParams(dimension_semantics=sem, vmem_limit_bytes=vmem)


def _const_spec(shape):
    nd = len(shape)
    return pl.BlockSpec(shape, lambda *_: (0,) * nd, pipeline_mode=pl.Buffered(1))


def _norm_mm_kernel(a_ref, g_ref, *rest, n_w, epilogue, rows):
    w_refs, o_ref, xn_ref = rest[:n_w], rest[n_w], rest[n_w + 1]

    @pl.when(pl.program_id(1) == 0)
    def _():
        def body(c, carry):
            r0 = pl.multiple_of(c * rows, rows)
            a = a_ref[pl.ds(r0, rows), :]
            xn_ref[pl.ds(r0, rows), :] = _rms(a, g_ref[...]).astype(BF16)
            return carry
        lax.fori_loop(0, a_ref.shape[0] // rows, body, 0)

    xn = xn_ref[...]
    outs = [jnp.dot(xn, w[...], preferred_element_type=F32) for w in w_refs]
    o_ref[...] = epilogue(*outs).astype(o_ref.dtype)


def _norm_matmul(a, gain, ws, epilogue, out_dtype, tm, tn, name):
    t, k = a.shape
    n = ws[0].shape[1]
    kern = functools.partial(_norm_mm_kernel, n_w=len(ws), epilogue=epilogue, rows=64)
    return pl.pallas_call(
        kern,
        grid=(t // tm, n // tn),
        in_specs=[pl.BlockSpec((tm, k), lambda i, j: (i, 0)),
                  pl.BlockSpec((1, k), lambda i, j: (0, 0))]
                 + [pl.BlockSpec((k, tn), lambda i, j: (0, j)) for _ in ws],
        out_specs=pl.BlockSpec((tm, tn), lambda i, j: (i, j)),
        out_shape=jax.ShapeDtypeStruct((t, n), out_dtype),
        scratch_shapes=[pltpu.VMEM((tm, k), BF16)],
        compiler_params=_params(("parallel", "arbitrary"), VMEM_BIG),
        name=name,
    )(a, gain.reshape(1, k), *ws)


def _swiglu_epilogue(g, u):
    return _silu(g) * u


def _identity(x):
    return x


def _mm_res_kernel(*refs, k_sizes, scale):
    n_a = len(k_sizes)
    a_refs, w_ref, r_ref, o_ref = refs[:n_a], refs[n_a], refs[n_a + 1], refs[n_a + 2]
    acc, off = None, 0
    for a_ref, ks in zip(a_refs, k_sizes):
        p = jnp.dot(a_ref[...], w_ref[off:off + ks, :], preferred_element_type=F32)
        acc = p if acc is None else acc + p
        off += ks
    if scale != 1.0:
        acc = scale * acc
    o_ref[...] = r_ref[...] + acc


def _matmul_residual(a_list, w, res, scale, tm, tn, name):
    t, n = res.shape
    k_sizes = tuple(a.shape[1] for a in a_list)
    k = sum(k_sizes)
    kern = functools.partial(_mm_res_kernel, k_sizes=k_sizes, scale=scale)
    n_a = len(a_list)
    return pl.pallas_call(
        kern,
        grid=(t // tm, n // tn),
        in_specs=[pl.BlockSpec((tm, ks), lambda i, j: (i, 0)) for ks in k_sizes]
                 + [pl.BlockSpec((k, tn), lambda i, j: (0, j)),
                    pl.BlockSpec((tm, tn), lambda i, j: (i, j))],
        out_specs=pl.BlockSpec((tm, tn), lambda i, j: (i, j)),
        out_shape=jax.ShapeDtypeStruct((t, n), F32),
        input_output_aliases={n_a + 1: 0},
        compiler_params=_params(("parallel", "arbitrary"), VMEM_BIG),
        name=name,
    )(*a_list, w, res)


def _pool_kernel(x_ref, halo_ref, w_ref, s_ref, o_ref, xs_ref, *, tiles_per_seq):
    tm = x_ref.shape[0]
    halo = POOL_WINDOWS[-1]
    it = pl.program_id(0) % tiles_per_seq
    keep = (it > 0).astype(F32)
    xs_ref[0:halo, :] = halo_ref[...] * keep
    xs_ref[halo:halo + tm, :] = x_ref[...]
    pos = it * tm + lax.broadcasted_iota(jnp.int32, (tm, 1), 0)
    for g, win in enumerate(POOL_WINDOWS):
        c0 = g * POOL_GROUP
        x = xs_ref[halo:halo + tm, c0:c0 + POOL_GROUP]
        acc = x
        for j in range(1, win):
            acc = acc + xs_ref[halo - j:halo - j + tm, c0:c0 + POOL_GROUP]
        cnt = jnp.minimum(pos + 1, win).astype(F32)
        d = acc / cnt - x
        y = jnp.dot(d.astype(BF16), w_ref[g], preferred_element_type=F32)
        o_ref[:, c0:c0 + POOL_GROUP] = (y * s_ref[:, c0:c0 + POOL_GROUP]).astype(o_ref.dtype)


def _pool_mixer(ab, pool_w, pool_scale, lp):
    t = ab.shape[0]
    tm, halo = SEQ_TILE, POOL_WINDOWS[-1]
    width = len(POOL_WINDOWS) * POOL_GROUP
    per_halo = tm // halo
    kern = functools.partial(_pool_kernel, tiles_per_seq=lp // tm)
    return pl.pallas_call(
        kern,
        grid=(t // tm,),
        in_specs=[pl.BlockSpec((tm, width), lambda i: (i, 0)),
                  pl.BlockSpec((halo, width), lambda i: (jnp.maximum(i * per_halo - 1, 0), 0)),
                  _const_spec((len(POOL_WINDOWS), POOL_GROUP, POOL_GROUP)),
                  _const_spec((1, width))],
        out_specs=pl.BlockSpec((tm, width), lambda i: (i, 0)),
        out_shape=jax.ShapeDtypeStruct((t, width), BF16),
        scratch_shapes=[pltpu.VMEM((tm + halo, width), F32)],
        compiler_params=_params(("parallel",), VMEM_MID),
        name="pool_mixer",
    )(ab, ab, pool_w, pool_scale.reshape(1, width))


def _rotary_slot(x, c_ref, s1_ref, s2_ref):
    return (x * c_ref[...] + pltpu.roll(x, 96, axis=1) * s1_ref[...]
            + pltpu.roll(x, 32, axis=1) * s2_ref[...])


def _mla_prep_kernel(cq_ref, ckv_ref, kpe_ref, c_ref, s1_ref, s2_ref, qn_ref, kvn_ref,
                     wq_ref, wkv_ref, qgn_ref, qgr_ref, kgn_ref, kgr_ref,
                     q_ref, k_ref, v_ref):
    cq = _rms(cq_ref[...], qn_ref[...]).astype(BF16)
    ckv = _rms(ckv_ref[...], kvn_ref[...]).astype(BF16)
    kr = _rotary_slot(_rms(kpe_ref[...], kgr_ref[...], MLA_ROPE), c_ref, s1_ref, s2_ref).astype(BF16)
    for h in range(HEADS):
        lo = h * 2 * HEAD_DIM
        q = jnp.dot(cq, wq_ref[:, lo:lo + 2 * HEAD_DIM], preferred_element_type=F32)
        qn = _rms(q[:, :HEAD_DIM], qgn_ref[...])
        qr = _rotary_slot(_rms(q[:, HEAD_DIM:], qgr_ref[...], MLA_ROPE), c_ref, s1_ref, s2_ref)
        q_ref[:, lo:lo + HEAD_DIM] = qn.astype(BF16)
        q_ref[:, lo + HEAD_DIM:lo + 2 * HEAD_DIM] = qr.astype(BF16)
        kv = jnp.dot(ckv, wkv_ref[:, lo:lo + 2 * HEAD_DIM], preferred_element_type=F32)
        k_ref[:, lo:lo + HEAD_DIM] = _rms(kv[:, :HEAD_DIM], kgn_ref[...]).astype(BF16)
        k_ref[:, lo + HEAD_DIM:lo + 2 * HEAD_DIM] = kr
        v_ref[:, h * HEAD_DIM:(h + 1) * HEAD_DIM] = kv[:, HEAD_DIM:].astype(BF16)


def _mla_prep(ab3, tables, q_norm, kv_norm, wq, wkv, q_gain, k_gain):
    b, lp, _ = ab3.shape
    tm = SEQ_TILE
    q_rank, kv_rank = wq.shape[0], wkv.shape[0]
    cat = HEADS * 2 * HEAD_DIM
    zeros = jnp.zeros((HEAD_DIM - MLA_ROPE,), F32)
    slot = lambda g: jnp.concatenate([g[HEAD_DIM:], zeros]).reshape(1, HEAD_DIM)
    row = lambda i_, j_: (j_, 0)
    return pl.pallas_call(
        _mla_prep_kernel,
        grid=(b, lp // tm),
        in_specs=[pl.BlockSpec((None, tm, q_rank), lambda i, j: (i, j, 2048 // q_rank)),
                  pl.BlockSpec((None, tm, kv_rank), lambda i, j: (i, j, 3072 // kv_rank)),
                  pl.BlockSpec((None, tm, HEAD_DIM), lambda i, j: (i, j, 3584 // HEAD_DIM)),
                  pl.BlockSpec((tm, HEAD_DIM), row),
                  pl.BlockSpec((tm, HEAD_DIM), row),
                  pl.BlockSpec((tm, HEAD_DIM), row),
                  _const_spec((1, q_rank)), _const_spec((1, kv_rank)),
                  _const_spec((q_rank, cat)), _const_spec((kv_rank, cat)),
                  _const_spec((1, HEAD_DIM)), _const_spec((1, HEAD_DIM)),
                  _const_spec((1, HEAD_DIM)), _const_spec((1, HEAD_DIM))],
        out_specs=[pl.BlockSpec((None, tm, cat), lambda i, j: (i, j, 0)),
                   pl.BlockSpec((None, tm, cat), lambda i, j: (i, j, 0)),
                   pl.BlockSpec((None, tm, HEADS * HEAD_DIM), lambda i, j: (i, j, 0))],
        out_shape=[jax.ShapeDtypeStruct((b, lp, cat), BF16),
                   jax.ShapeDtypeStruct((b, lp, cat), BF16),
                   jax.ShapeDtypeStruct((b, lp, HEADS * HEAD_DIM), BF16)],
        compiler_params=_params(("parallel", "parallel"), VMEM_BIG),
        name="mla_prep",
    )(ab3, ab3, ab3, *tables, q_norm.reshape(1, -1), kv_norm.reshape(1, -1), wq, wkv,
      q_gain[:HEAD_DIM].reshape(1, HEAD_DIM), slot(q_gain),
      k_gain[:HEAD_DIM].reshape(1, HEAD_DIM), slot(k_gain))


def _attn_kernel(q_ref, k_ref, v_ref, *rest, chunked, scale, diag_width):
    if chunked:
        (o_ref,) = rest
    else:
        f_ref, o_ref = rest
    tq = q_ref.shape[0]
    lk = k_ref.shape[0]
    i = pl.program_id(2)
    q0 = pl.multiple_of(i * tq, tq)
    q = q_ref[...]
    if not chunked:
        f0 = jnp.max(f_ref[:, pl.ds(q0, tq)], axis=-1, keepdims=True)

    def scores(start, width):
        k = k_ref[pl.ds(start, width), :]
        s = lax.dot_general(q, k, (((1,), (1,)), ((), ())), preferred_element_type=F32) * scale
        if not chunked:
            s = s - (f_ref[:, pl.ds(start, width)] - f0)
        return s

    def update(carry, s, start, width):
        m, l, acc = carry
        m_new = jnp.maximum(m, jnp.max(s, axis=-1, keepdims=True))
        alpha = jnp.exp(m - m_new)
        p = jnp.exp(s - m_new)
        l = alpha * l + jnp.sum(p, axis=-1, keepdims=True)
        pv = jnp.dot(p.astype(BF16), v_ref[pl.ds(start, width), :], preferred_element_type=F32)
        return m_new, l, alpha * acc + pv

    def full_step(c, carry):
        start = pl.multiple_of(c * tq, tq)
        return update(carry, scores(start, tq), start, tq)

    init = (jnp.full((tq, 1), NEG_BIG, F32), jnp.zeros((tq, 1), F32),
            jnp.zeros((tq, v_ref.shape[1]), F32))
    carry = lax.fori_loop(0, i, full_step, init)

    start = pl.multiple_of(jnp.minimum(q0, lk - diag_width), 128)
    s = scores(start, diag_width)
    qpos = q0 + lax.broadcasted_iota(jnp.int32, (tq, diag_width), 0)
    kpos = start + lax.broadcasted_iota(jnp.int32, (tq, diag_width), 1)
    if chunked:
        shift = CHUNK - N_META
        mask = (kpos >= q0) & ((kpos + shift) // CHUNK <= (qpos + shift) // CHUNK)
    else:
        mask = kpos <= qpos
    m, l, acc = update(carry, jnp.where(mask, s, NEG_BIG), start, diag_width)
    o_ref[...] = (acc / l).astype(o_ref.dtype)


def _attention(q, k, v, f, *, chunked, scale, head_width):
    b, lp, _ = q.shape
    tq = SEQ_TILE
    diag_width = tq + PAD_TO if chunked else tq
    kern = functools.partial(_attn_kernel, chunked=chunked, scale=scale, diag_width=diag_width)
    in_specs = [pl.BlockSpec((None, tq, head_width), lambda bi, h, i: (bi, i, h)),
                pl.BlockSpec((None, lp, head_width), lambda bi, h, i: (bi, 0, h)),
                pl.BlockSpec((None, lp, HEAD_DIM), lambda bi, h, i: (bi, 0, h))]
    args = [q, k, v]
    if not chunked:
        in_specs.append(pl.BlockSpec((None, None, 1, lp), lambda bi, h, i: (bi, h, 0, 0)))
        args.append(f)
    return pl.pallas_call(
        kern,
        grid=(b, HEADS, lp // tq),
        in_specs=in_specs,
        out_specs=pl.BlockSpec((None, tq, HEAD_DIM), lambda bi, h, i: (bi, i, h)),
        out_shape=jax.ShapeDtypeStruct((b, lp, HEADS * HEAD_DIM), BF16),
        compiler_params=_params(("parallel", "parallel", "arbitrary"), VMEM_MID),
        name="mla_attention" if chunked else "fox_attention",
    )(*args)


def _fox_prep_kernel(q_ref, k_ref, v_ref, qg_ref, kg_ref, qo_ref, ko_ref, vo_ref):
    for h in range(HEADS):
        sl = slice(h * HEAD_DIM, (h + 1) * HEAD_DIM)
        qo_ref[:, sl] = _rms(q_ref[:, sl], qg_ref[...]).astype(BF16)
        ko_ref[:, sl] = _rms(k_ref[:, sl], kg_ref[...]).astype(BF16)
    vo_ref[...] = v_ref[...].astype(BF16)


def _fox_prep(cd, q_gain, k_gain):
    t = cd.shape[0]
    tm, width = SEQ_TILE, HEADS * HEAD_DIM
    out = jax.ShapeDtypeStruct((t, width), BF16)
    return pl.pallas_call(
        _fox_prep_kernel,
        grid=(t // tm,),
        in_specs=[pl.BlockSpec((tm, width), lambda i: (i, 0)),
                  pl.BlockSpec((tm, width), lambda i: (i, 1)),
                  pl.BlockSpec((tm, width), lambda i: (i, 2)),
                  _const_spec((1, HEAD_DIM)), _const_spec((1, HEAD_DIM))],
        out_specs=[pl.BlockSpec((tm, width), lambda i: (i, 0))] * 3,
        out_shape=[out, out, out],
        compiler_params=_params(("parallel",), VMEM_BIG),
        name="fox_prep",
    )(cd, cd, cd, q_gain.reshape(1, HEAD_DIM), k_gain.reshape(1, HEAD_DIM))


def _forget_cumsum_kernel(x_ref, b_ref, o_ref):
    rows, lp = x_ref.shape
    lane = lax.broadcasted_iota(jnp.int32, (rows, PAD_TO), 1)
    carry = jnp.zeros((rows, 1), F32)
    for c in range(lp // PAD_TO):
        sl = slice(c * PAD_TO, (c + 1) * PAD_TO)
        x = _log_sigmoid(x_ref[:, sl] + b_ref[...])
        step = 1
        while step < PAD_TO:
            x = x + jnp.where(lane >= step, pltpu.roll(x, step, axis=1), 0.0)
            step *= 2
        x = x + carry
        o_ref[:, sl] = x
        carry = x[:, PAD_TO - 1:PAD_TO]


def _forget_cumsum(ff_t, bias_col):
    rows, lp = ff_t.shape
    return pl.pallas_call(
        _forget_cumsum_kernel,
        out_shape=jax.ShapeDtypeStruct((rows, lp), F32),
        name="fox_forget_cumsum",
    )(ff_t, bias_col)


def _cumsum_rows(x):
    row = lax.broadcasted_iota(jnp.int32, x.shape, 0)
    step = 1
    while step < x.shape[0]:
        x = x + jnp.where(row >= step, pltpu.roll(x, step, axis=0), 0.0)
        step *= 2
    return x


def _gla_kernel(q_ref, k_ref, v_ref, r_ref, a_ref, wa_ref, ba_ref, on_ref, o_ref, st_ref):
    tm = q_ref.shape[0]

    @pl.when(pl.program_id(2) == 0)
    def _():
        st_ref[...] = jnp.zeros_like(st_ref)

    causal = (lax.broadcasted_iota(jnp.int32, (CHUNK, CHUNK), 1)
              <= lax.broadcasted_iota(jnp.int32, (CHUNK, CHUNK), 0))
    nt = (((1,), (1,)), ((), ()))
    tn = (((0,), (0,)), ((), ()))

    def body(c, carry):
        r0 = pl.multiple_of(c * CHUNK, CHUNK)
        rows = pl.ds(r0, CHUNK)
        gate = jnp.dot(a_ref[rows, :].astype(BF16), wa_ref[...], preferred_element_type=F32)
        g = _log_sigmoid(gate + ba_ref[...]) / GLA_TAU
        bcum = _cumsum_rows(g)
        b_last = bcum[CHUNK - 1:CHUNK, :]
        k = k_ref[rows, :]
        v = v_ref[rows, :].astype(BF16)
        q_dec = ((q_ref[rows, :] * (GLA_DK ** -0.5)) * jnp.exp(bcum)).astype(BF16)
        k_inv = (k * jnp.exp(-bcum)).astype(BF16)
        k_end = (k * jnp.exp(b_last - bcum)).astype(BF16)
        a = lax.dot_general(q_dec, k_inv, nt, preferred_element_type=F32)
        a = jnp.where(causal, a, 0.0).astype(BF16)
        st = st_ref[...]
        o = (jnp.dot(a, v, preferred_element_type=F32)
             + lax.dot_general(q_dec, st.astype(BF16), nt, preferred_element_type=F32))
        st_ref[...] = st * jnp.exp(b_last) + lax.dot_general(v, k_end, tn, preferred_element_type=F32)
        o = _rms(o, on_ref[...])
        o_ref[rows, :] = (o * _silu(r_ref[rows, :])).astype(o_ref.dtype)
        return carry

    lax.fori_loop(0, tm // CHUNK, body, 0)


def _gla(cd3, wa, ba, o_norm):
    b, lp, _ = cd3.shape
    tm = SEQ_TILE
    qb, kb = 6144 // GLA_DK, 7168 // GLA_DK
    vb, rb = 8192 // GLA_DV, 10240 // GLA_DV
    ab = 12288 // PAD_TO
    return pl.pallas_call(
        _gla_kernel,
        grid=(b, GLA_HEADS, lp // tm),
        in_specs=[pl.BlockSpec((None, tm, GLA_DK), lambda bi, h, i: (bi, i, qb + h)),
                  pl.BlockSpec((None, tm, GLA_DK), lambda bi, h, i: (bi, i, kb + h)),
                  pl.BlockSpec((None, tm, GLA_DV), lambda bi, h, i: (bi, i, vb + h)),
                  pl.BlockSpec((None, tm, GLA_DV), lambda bi, h, i: (bi, i, rb + h)),
                  pl.BlockSpec((None, tm, PAD_TO), lambda bi, h, i: (bi, i, ab)),
                  pl.BlockSpec((PAD_TO, GLA_DK), lambda bi, h, i: (0, h)),
                  pl.BlockSpec((1, GLA_DK), lambda bi, h, i: (0, h)),
                  pl.BlockSpec((1, GLA_DV), lambda bi, h, i: (0, 0))],
        out_specs=pl.BlockSpec((None, tm, GLA_DV), lambda bi, h, i: (bi, i, h)),
        out_shape=jax.ShapeDtypeStruct((b, lp, GLA_HEADS * GLA_DV), BF16),
        scratch_shapes=[pltpu.VMEM((GLA_DV, GLA_DK), F32)],
        compiler_params=_params(("parallel", "parallel", "arbitrary"), VMEM_MID),
        name="gla",
    )(cd3, cd3, cd3, cd3, cd3, wa, ba.reshape(1, -1), o_norm.reshape(1, -1))


def _pad_cols(w, n):
    return jnp.pad(w, ((0, 0), (0, n - w.shape[1])))


def _ab_in_weight(w):
    return _pad_cols(w, 3840).astype(BF16)


def _cd_in_weight(w):
    fqkv, ff, gqkv, ga, gr = (w[:, :6144], w[:, 6144:6160], w[:, 6160:10256],
                              w[:, 10256:10272], w[:, 10272:])
    return _pad_cols(jnp.concatenate([fqkv, gqkv, gr, ff, ga], axis=1), 12800).astype(BF16)


def _mla_q_weight(w):
    r = w.shape[0]
    w = jnp.pad(w.reshape(r, HEADS, MLA_QK), ((0, 0), (0, 0), (0, 2 * HEAD_DIM - MLA_QK)))
    return w.reshape(r, HEADS * 2 * HEAD_DIM).astype(BF16)


def _gla_gate_weight(w):
    return jnp.pad(w, ((GLA_RANK, PAD_TO - 2 * GLA_RANK), (0, 0))).astype(BF16)


def _rotary_tables(lp):
    pos = jnp.arange(lp, dtype=F32)
    inv_freq = ROPE_BASE ** (-jnp.arange(0, MLA_ROPE, 2, dtype=F32) / MLA_ROPE)
    ang = pos[:, None] * inv_freq[None, :]
    cos, sin = jnp.cos(ang), jnp.sin(ang)
    z32, z64 = jnp.zeros_like(cos), jnp.zeros((lp, HEAD_DIM - MLA_ROPE), F32)
    return (jnp.concatenate([cos, cos, z64], axis=1),
            jnp.concatenate([-sin, z32, z64], axis=1),
            jnp.concatenate([z32, sin, z64], axis=1))


def _ffn(h, norm, w_gate, w_up, w_down, name):
    hidden = _norm_matmul(h, norm, [w_gate.astype(BF16), w_up.astype(BF16)], _swiglu_epilogue,
                          BF16, ROW_TILE, 512, name + "_up")
    return _matmul_residual([hidden], w_down.astype(BF16), h, 0.5, ROW_TILE, 512, name + "_down")


def _pool_mla_layer(h, b, lp, norm, w_in, pool_w, pool_scale, q_norm, w_q_up, kv_norm, w_kv_up,
                    q_gain, k_gain, w_out, tables):
    ab = _norm_matmul(h, norm, [_ab_in_weight(w_in)], _identity, F32, ROW_TILE, 768, "ab_in")
    y_pool = _pool_mixer(ab, pool_w.astype(BF16), pool_scale, lp)
    q, k, v = _mla_prep(ab.reshape(b, lp, -1), tables, q_norm, kv_norm, _mla_q_weight(w_q_up),
                        w_kv_up.astype(BF16), q_gain, k_gain)
    y_mla = _attention(q, k, v, None, chunked=True, scale=MLA_QK ** -0.5, head_width=2 * HEAD_DIM)
    return _matmul_residual([y_pool, y_mla.reshape(b * lp, -1)], w_out.astype(BF16), h, 1.0,
                            ROW_TILE, 512, "ab_out")


def _fox_gla_layer(h, b, lp, norm, w_in, fox_q_gain, fox_k_gain, fox_f_bias, gla_w_a2, gla_b_a,
                   gla_o_norm, w_out):
    cd = _norm_matmul(h, norm, [_cd_in_weight(w_in)], _identity, F32, ROW_TILE, 512, "cd_in")
    q, k, v = _fox_prep(cd, fox_q_gain, fox_k_gain)
    ff_t = cd[:, 12288:12288 + HEADS].reshape(b, lp, HEADS).transpose(0, 2, 1).reshape(b * HEADS, lp)
    f = _forget_cumsum(ff_t, jnp.tile(fox_f_bias, b).reshape(b * HEADS, 1))
    shape3 = (b, lp, HEADS * HEAD_DIM)
    y_fox = _attention(q.reshape(shape3), k.reshape(shape3), v.reshape(shape3),
                       f.reshape(b, HEADS, 1, lp), chunked=False, scale=HEAD_DIM ** -0.5,
                       head_width=HEAD_DIM)
    y_gla = _gla(cd.reshape(b, lp, -1), _gla_gate_weight(gla_w_a2), gla_b_a, gla_o_norm)
    return _matmul_residual([y_fox.reshape(b * lp, -1), y_gla.reshape(b * lp, -1)],
                            w_out.astype(BF16), h, 1.0, ROW_TILE, 512, "cd_out")


def kernel(x, meta_tokens, ffn1_norm, ffn1_w_gate, ffn1_w_up, ffn1_w_down, mix_norm, ffn2_norm, ffn2_w_gate, ffn2_w_up, ffn2_w_down, ab_w_in, pool_w, pool_scale, mla_q_norm, mla_w_q_up, mla_kv_norm, mla_w_kv_up, mla_q_gain, mla_k_gain, ab_w_out, cd_w_in, fox_q_gain, fox_k_gain, fox_f_bias, gla_w_a2, gla_b_a, gla_o_norm, cd_w_out):
    b, s, d = x.shape
    length = N_META + s
    lp = -(-length // PAD_TO) * PAD_TO
    meta = jnp.broadcast_to(meta_tokens.astype(x.dtype)[None], (b, N_META, d))
    h = jnp.concatenate([meta, x, jnp.zeros((b, lp - length, d), x.dtype)], axis=1)
    h = h.reshape(b * lp, d)
    tables = _rotary_tables(lp)

    for layer in range(ffn1_norm.shape[0]):
        i = layer // 2
        h = _ffn(h, ffn1_norm[layer], ffn1_w_gate[layer], ffn1_w_up[layer], ffn1_w_down[layer],
                 "ffn1")
        if layer % 2 == 0:
            h = _pool_mla_layer(h, b, lp, mix_norm[layer], ab_w_in[i], pool_w[i], pool_scale[i],
                                mla_q_norm[i], mla_w_q_up[i], mla_kv_norm[i], mla_w_kv_up[i],
                                mla_q_gain[i], mla_k_gain[i], ab_w_out[i], tables)
        else:
            h = _fox_gla_layer(h, b, lp, mix_norm[layer], cd_w_in[i], fox_q_gain[i], fox_k_gain[i],
                               fox_f_bias[i], gla_w_a2[i], gla_b_a[i], gla_o_norm[i], cd_w_out[i])
        h = _ffn(h, ffn2_norm[layer], ffn2_w_gate[layer], ffn2_w_up[layer], ffn2_w_down[layer],
                 "ffn2")
    return h.reshape(b, lp, d)[:, N_META:N_META + s]
```

```python
import functools
import math

import jax
import jax.numpy as jnp
from jax import lax
from jax.experimental import pallas as pl
from jax.experimental.pallas import tpu as pltpu

F32 = jnp.float32
BF16 = jnp.bfloat16

N_META = 16
PAD_TO = 128
RMS_EPS = 1e-6
ROPE_BASE = 10000.0
CHUNK = 64

POOL_WINDOWS = (2, 4, 8, 16)
POOL_GROUP = 512
HEADS = 16
HEAD_DIM = 128
MLA_ROPE = 64
MLA_QK = 192
GLA_HEADS = 4
GLA_DK = 256
GLA_DV = 512
GLA_TAU = 16.0
GLA_RANK = 16

VMEM_BIG = 58 * 1024 * 1024
VMEM_MID = 40 * 1024 * 1024

ROW_TILE = 768
SEQ_TILE = 384
ATT_HEADS = 2
NEG_BIG = -1e30
LOG2E = math.log2(math.e)

NT_DIMS = (((1,), (1,)), ((), ()))
TN_DIMS = (((0,), (0,)), ((), ()))


def _silu(x):
    return x / (1.0 + jnp.exp(-x))


def _log_sigmoid(x):
    return jnp.minimum(x, 0.0) - jnp.log(1.0 + jnp.exp(-jnp.abs(x)))


def _rms(x, gain, n=None):
    n = x.shape[-1] if n is None else n
    ss = jnp.sum(x * x, axis=-1, keepdims=True)
    return x * lax.rsqrt(ss / n + RMS_EPS) * gain


def _params(sem, vmem):
    return pltpu.CompilerParams(dimension_semantics=sem, vmem_limit_bytes=vmem)


def _const_spec(shape, layer=None):
    if layer is None:
        nd = len(shape)
        return pl.BlockSpec(shape, lambda *_: (0,) * nd, pipeline_mode=pl.Buffered(1))
    nd = len(shape)
    return pl.BlockSpec((None,) + tuple(shape), lambda *_: (layer,) + (0,) * nd,
                        pipeline_mode=pl.Buffered(1))


def _norm_mm_kernel(a_ref, g_ref, *rest, n_w, epilogue, rows):
    w_refs, o_ref, xn_ref = rest[:n_w], rest[n_w], rest[n_w + 1]

    @pl.when(pl.program_id(1) == 0)
    def _():
        def body(c, carry):
            r0 = pl.multiple_of(c * rows, rows)
            a = a_ref[pl.ds(r0, rows), :]
            xn_ref[pl.ds(r0, rows), :] = _rms(a, g_ref[...]).astype(BF16)
            return carry
        lax.fori_loop(0, a_ref.shape[0] // rows, body, 0)

    xn = xn_ref[...]
    outs = [jnp.dot(xn, w[...], preferred_element_type=F32) for w in w_refs]
    o_ref[...] = epilogue(*outs).astype(o_ref.dtype)


def _norm_matmul(a, gain, ws, layer, epilogue, out_dtype, tm, tn, name):
    t, k = a.shape
    n = ws[0].shape[2]
    kern = functools.partial(_norm_mm_kernel, n_w=len(ws), epilogue=epilogue, rows=64)
    return pl.pallas_call(
        kern,
        grid=(t // tm, n // tn),
        in_specs=[pl.BlockSpec((tm, k), lambda i, j: (i, 0)),
                  pl.BlockSpec((1, k), lambda i, j: (0, 0))]
                 + [pl.BlockSpec((None, k, tn), lambda i, j: (layer, 0, j)) for _ in ws],
        out_specs=pl.BlockSpec((tm, tn), lambda i, j: (i, j)),
        out_shape=jax.ShapeDtypeStruct((t, n), out_dtype),
        scratch_shapes=[pltpu.VMEM((tm, k), BF16)],
        compiler_params=_params(("parallel", "arbitrary"), VMEM_BIG),
        name=name,
    )(a, gain.reshape(1, k), *ws)


def _swiglu_epilogue(g, u):
    return _silu(g) * u


def _identity(x):
    return x


def _mm_res_kernel(*refs, k_sizes, scale):
    n_a = len(k_sizes)
    a_refs, w_ref, r_ref, o_ref = refs[:n_a], refs[n_a], refs[n_a + 1], refs[n_a + 2]
    acc, off = None, 0
    for a_ref, ks in zip(a_refs, k_sizes):
        p = jnp.dot(a_ref[...], w_ref[off:off + ks, :], preferred_element_type=F32)
        acc = p if acc is None else acc + p
        off += ks
    if scale != 1.0:
        acc = scale * acc
    o_ref[...] = r_ref[...] + acc


def _matmul_residual(a_list, w, layer, res, scale, tm, tn, name):
    t, n = res.shape
    k_sizes = tuple(a.shape[1] for a in a_list)
    k = sum(k_sizes)
    kern = functools.partial(_mm_res_kernel, k_sizes=k_sizes, scale=scale)
    n_a = len(a_list)
    return pl.pallas_call(
        kern,
        grid=(t // tm, n // tn),
        in_specs=[pl.BlockSpec((tm, ks), lambda i, j: (i, 0)) for ks in k_sizes]
                 + [pl.BlockSpec((None, k, tn), lambda i, j: (layer, 0, j)),
                    pl.BlockSpec((tm, tn), lambda i, j: (i, j))],
        out_specs=pl.BlockSpec((tm, tn), lambda i, j: (i, j)),
        out_shape=jax.ShapeDtypeStruct((t, n), F32),
        input_output_aliases={n_a + 1: 0},
        compiler_params=_params(("parallel", "arbitrary"), VMEM_BIG),
        name=name,
    )(*a_list, w, res)


def _pool_kernel(x_ref, halo_ref, w_ref, s_ref, o_ref, xs_ref, *, tiles_per_seq):
    tm = x_ref.shape[0]
    halo = POOL_WINDOWS[-1]
    it = pl.program_id(0) % tiles_per_seq
    keep = (it > 0).astype(F32)
    xs_ref[0:halo, :] = halo_ref[...] * keep
    xs_ref[halo:halo + tm, :] = x_ref[...]
    pos = it * tm + lax.broadcasted_iota(jnp.int32, (tm, 1), 0)
    for g, win in enumerate(POOL_WINDOWS):
        c0 = g * POOL_GROUP
        x = xs_ref[halo:halo + tm, c0:c0 + POOL_GROUP]
        acc = x
        for j in range(1, win):
            acc = acc + xs_ref[halo - j:halo - j + tm, c0:c0 + POOL_GROUP]
        cnt = jnp.minimum(pos + 1, win).astype(F32)
        d = acc / cnt - x
        y = jnp.dot(d.astype(BF16), w_ref[g], preferred_element_type=F32)
        o_ref[:, c0:c0 + POOL_GROUP] = (y * s_ref[:, c0:c0 + POOL_GROUP]).astype(o_ref.dtype)


def _pool_mixer(ab, pool_w, layer, pool_scale, lp):
    t = ab.shape[0]
    tm, halo = SEQ_TILE, POOL_WINDOWS[-1]
    width = len(POOL_WINDOWS) * POOL_GROUP
    per_halo = tm // halo
    kern = functools.partial(_pool_kernel, tiles_per_seq=lp // tm)
    return pl.pallas_call(
        kern,
        grid=(t // tm,),
        in_specs=[pl.BlockSpec((tm, width), lambda i: (i, 0)),
                  pl.BlockSpec((halo, width), lambda i: (jnp.maximum(i * per_halo - 1, 0), 0)),
                  _const_spec((len(POOL_WINDOWS), POOL_GROUP, POOL_GROUP), layer),
                  _const_spec((1, width))],
        out_specs=pl.BlockSpec((tm, width), lambda i: (i, 0)),
        out_shape=jax.ShapeDtypeStruct((t, width), BF16),
        scratch_shapes=[pltpu.VMEM((tm + halo, width), F32)],
        compiler_params=_params(("parallel",), VMEM_MID),
        name="pool_mixer",
    )(ab, ab, pool_w, pool_scale.reshape(1, width))


def _rotary_slot(x, c_ref, s1_ref, s2_ref):
    return (x * c_ref[...] + pltpu.roll(x, 96, axis=1) * s1_ref[...]
            + pltpu.roll(x, 32, axis=1) * s2_ref[...])


def _ones_column(rows):
    lane = lax.broadcasted_iota(jnp.int32, (rows, HEAD_DIM), 1)
    return (lane == 0).astype(BF16)


def _mla_prep_kernel(cq_ref, ckv_ref, kpe_ref, c_ref, s1_ref, s2_ref, qn_ref, kvn_ref,
                     wq_ref, wkv_ref, qgn_ref, qgr_ref, kgn_ref, kgr_ref,
                     q_ref, k_ref, v_ref, *, q_scale):
    cq = _rms(cq_ref[...], qn_ref[...]).astype(BF16)
    ckv = _rms(ckv_ref[...], kvn_ref[...]).astype(BF16)
    kr = _rotary_slot(_rms(kpe_ref[...], kgr_ref[...], MLA_ROPE), c_ref, s1_ref, s2_ref).astype(BF16)
    ones = _ones_column(cq.shape[0])
    for h in range(HEADS):
        lo = h * 2 * HEAD_DIM
        mid, hi = lo + HEAD_DIM, lo + 2 * HEAD_DIM
        q = jnp.dot(cq, wq_ref[:, lo:hi], preferred_element_type=F32)
        qn = _rms(q[:, :HEAD_DIM], qgn_ref[...])
        qr = _rotary_slot(_rms(q[:, HEAD_DIM:], qgr_ref[...], MLA_ROPE), c_ref, s1_ref, s2_ref)
        q_ref[:, lo:mid] = (qn * q_scale).astype(BF16)
        q_ref[:, mid:hi] = (qr * q_scale).astype(BF16)
        kv = jnp.dot(ckv, wkv_ref[:, lo:hi], preferred_element_type=F32)
        k_ref[:, lo:mid] = _rms(kv[:, :HEAD_DIM], kgn_ref[...]).astype(BF16)
        k_ref[:, mid:hi] = kr
        v_ref[:, lo:mid] = kv[:, HEAD_DIM:].astype(BF16)
        v_ref[:, mid:hi] = ones


def _mla_prep(ab3, tables, q_norm, kv_norm, wq, wkv, layer, q_gain, k_gain, q_scale):
    b, lp, _ = ab3.shape
    tm = SEQ_TILE
    q_rank, kv_rank = wq.shape[1], wkv.shape[1]
    cat = HEADS * 2 * HEAD_DIM
    zeros = jnp.zeros((HEAD_DIM - MLA_ROPE,), F32)
    slot = lambda g: jnp.concatenate([g[HEAD_DIM:], zeros]).reshape(1, HEAD_DIM)
    row = lambda i_, j_: (j_, 0)
    out = jax.ShapeDtypeStruct((b, lp, cat), BF16)
    return pl.pallas_call(
        functools.partial(_mla_prep_kernel, q_scale=q_scale),
        grid=(b, lp // tm),
        in_specs=[pl.BlockSpec((None, tm, q_rank), lambda i, j: (i, j, 2048 // q_rank)),
                  pl.BlockSpec((None, tm, kv_rank), lambda i, j: (i, j, 3072 // kv_rank)),
                  pl.BlockSpec((None, tm, HEAD_DIM), lambda i, j: (i, j, 3584 // HEAD_DIM)),
                  pl.BlockSpec((tm, HEAD_DIM), row),
                  pl.BlockSpec((tm, HEAD_DIM), row),
                  pl.BlockSpec((tm, HEAD_DIM), row),
                  _const_spec((1, q_rank)), _const_spec((1, kv_rank)),
                  _const_spec((q_rank, cat), layer), _const_spec((kv_rank, cat), layer),
                  _const_spec((1, HEAD_DIM)), _const_spec((1, HEAD_DIM)),
                  _const_spec((1, HEAD_DIM)), _const_spec((1, HEAD_DIM))],
        out_specs=[pl.BlockSpec((None, tm, cat), lambda i, j: (i, j, 0))] * 3,
        out_shape=[out, out, out],
        compiler_params=_params(("parallel", "parallel"), VMEM_BIG),
        name="mla_prep",
    )(ab3, ab3, ab3, *tables, q_norm.reshape(1, -1), kv_norm.reshape(1, -1), wq, wkv,
      q_gain[:HEAD_DIM].reshape(1, HEAD_DIM), slot(q_gain),
      k_gain[:HEAD_DIM].reshape(1, HEAD_DIM), slot(k_gain))


def _softmax_sweep(qs, key_fn, val_fn, bias_fn, n_full, tq, diag_mask, tail, scratch):
    nh = len(qs)
    s_even, s_odd, m_ref, acc_ref = scratch

    def scores(h, start, width):
        s = lax.dot_general(qs[h], key_fn(h, start, width), NT_DIMS, preferred_element_type=F32)
        return bias_fn(h, s, start, width)

    def absorb(h, s, start, width):
        m = m_ref[h]
        m_new = jnp.maximum(m, jnp.max(s, axis=-1, keepdims=True))
        p = jnp.exp2(s - m_new).astype(BF16)
        pv = jnp.dot(p, val_fn(h, start, width), preferred_element_type=F32)
        acc_ref[h] = jnp.exp2(m - m_new) * acc_ref[h] + pv
        m_ref[h] = m_new

    def half_step(c, cur, nxt):
        start = pl.multiple_of(c * tq, tq)
        for h in range(nh):
            nxt[h] = scores(h, start + tq, tq)
            absorb(h, cur[h], start, tq)

    def step(c, carry):
        lax.cond(c % 2 == 0, lambda: half_step(c, s_even, s_odd), lambda: half_step(c, s_odd, s_even))
        return carry

    for h in range(nh):
        m_ref[h] = jnp.full((tq, 1), NEG_BIG, F32)
        acc_ref[h] = jnp.zeros((tq, 2 * HEAD_DIM), F32)
        s_even[h] = scores(h, 0, tq)
    lax.fori_loop(0, n_full, step, 0)
    diag_start = pl.multiple_of(n_full * tq, tq)
    outs = []
    for h in range(nh):
        s = lax.cond(n_full % 2 == 0, lambda: s_even[h], lambda: s_odd[h])
        absorb(h, jnp.where(diag_mask, s, NEG_BIG), diag_start, tq)
        if tail is not None:
            t_start, t_width, t_mask = tail
            absorb(h, jnp.where(t_mask, scores(h, t_start, t_width), NEG_BIG), t_start, t_width)
        acc = acc_ref[h]
        outs.append(acc[:, :HEAD_DIM] / acc[:, HEAD_DIM:HEAD_DIM + 1])
    return outs


def _sweep_scratch(tq):
    return [pltpu.VMEM((ATT_HEADS, tq, tq), F32), pltpu.VMEM((ATT_HEADS, tq, tq), F32),
            pltpu.VMEM((ATT_HEADS, tq, 1), F32), pltpu.VMEM((ATT_HEADS, tq, 2 * HEAD_DIM), F32)]


def _mla_attn_kernel(q_ref, k_ref, v_ref, o_ref, *scratch):
    tq, lk = q_ref.shape[0], k_ref.shape[0]
    width = 2 * HEAD_DIM
    i = pl.program_id(2)
    q0 = i * tq
    shift = CHUNK - N_META

    def chunk_mask(k_start, k_width):
        qc = (q0 + shift + lax.broadcasted_iota(jnp.int32, (tq, k_width), 0)) // CHUNK
        kc = (k_start + shift + lax.broadcasted_iota(jnp.int32, (tq, k_width), 1)) // CHUNK
        return kc <= qc

    t_start = pl.multiple_of(jnp.minimum(q0 + tq, lk - PAD_TO), PAD_TO)
    t_kpos = t_start + lax.broadcasted_iota(jnp.int32, (tq, PAD_TO), 1)
    t_mask = (t_kpos >= q0 + tq) & chunk_mask(t_start, PAD_TO)
    qs = [q_ref[:, h * width:(h + 1) * width] for h in range(ATT_HEADS)]
    outs = _softmax_sweep(
        qs,
        lambda h, s0, w: k_ref[pl.ds(s0, w), h * width:(h + 1) * width],
        lambda h, s0, w: v_ref[pl.ds(s0, w), h * width:(h + 1) * width],
        lambda h, s, s0, w: s,
        i, tq, chunk_mask(q0, tq), (t_start, PAD_TO, t_mask), scratch)
    for h in range(ATT_HEADS):
        o_ref[:, h * HEAD_DIM:(h + 1) * HEAD_DIM] = outs[h].astype(o_ref.dtype)


def _mla_attention(q, k, v):
    b, lp, _ = q.shape
    tq = SEQ_TILE
    gw = ATT_HEADS * 2 * HEAD_DIM
    resident = pl.BlockSpec((None, lp, gw), lambda bi, g, i: (bi, 0, g))
    return pl.pallas_call(
        _mla_attn_kernel,
        grid=(b, HEADS // ATT_HEADS, lp // tq),
        in_specs=[pl.BlockSpec((None, tq, gw), lambda bi, g, i: (bi, i, g)), resident, resident],
        out_specs=pl.BlockSpec((None, tq, ATT_HEADS * HEAD_DIM), lambda bi, g, i: (bi, i, g)),
        out_shape=jax.ShapeDtypeStruct((b, lp, HEADS * HEAD_DIM), BF16),
        scratch_shapes=_sweep_scratch(tq),
        compiler_params=_params(("parallel", "parallel", "arbitrary"), VMEM_MID),
        name="mla_attention",
    )(q, k, v)


def _fox_attn_kernel(q_ref, k_ref, v_ref, f_ref, qg_ref, kg_ref, o_ref, kn_ref, va_ref, fb_ref,
                     *scratch, q_scale):
    tq, lk = q_ref.shape[0], k_ref.shape[0]
    i = pl.program_id(2)
    q0 = pl.multiple_of(i * tq, tq)
    heads = [slice(h * HEAD_DIM, (h + 1) * HEAD_DIM) for h in range(ATT_HEADS)]

    @pl.when(i == 0)
    def _():
        ones = _ones_column(tq)

        def body(c, carry):
            rows = pl.ds(pl.multiple_of(c * tq, tq), tq)
            for h, sl in enumerate(heads):
                kn_ref[rows, sl] = _rms(k_ref[rows, sl], kg_ref[...]).astype(BF16)
                va_ref[h, rows, 0:HEAD_DIM] = v_ref[rows, sl].astype(BF16)
                va_ref[h, rows, HEAD_DIM:2 * HEAD_DIM] = ones
            return carry
        lax.fori_loop(0, lk // tq, body, 0)
        fb_ref[...] = f_ref[...] * LOG2E

    qs = [(_rms(q_ref[:, sl], qg_ref[...]) * q_scale).astype(BF16) for sl in heads]
    f0 = [jnp.max(fb_ref[h, :, pl.ds(q0, tq)], axis=-1, keepdims=True) for h in range(ATT_HEADS)]
    qpos = q0 + lax.broadcasted_iota(jnp.int32, (tq, tq), 0)
    kpos = q0 + lax.broadcasted_iota(jnp.int32, (tq, tq), 1)
    outs = _softmax_sweep(
        qs,
        lambda h, s0, w: kn_ref[pl.ds(s0, w), heads[h]],
        lambda h, s0, w: va_ref[h, pl.ds(s0, w), :],
        lambda h, s, s0, w: s - (fb_ref[h, :, pl.ds(s0, w)] - f0[h]),
        i, tq, kpos <= qpos, None, scratch)
    for h, sl in enumerate(heads):
        o_ref[:, sl] = outs[h].astype(o_ref.dtype)


def _fox_attention(cd3, f, q_gain, k_gain, q_scale):
    b, lp, _ = cd3.shape
    tq = SEQ_TILE
    gw = ATT_HEADS * HEAD_DIM
    groups = HEADS // ATT_HEADS
    kern = functools.partial(_fox_attn_kernel, q_scale=q_scale)
    return pl.pallas_call(
        kern,
        grid=(b, groups, lp // tq),
        in_specs=[pl.BlockSpec((None, tq, gw), lambda bi, g, i: (bi, i, g)),
                  pl.BlockSpec((None, lp, gw), lambda bi, g, i: (bi, 0, groups + g)),
                  pl.BlockSpec((None, lp, gw), lambda bi, g, i: (bi, 0, 2 * groups + g)),
                  pl.BlockSpec((None, ATT_HEADS, 1, lp), lambda bi, g, i: (bi, g, 0, 0)),
                  _const_spec((1, HEAD_DIM)), _const_spec((1, HEAD_DIM))],
        out_specs=pl.BlockSpec((None, tq, gw), lambda bi, g, i: (bi, i, g)),
        out_shape=jax.ShapeDtypeStruct((b, lp, HEADS * HEAD_DIM), BF16),
        scratch_shapes=[pltpu.VMEM((lp, gw), BF16),
                        pltpu.VMEM((ATT_HEADS, lp, 2 * HEAD_DIM), BF16),
                        pltpu.VMEM((ATT_HEADS, 1, lp), F32)] + _sweep_scratch(tq),
        compiler_params=_params(("parallel", "parallel", "arbitrary"), VMEM_MID),
        name="fox_attention",
    )(cd3, cd3, cd3, f, q_gain.reshape(1, HEAD_DIM), k_gain.reshape(1, HEAD_DIM))


def _forget_cumsum_kernel(x_ref, b_ref, o_ref):
    rows, lp = x_ref.shape
    lane = lax.broadcasted_iota(jnp.int32, (rows, PAD_TO), 1)
    carry = jnp.zeros((rows, 1), F32)
    for c in range(lp // PAD_TO):
        sl = slice(c * PAD_TO, (c + 1) * PAD_TO)
        x = _log_sigmoid(x_ref[:, sl] + b_ref[...])
        step = 1
        while step < PAD_TO:
            x = x + jnp.where(lane >= step, pltpu.roll(x, step, axis=1), 0.0)
            step *= 2
        x = x + carry
        o_ref[:, sl] = x
        carry = x[:, PAD_TO - 1:PAD_TO]


def _forget_cumsum(ff_t, bias_col):
    rows, lp = ff_t.shape
    return pl.pallas_call(
        _forget_cumsum_kernel,
        out_shape=jax.ShapeDtypeStruct((rows, lp), F32),
        name="fox_forget_cumsum",
    )(ff_t, bias_col)


def _cumsum_rows(x):
    row = lax.broadcasted_iota(jnp.int32, x.shape, 0)
    step = 1
    while step < x.shape[0]:
        x = x + jnp.where(row >= step, pltpu.roll(x, step, axis=0), 0.0)
        step *= 2
    return x


def _gla_kernel(q_ref, k_ref, v_ref, r_ref, a_ref, wa_ref, ba_ref, on_ref, o_ref, st_ref):
    tm = q_ref.shape[0]

    @pl.when(pl.program_id(2) == 0)
    def _():
        st_ref[...] = jnp.zeros_like(st_ref)

    causal = (lax.broadcasted_iota(jnp.int32, (CHUNK, CHUNK), 1)
              <= lax.broadcasted_iota(jnp.int32, (CHUNK, CHUNK), 0))

    def body(c, carry):
        r0 = pl.multiple_of(c * CHUNK, CHUNK)
        rows = pl.ds(r0, CHUNK)
        gate = jnp.dot(a_ref[rows, :].astype(BF16), wa_ref[...], preferred_element_type=F32)
        g = _log_sigmoid(gate + ba_ref[...]) / GLA_TAU
        bcum = _cumsum_rows(g)
        b_last = bcum[CHUNK - 1:CHUNK, :]
        k = k_ref[rows, :]
        v = v_ref[rows, :].astype(BF16)
        q_dec = ((q_ref[rows, :] * (GLA_DK ** -0.5)) * jnp.exp(bcum)).astype(BF16)
        k_inv = (k * jnp.exp(-bcum)).astype(BF16)
        k_end = (k * jnp.exp(b_last - bcum)).astype(BF16)
        a = lax.dot_general(q_dec, k_inv, NT_DIMS, preferred_element_type=F32)
        a = jnp.where(causal, a, 0.0).astype(BF16)
        st = st_ref[...]
        o = (jnp.dot(a, v, preferred_element_type=F32)
             + lax.dot_general(q_dec, st.astype(BF16), NT_DIMS, preferred_element_type=F32))
        st_ref[...] = st * jnp.exp(b_last) + lax.dot_general(v, k_end, TN_DIMS,
                                                             preferred_element_type=F32)
        o = _rms(o, on_ref[...])
        o_ref[rows, :] = (o * _silu(r_ref[rows, :])).astype(o_ref.dtype)
        return carry

    lax.fori_loop(0, tm // CHUNK, body, 0)


def _gla(cd3, wa, layer, ba, o_norm):
    b, lp, _ = cd3.shape
    tm = SEQ_TILE
    qb, kb = 6144 // GLA_DK, 7168 // GLA_DK
    vb, rb = 8192 // GLA_DV, 10240 // GLA_DV
    ab = 12288 // PAD_TO
    return pl.pallas_call(
        _gla_kernel,
        grid=(b, GLA_HEADS, lp // tm),
        in_specs=[pl.BlockSpec((None, tm, GLA_DK), lambda bi, h, i: (bi, i, qb + h)),
                  pl.BlockSpec((None, tm, GLA_DK), lambda bi, h, i: (bi, i, kb + h)),
                  pl.BlockSpec((None, tm, GLA_DV), lambda bi, h, i: (bi, i, vb + h)),
                  pl.BlockSpec((None, tm, GLA_DV), lambda bi, h, i: (bi, i, rb + h)),
                  pl.BlockSpec((None, tm, PAD_TO), lambda bi, h, i: (bi, i, ab)),
                  pl.BlockSpec((None, PAD_TO, GLA_DK), lambda bi, h, i: (layer, 0, h)),
                  pl.BlockSpec((1, GLA_DK), lambda bi, h, i: (0, h)),
                  pl.BlockSpec((1, GLA_DV), lambda bi, h, i: (0, 0))],
        out_specs=pl.BlockSpec((None, tm, GLA_DV), lambda bi, h, i: (bi, i, h)),
        out_shape=jax.ShapeDtypeStruct((b, lp, GLA_HEADS * GLA_DV), BF16),
        scratch_shapes=[pltpu.VMEM((GLA_DV, GLA_DK), F32)],
        compiler_params=_params(("parallel", "parallel", "arbitrary"), VMEM_MID),
        name="gla",
    )(cd3, cd3, cd3, cd3, cd3, wa, ba.reshape(1, -1), o_norm.reshape(1, -1))


def _ab_in_weight(w):
    w = w.astype(BF16)
    return jnp.concatenate([w, jnp.zeros(w.shape[:2] + (3840 - w.shape[2],), BF16)], axis=2)


def _cd_in_weight(w):
    w = w.astype(BF16)
    pieces = [w[..., :6144], w[..., 6160:10256], w[..., 10272:], w[..., 6144:6160],
              w[..., 10256:10272], jnp.zeros(w.shape[:2] + (12800 - w.shape[2],), BF16)]
    return jnp.concatenate(pieces, axis=2)


def _mla_q_weight(w):
    n, r, _ = w.shape
    w = jnp.pad(w.astype(BF16).reshape(n, r, HEADS, MLA_QK),
                ((0, 0), (0, 0), (0, 0), (0, 2 * HEAD_DIM - MLA_QK)))
    return w.reshape(n, r, HEADS * 2 * HEAD_DIM)


def _gla_gate_weight(w):
    return jnp.pad(w.astype(BF16), ((0, 0), (GLA_RANK, PAD_TO - 2 * GLA_RANK), (0, 0)))


def _rotary_tables(lp):
    pos = jnp.arange(lp, dtype=F32)
    inv_freq = ROPE_BASE ** (-jnp.arange(0, MLA_ROPE, 2, dtype=F32) / MLA_ROPE)
    ang = pos[:, None] * inv_freq[None, :]
    cos, sin = jnp.cos(ang), jnp.sin(ang)
    z32, z64 = jnp.zeros_like(cos), jnp.zeros((lp, HEAD_DIM - MLA_ROPE), F32)
    return (jnp.concatenate([cos, cos, z64], axis=1),
            jnp.concatenate([-sin, z32, z64], axis=1),
            jnp.concatenate([z32, sin, z64], axis=1))


def _ffn(h, norm, w_gate, w_up, w_down, layer, name):
    hidden = _norm_matmul(h, norm, [w_gate, w_up], layer, _swiglu_epilogue, BF16, ROW_TILE, 512,
                          name + "_up")
    return _matmul_residual([hidden], w_down, layer, h, 0.5, ROW_TILE, 512, name + "_down")


def _pool_mla_layer(h, b, lp, norm, w_in, pool_w, pool_scale, q_norm, wq, kv_norm, wkv,
                    q_gain, k_gain, w_out, layer, tables):
    ab = _norm_matmul(h, norm, [w_in], layer, _identity, F32, ROW_TILE, 768, "ab_in")
    y_pool = _pool_mixer(ab, pool_w, layer, pool_scale, lp)
    q, k, v = _mla_prep(ab.reshape(b, lp, -1), tables, q_norm, kv_norm, wq, wkv, layer,
                        q_gain, k_gain, MLA_QK ** -0.5 * LOG2E)
    y_mla = _mla_attention(q, k, v)
    return _matmul_residual([y_pool, y_mla.reshape(b * lp, -1)], w_out, layer, h, 1.0,
                            ROW_TILE, 512, "ab_out")


def _fox_gla_layer(h, b, lp, norm, w_in, fox_q_gain, fox_k_gain, fox_f_bias, wa, gla_b_a,
                   gla_o_norm, w_out, layer):
    cd = _norm_matmul(h, norm, [w_in], layer, _identity, F32, ROW_TILE, 512, "cd_in")
    cd3 = cd.reshape(b, lp, -1)
    ff_t = cd[:, 12288:12288 + HEADS].reshape(b, lp, HEADS).transpose(0, 2, 1).reshape(b * HEADS, lp)
    f = _forget_cumsum(ff_t, jnp.tile(fox_f_bias, b).reshape(b * HEADS, 1))
    y_fox = _fox_attention(cd3, f.reshape(b, HEADS, 1, lp), fox_q_gain, fox_k_gain,
                           HEAD_DIM ** -0.5 * LOG2E)
    y_gla = _gla(cd3, wa, layer, gla_b_a, gla_o_norm)
    return _matmul_residual([y_fox.reshape(b * lp, -1), y_gla.reshape(b * lp, -1)], w_out, layer,
                            h, 1.0, ROW_TILE, 512, "cd_out")


def kernel(x, meta_tokens, ffn1_norm, ffn1_w_gate, ffn1_w_up, ffn1_w_down, mix_norm, ffn2_norm, ffn2_w_gate, ffn2_w_up, ffn2_w_down, ab_w_in, pool_w, pool_scale, mla_q_norm, mla_w_q_up, mla_kv_norm, mla_w_kv_up, mla_q_gain, mla_k_gain, ab_w_out, cd_w_in, fox_q_gain, fox_k_gain, fox_f_bias, gla_w_a2, gla_b_a, gla_o_norm, cd_w_out):
    b, s, d = x.shape
    length = N_META + s
    lp = -(-length // PAD_TO) * PAD_TO
    meta = jnp.broadcast_to(meta_tokens.astype(x.dtype)[None], (b, N_META, d))
    h = jnp.concatenate([meta, x, jnp.zeros((b, lp - length, d), x.dtype)], axis=1)
    h = h.reshape(b * lp, d)
    tables = _rotary_tables(lp)

    ffn1 = [w.astype(BF16) for w in (ffn1_w_gate, ffn1_w_up, ffn1_w_down)]
    ffn2 = [w.astype(BF16) for w in (ffn2_w_gate, ffn2_w_up, ffn2_w_down)]
    ab_in, cd_in = _ab_in_weight(ab_w_in), _cd_in_weight(cd_w_in)
    ab_out, cd_out = ab_w_out.astype(BF16), cd_w_out.astype(BF16)
    pool_wb, wq, wkv = pool_w.astype(BF16), _mla_q_weight(mla_w_q_up), mla_w_kv_up.astype(BF16)
    wa = _gla_gate_weight(gla_w_a2)

    for layer in range(ffn1_norm.shape[0]):
        i = layer // 2
        h = _ffn(h, ffn1_norm[layer], *ffn1, layer, "ffn1")
        if layer % 2 == 0:
            h = _pool_mla_layer(h, b, lp, mix_norm[layer], ab_in, pool_wb, pool_scale[i],
                                mla_q_norm[i], wq, mla_kv_norm[i], wkv, mla_q_gain[i],
                                mla_k_gain[i], ab_out, i, tables)
        else:
            h = _fox_gla_layer(h, b, lp, mix_norm[layer], cd_in, fox_q_gain[i], fox_k_gain[i],
                               fox_f_bias[i], wa, gla_b_a[i], gla_o_norm[i], cd_out, i)
        h = _ffn(h, ffn2_norm[layer], *ffn2, layer, "ffn2")
    return h.reshape(b, lp, d)[:, N_META:N_META + s]
```

```python
import functools
import math

import jax
import jax.numpy as jnp
from jax import lax
from jax.experimental import pallas as pl
from jax.experimental.pallas import tpu as pltpu

F32 = jnp.float32
BF16 = jnp.bfloat16

N_META = 16
PAD_TO = 128
RMS_EPS = 1e-6
ROPE_BASE = 10000.0
CHUNK = 64

POOL_WINDOWS = (2, 4, 8, 16)
POOL_GROUP = 512
HEADS = 16
HEAD_DIM = 128
MLA_ROPE = 64
MLA_QK = 192
GLA_HEADS = 4
GLA_DK = 256
GLA_DV = 512
GLA_TAU = 16.0
GLA_RANK = 16

VMEM_BIG = 58 * 1024 * 1024
VMEM_MID = 40 * 1024 * 1024

ROW_TILE = 768
SEQ_TILE = 384
ATT_HEADS = 4
NEG_BIG = -1e30
LOG2E = math.log2(math.e)

NT_DIMS = (((1,), (1,)), ((), ()))
TN_DIMS = (((0,), (0,)), ((), ()))


def _silu(x):
    return x / (1.0 + jnp.exp(-x))


def _log_sigmoid(x):
    return jnp.minimum(x, 0.0) - jnp.log(1.0 + jnp.exp(-jnp.abs(x)))


def _rms(x, gain, n=None):
    n = x.shape[-1] if n is None else n
    ss = jnp.sum(x * x, axis=-1, keepdims=True)
    return x * lax.rsqrt(ss / n + RMS_EPS) * gain


def _params(sem, vmem):
    return pltpu.CompilerParams(dimension_semantics=sem, vmem_limit_bytes=vmem)


def _const_spec(shape, layer=None):
    if layer is None:
        nd = len(shape)
        return pl.BlockSpec(shape, lambda *_: (0,) * nd, pipeline_mode=pl.Buffered(1))
    nd = len(shape)
    return pl.BlockSpec((None,) + tuple(shape), lambda *_: (layer,) + (0,) * nd,
                        pipeline_mode=pl.Buffered(1))


def _norm_mm_kernel(a_ref, g_ref, *rest, n_w, epilogue, rows):
    w_refs, o_ref, xn_ref = rest[:n_w], rest[n_w], rest[n_w + 1]

    @pl.when(pl.program_id(1) == 0)
    def _():
        def body(c, carry):
            r0 = pl.multiple_of(c * rows, rows)
            a = a_ref[pl.ds(r0, rows), :]
            xn_ref[pl.ds(r0, rows), :] = _rms(a, g_ref[...]).astype(BF16)
            return carry
        lax.fori_loop(0, a_ref.shape[0] // rows, body, 0)

    xn = xn_ref[...]
    outs = [jnp.dot(xn, w[...], preferred_element_type=F32) for w in w_refs]
    o_ref[...] = epilogue(*outs).astype(o_ref.dtype)


def _norm_matmul(a, gain, ws, layer, epilogue, out_dtype, tm, tn, name):
    t, k = a.shape
    n = ws[0].shape[2]
    kern = functools.partial(_norm_mm_kernel, n_w=len(ws), epilogue=epilogue, rows=64)
    return pl.pallas_call(
        kern,
        grid=(t // tm, n // tn),
        in_specs=[pl.BlockSpec((tm, k), lambda i, j: (i, 0)),
                  pl.BlockSpec((1, k), lambda i, j: (0, 0))]
                 + [pl.BlockSpec((None, k, tn), lambda i, j: (layer, 0, j)) for _ in ws],
        out_specs=pl.BlockSpec((tm, tn), lambda i, j: (i, j)),
        out_shape=jax.ShapeDtypeStruct((t, n), out_dtype),
        scratch_shapes=[pltpu.VMEM((tm, k), BF16)],
        compiler_params=_params(("parallel", "arbitrary"), VMEM_BIG),
        name=name,
    )(a, gain.reshape(1, k), *ws)


def _swiglu_epilogue(g, u):
    return _silu(g) * u


def _identity(x):
    return x


def _mm_res_kernel(*refs, k_sizes, scale):
    n_a = len(k_sizes)
    a_refs, w_ref, r_ref, o_ref = refs[:n_a], refs[n_a], refs[n_a + 1], refs[n_a + 2]
    acc, off = None, 0
    for a_ref, ks in zip(a_refs, k_sizes):
        p = jnp.dot(a_ref[...], w_ref[off:off + ks, :], preferred_element_type=F32)
        acc = p if acc is None else acc + p
        off += ks
    if scale != 1.0:
        acc = scale * acc
    o_ref[...] = r_ref[...] + acc


def _matmul_residual(a_list, w, layer, res, scale, tm, tn, name):
    t, n = res.shape
    k_sizes = tuple(a.shape[1] for a in a_list)
    k = sum(k_sizes)
    kern = functools.partial(_mm_res_kernel, k_sizes=k_sizes, scale=scale)
    n_a = len(a_list)
    return pl.pallas_call(
        kern,
        grid=(t // tm, n // tn),
        in_specs=[pl.BlockSpec((tm, ks), lambda i, j: (i, 0)) for ks in k_sizes]
                 + [pl.BlockSpec((None, k, tn), lambda i, j: (layer, 0, j)),
                    pl.BlockSpec((tm, tn), lambda i, j: (i, j))],
        out_specs=pl.BlockSpec((tm, tn), lambda i, j: (i, j)),
        out_shape=jax.ShapeDtypeStruct((t, n), F32),
        input_output_aliases={n_a + 1: 0},
        compiler_params=_params(("parallel", "arbitrary"), VMEM_BIG),
        name=name,
    )(*a_list, w, res)


def _pool_kernel(x_ref, halo_ref, w_ref, s_ref, o_ref, xs_ref, *, tiles_per_seq):
    tm = x_ref.shape[0]
    halo = POOL_WINDOWS[-1]
    it = pl.program_id(0) % tiles_per_seq
    keep = (it > 0).astype(F32)
    xs_ref[0:halo, :] = halo_ref[...] * keep
    xs_ref[halo:halo + tm, :] = x_ref[...]
    pos = it * tm + lax.broadcasted_iota(jnp.int32, (tm, 1), 0)
    for g, win in enumerate(POOL_WINDOWS):
        c0 = g * POOL_GROUP
        x = xs_ref[halo:halo + tm, c0:c0 + POOL_GROUP]
        acc = x
        for j in range(1, win):
            acc = acc + xs_ref[halo - j:halo - j + tm, c0:c0 + POOL_GROUP]
        cnt = jnp.minimum(pos + 1, win).astype(F32)
        d = acc / cnt - x
        y = jnp.dot(d.astype(BF16), w_ref[g], preferred_element_type=F32)
        o_ref[:, c0:c0 + POOL_GROUP] = (y * s_ref[:, c0:c0 + POOL_GROUP]).astype(o_ref.dtype)


def _pool_mixer(ab, pool_w, layer, pool_scale, lp):
    t = ab.shape[0]
    tm, halo = SEQ_TILE, POOL_WINDOWS[-1]
    width = len(POOL_WINDOWS) * POOL_GROUP
    per_halo = tm // halo
    kern = functools.partial(_pool_kernel, tiles_per_seq=lp // tm)
    return pl.pallas_call(
        kern,
        grid=(t // tm,),
        in_specs=[pl.BlockSpec((tm, width), lambda i: (i, 0)),
                  pl.BlockSpec((halo, width), lambda i: (jnp.maximum(i * per_halo - 1, 0), 0)),
                  _const_spec((len(POOL_WINDOWS), POOL_GROUP, POOL_GROUP), layer),
                  _const_spec((1, width))],
        out_specs=pl.BlockSpec((tm, width), lambda i: (i, 0)),
        out_shape=jax.ShapeDtypeStruct((t, width), BF16),
        scratch_shapes=[pltpu.VMEM((tm + halo, width), F32)],
        compiler_params=_params(("parallel",), VMEM_MID),
        name="pool_mixer",
    )(ab, ab, pool_w, pool_scale.reshape(1, width))


def _rotary_slot(x, c_ref, s1_ref, s2_ref):
    return (x * c_ref[...] + pltpu.roll(x, 96, axis=1) * s1_ref[...]
            + pltpu.roll(x, 32, axis=1) * s2_ref[...])


def _ones_column(rows):
    lane = lax.broadcasted_iota(jnp.int32, (rows, HEAD_DIM), 1)
    return (lane == 0).astype(BF16)


def _mla_prep_kernel(cq_ref, ckv_ref, kpe_ref, c_ref, s1_ref, s2_ref, qn_ref, kvn_ref,
                     wq_ref, wkv_ref, qgn_ref, qgr_ref, kgn_ref, kgr_ref,
                     q_ref, k_ref, v_ref, *, q_scale):
    cq = _rms(cq_ref[...], qn_ref[...]).astype(BF16)
    ckv = _rms(ckv_ref[...], kvn_ref[...]).astype(BF16)
    kr = _rotary_slot(_rms(kpe_ref[...], kgr_ref[...], MLA_ROPE), c_ref, s1_ref, s2_ref).astype(BF16)
    ones = _ones_column(cq.shape[0])
    for h in range(HEADS):
        lo = h * 2 * HEAD_DIM
        mid, hi = lo + HEAD_DIM, lo + 2 * HEAD_DIM
        q = jnp.dot(cq, wq_ref[:, lo:hi], preferred_element_type=F32)
        qn = _rms(q[:, :HEAD_DIM], qgn_ref[...])
        qr = _rotary_slot(_rms(q[:, HEAD_DIM:], qgr_ref[...], MLA_ROPE), c_ref, s1_ref, s2_ref)
        q_ref[:, lo:mid] = (qn * q_scale).astype(BF16)
        q_ref[:, mid:hi] = (qr * q_scale).astype(BF16)
        kv = jnp.dot(ckv, wkv_ref[:, lo:hi], preferred_element_type=F32)
        k_ref[:, lo:mid] = _rms(kv[:, :HEAD_DIM], kgn_ref[...]).astype(BF16)
        k_ref[:, mid:hi] = kr
        v_ref[:, lo:mid] = kv[:, HEAD_DIM:].astype(BF16)
        v_ref[:, mid:hi] = ones


def _mla_prep(ab3, tables, q_norm, kv_norm, wq, wkv, layer, q_gain, k_gain, q_scale):
    b, lp, _ = ab3.shape
    tm = SEQ_TILE
    q_rank, kv_rank = wq.shape[1], wkv.shape[1]
    cat = HEADS * 2 * HEAD_DIM
    zeros = jnp.zeros((HEAD_DIM - MLA_ROPE,), F32)
    slot = lambda g: jnp.concatenate([g[HEAD_DIM:], zeros]).reshape(1, HEAD_DIM)
    row = lambda i_, j_: (j_, 0)
    out = jax.ShapeDtypeStruct((b, lp, cat), BF16)
    return pl.pallas_call(
        functools.partial(_mla_prep_kernel, q_scale=q_scale),
        grid=(b, lp // tm),
        in_specs=[pl.BlockSpec((None, tm, q_rank), lambda i, j: (i, j, 2048 // q_rank)),
                  pl.BlockSpec((None, tm, kv_rank), lambda i, j: (i, j, 3072 // kv_rank)),
                  pl.BlockSpec((None, tm, HEAD_DIM), lambda i, j: (i, j, 3584 // HEAD_DIM)),
                  pl.BlockSpec((tm, HEAD_DIM), row),
                  pl.BlockSpec((tm, HEAD_DIM), row),
                  pl.BlockSpec((tm, HEAD_DIM), row),
                  _const_spec((1, q_rank)), _const_spec((1, kv_rank)),
                  _const_spec((q_rank, cat), layer), _const_spec((kv_rank, cat), layer),
                  _const_spec((1, HEAD_DIM)), _const_spec((1, HEAD_DIM)),
                  _const_spec((1, HEAD_DIM)), _const_spec((1, HEAD_DIM))],
        out_specs=[pl.BlockSpec((None, tm, cat), lambda i, j: (i, j, 0))] * 3,
        out_shape=[out, out, out],
        compiler_params=_params(("parallel", "parallel"), VMEM_BIG),
        name="mla_prep",
    )(ab3, ab3, ab3, *tables, q_norm.reshape(1, -1), kv_norm.reshape(1, -1), wq, wkv,
      q_gain[:HEAD_DIM].reshape(1, HEAD_DIM), slot(q_gain),
      k_gain[:HEAD_DIM].reshape(1, HEAD_DIM), slot(k_gain))


def _softmax_sweep(qs, key_fn, val_fn, bias_fn, n_full, tq, diag_mask, tail, scratch):
    nh = len(qs)
    s_even, s_odd, m_ref, acc_ref = scratch

    def scores(h, start, width):
        s = lax.dot_general(qs[h], key_fn(h, start, width), NT_DIMS, preferred_element_type=F32)
        return bias_fn(h, s, start, width)

    def absorb(h, s, vals):
        m = m_ref[h]
        m_new = jnp.maximum(m, jnp.max(s, axis=-1, keepdims=True))
        p = jnp.exp2(s - m_new).astype(BF16)
        pv = jnp.dot(p, vals, preferred_element_type=F32)
        acc_ref[h] = jnp.exp2(m - m_new) * acc_ref[h] + pv
        m_ref[h] = m_new

    def half_step(c, cur, nxt, diagonal):
        start = pl.multiple_of(c * tq, tq)
        for h in range(nh):
            if not diagonal:
                nxt[h] = scores(h, start + tq, tq)
                absorb(h, cur[h], val_fn(h, start, tq))
                continue
            s = jnp.where(diag_mask(), cur[h], NEG_BIG)
            vals = val_fn(h, start, tq)
            if tail is not None:
                t_start, t_width, t_mask = tail
                s_tail = jnp.where(t_mask(), scores(h, t_start, t_width), NEG_BIG)
                s = jnp.concatenate([s, s_tail], axis=1)
                vals = jnp.concatenate([vals, val_fn(h, t_start, t_width)], axis=0)
            absorb(h, s, vals)

    def by_parity(c, diagonal):
        return lambda: lax.cond(c % 2 == 0, lambda: half_step(c, s_even, s_odd, diagonal),
                                lambda: half_step(c, s_odd, s_even, diagonal))

    def step(c, carry):
        lax.cond(c == n_full, by_parity(c, True), by_parity(c, False))
        return carry

    for h in range(nh):
        m_ref[h] = jnp.full((tq, 1), NEG_BIG, F32)
        acc_ref[h] = jnp.zeros((tq, 2 * HEAD_DIM), F32)
        s_even[h] = scores(h, 0, tq)
    lax.fori_loop(0, n_full + 1, step, 0)
    outs = []
    for h in range(nh):
        acc = acc_ref[h]
        outs.append(acc[:, :HEAD_DIM] / acc[:, HEAD_DIM:HEAD_DIM + 1])
    return outs


def _sweep_scratch(tq):
    return [pltpu.VMEM((ATT_HEADS, tq, tq), F32), pltpu.VMEM((ATT_HEADS, tq, tq), F32),
            pltpu.VMEM((ATT_HEADS, tq, 1), F32), pltpu.VMEM((ATT_HEADS, tq, 2 * HEAD_DIM), F32)]


def _mla_attn_kernel(q_ref, k_ref, v_ref, o_ref, *scratch):
    tq, lk = q_ref.shape[0], k_ref.shape[0]
    width = 2 * HEAD_DIM
    i = pl.program_id(2)
    q0 = i * tq
    shift = CHUNK - N_META

    def chunk_mask(k_start, k_width):
        qc = (q0 + shift + lax.broadcasted_iota(jnp.int32, (tq, k_width), 0)) // CHUNK
        kc = (k_start + shift + lax.broadcasted_iota(jnp.int32, (tq, k_width), 1)) // CHUNK
        return kc <= qc

    t_start = pl.multiple_of(jnp.minimum(q0 + tq, lk - PAD_TO), PAD_TO)

    def t_mask():
        t_kpos = t_start + lax.broadcasted_iota(jnp.int32, (tq, PAD_TO), 1)
        return (t_kpos >= q0 + tq) & chunk_mask(t_start, PAD_TO)

    qs = [q_ref[:, h * width:(h + 1) * width] for h in range(ATT_HEADS)]
    outs = _softmax_sweep(
        qs,
        lambda h, s0, w: k_ref[pl.ds(s0, w), h * width:(h + 1) * width],
        lambda h, s0, w: v_ref[pl.ds(s0, w), h * width:(h + 1) * width],
        lambda h, s, s0, w: s,
        i, tq, lambda: chunk_mask(q0, tq), (t_start, PAD_TO, t_mask), scratch)
    for h in range(ATT_HEADS):
        o_ref[:, h * HEAD_DIM:(h + 1) * HEAD_DIM] = outs[h].astype(o_ref.dtype)


def _mla_attention(q, k, v):
    b, lp, _ = q.shape
    tq = SEQ_TILE
    gw = ATT_HEADS * 2 * HEAD_DIM
    resident = pl.BlockSpec((None, lp, gw), lambda bi, g, i: (bi, 0, g))
    return pl.pallas_call(
        _mla_attn_kernel,
        grid=(b, HEADS // ATT_HEADS, lp // tq),
        in_specs=[pl.BlockSpec((None, tq, gw), lambda bi, g, i: (bi, i, g)), resident, resident],
        out_specs=pl.BlockSpec((None, tq, ATT_HEADS * HEAD_DIM), lambda bi, g, i: (bi, i, g)),
        out_shape=jax.ShapeDtypeStruct((b, lp, HEADS * HEAD_DIM), BF16),
        scratch_shapes=_sweep_scratch(tq),
        compiler_params=_params(("parallel", "parallel", "arbitrary"), VMEM_BIG),
        name="mla_attention",
    )(q, k, v)


def _fox_attn_kernel(q_ref, k_ref, v_ref, f_ref, qg_ref, kg_ref, o_ref, kn_ref, va_ref, fb_ref,
                     *scratch, q_scale):
    tq, lk = q_ref.shape[0], k_ref.shape[0]
    i = pl.program_id(2)
    q0 = pl.multiple_of(i * tq, tq)
    heads = [slice(h * HEAD_DIM, (h + 1) * HEAD_DIM) for h in range(ATT_HEADS)]

    @pl.when(i == 0)
    def _():
        ones = _ones_column(tq)

        def body(c, carry):
            rows = pl.ds(pl.multiple_of(c * tq, tq), tq)
            for h, sl in enumerate(heads):
                kn_ref[rows, sl] = _rms(k_ref[rows, sl], kg_ref[...]).astype(BF16)
                va_ref[h, rows, 0:HEAD_DIM] = v_ref[rows, sl].astype(BF16)
                va_ref[h, rows, HEAD_DIM:2 * HEAD_DIM] = ones
            return carry
        lax.fori_loop(0, lk // tq, body, 0)
        fb_ref[...] = f_ref[...] * LOG2E

    qs = [(_rms(q_ref[:, sl], qg_ref[...]) * q_scale).astype(BF16) for sl in heads]
    f0 = [jnp.max(fb_ref[h, :, pl.ds(q0, tq)], axis=-1, keepdims=True) for h in range(ATT_HEADS)]

    def causal():
        return (lax.broadcasted_iota(jnp.int32, (tq, tq), 1)
                <= lax.broadcasted_iota(jnp.int32, (tq, tq), 0))

    outs = _softmax_sweep(
        qs,
        lambda h, s0, w: kn_ref[pl.ds(s0, w), heads[h]],
        lambda h, s0, w: va_ref[h, pl.ds(s0, w), :],
        lambda h, s, s0, w: s - (fb_ref[h, :, pl.ds(s0, w)] - f0[h]),
        i, tq, causal, None, scratch)
    for h, sl in enumerate(heads):
        o_ref[:, sl] = outs[h].astype(o_ref.dtype)


def _fox_attention(cd3, f, q_gain, k_gain, q_scale):
    b, lp, _ = cd3.shape
    tq = SEQ_TILE
    gw = ATT_HEADS * HEAD_DIM
    groups = HEADS // ATT_HEADS
    kern = functools.partial(_fox_attn_kernel, q_scale=q_scale)
    return pl.pallas_call(
        kern,
        grid=(b, groups, lp // tq),
        in_specs=[pl.BlockSpec((None, tq, gw), lambda bi, g, i: (bi, i, g)),
                  pl.BlockSpec((None, lp, gw), lambda bi, g, i: (bi, 0, groups + g),
                               pipeline_mode=pl.Buffered(1)),
                  pl.BlockSpec((None, lp, gw), lambda bi, g, i: (bi, 0, 2 * groups + g),
                               pipeline_mode=pl.Buffered(1)),
                  pl.BlockSpec((None, ATT_HEADS, 1, lp), lambda bi, g, i: (bi, g, 0, 0)),
                  _const_spec((1, HEAD_DIM)), _const_spec((1, HEAD_DIM))],
        out_specs=pl.BlockSpec((None, tq, gw), lambda bi, g, i: (bi, i, g)),
        out_shape=jax.ShapeDtypeStruct((b, lp, HEADS * HEAD_DIM), BF16),
        scratch_shapes=[pltpu.VMEM((lp, gw), BF16),
                        pltpu.VMEM((ATT_HEADS, lp, 2 * HEAD_DIM), BF16),
                        pltpu.VMEM((ATT_HEADS, 1, lp), F32)] + _sweep_scratch(tq),
        compiler_params=_params(("parallel", "parallel", "arbitrary"), VMEM_BIG),
        name="fox_attention",
    )(cd3, cd3, cd3, f, q_gain.reshape(1, HEAD_DIM), k_gain.reshape(1, HEAD_DIM))


def _forget_cumsum_kernel(x_ref, b_ref, o_ref):
    rows, lp = x_ref.shape
    lane = lax.broadcasted_iota(jnp.int32, (rows, PAD_TO), 1)
    carry = jnp.zeros((rows, 1), F32)
    for c in range(lp // PAD_TO):
        sl = slice(c * PAD_TO, (c + 1) * PAD_TO)
        x = _log_sigmoid(x_ref[:, sl] + b_ref[...])
        step = 1
        while step < PAD_TO:
            x = x + jnp.where(lane >= step, pltpu.roll(x, step, axis=1), 0.0)
            step *= 2
        x = x + carry
        o_ref[:, sl] = x
        carry = x[:, PAD_TO - 1:PAD_TO]


def _forget_cumsum(ff_t, bias_col):
    rows, lp = ff_t.shape
    return pl.pallas_call(
        _forget_cumsum_kernel,
        out_shape=jax.ShapeDtypeStruct((rows, lp), F32),
        name="fox_forget_cumsum",
    )(ff_t, bias_col)


def _cumsum_rows(x):
    row = lax.broadcasted_iota(jnp.int32, x.shape, 0)
    step = 1
    while step < x.shape[0]:
        x = x + jnp.where(row >= step, pltpu.roll(x, step, axis=0), 0.0)
        step *= 2
    return x


def _gla_kernel(q_ref, k_ref, v_ref, r_ref, a_ref, wa_ref, ba_ref, on_ref, o_ref, st_ref):
    tm = q_ref.shape[0]

    @pl.when(pl.program_id(1) == 0)
    def _():
        st_ref[...] = jnp.zeros_like(st_ref)

    causal = (lax.broadcasted_iota(jnp.int32, (CHUNK, CHUNK), 1)
              <= lax.broadcasted_iota(jnp.int32, (CHUNK, CHUNK), 0))

    def body(c, carry):
        r0 = pl.multiple_of(c * CHUNK, CHUNK)
        rows = pl.ds(r0, CHUNK)
        gate = jnp.dot(a_ref[rows, :].astype(BF16), wa_ref[...], preferred_element_type=F32)
        for h in range(GLA_HEADS):
            ks = slice(h * GLA_DK, (h + 1) * GLA_DK)
            vs = slice(h * GLA_DV, (h + 1) * GLA_DV)
            g = _log_sigmoid(gate[:, ks] + ba_ref[:, ks]) / GLA_TAU
            bcum = _cumsum_rows(g)
            b_last = bcum[CHUNK - 1:CHUNK, :]
            k = k_ref[rows, ks]
            v = v_ref[rows, vs].astype(BF16)
            q_dec = ((q_ref[rows, ks] * (GLA_DK ** -0.5)) * jnp.exp(bcum)).astype(BF16)
            k_inv = (k * jnp.exp(-bcum)).astype(BF16)
            k_end = (k * jnp.exp(b_last - bcum)).astype(BF16)
            a = lax.dot_general(q_dec, k_inv, NT_DIMS, preferred_element_type=F32)
            a = jnp.where(causal, a, 0.0).astype(BF16)
            st = st_ref[h]
            o = (jnp.dot(a, v, preferred_element_type=F32)
                 + lax.dot_general(q_dec, st.astype(BF16), NT_DIMS, preferred_element_type=F32))
            st_ref[h] = st * jnp.exp(b_last) + lax.dot_general(v, k_end, TN_DIMS,
                                                               preferred_element_type=F32)
            o = _rms(o, on_ref[...])
            o_ref[rows, vs] = (o * _silu(r_ref[rows, vs])).astype(o_ref.dtype)
        return carry

    lax.fori_loop(0, tm // CHUNK, body, 0)


def _gla(cd3, wa, layer, ba, o_norm):
    b, lp, _ = cd3.shape
    tm = SEQ_TILE
    kw, vw = GLA_HEADS * GLA_DK, GLA_HEADS * GLA_DV
    return pl.pallas_call(
        _gla_kernel,
        grid=(b, lp // tm),
        in_specs=[pl.BlockSpec((None, tm, kw), lambda bi, i: (bi, i, 6144 // kw)),
                  pl.BlockSpec((None, tm, kw), lambda bi, i: (bi, i, 7168 // kw)),
                  pl.BlockSpec((None, tm, vw), lambda bi, i: (bi, i, 8192 // vw)),
                  pl.BlockSpec((None, tm, vw), lambda bi, i: (bi, i, 10240 // vw)),
                  pl.BlockSpec((None, tm, PAD_TO), lambda bi, i: (bi, i, 12288 // PAD_TO)),
                  _const_spec((PAD_TO, kw), layer), _const_spec((1, kw)), _const_spec((1, GLA_DV))],
        out_specs=pl.BlockSpec((None, tm, vw), lambda bi, i: (bi, i, 0)),
        out_shape=jax.ShapeDtypeStruct((b, lp, vw), BF16),
        scratch_shapes=[pltpu.VMEM((GLA_HEADS, GLA_DV, GLA_DK), F32)],
        compiler_params=_params(("parallel", "arbitrary"), VMEM_MID),
        name="gla",
    )(cd3, cd3, cd3, cd3, cd3, wa, ba.reshape(1, -1), o_norm.reshape(1, -1))


def _ab_in_weight(w):
    w = w.astype(BF16)
    return jnp.concatenate([w, jnp.zeros(w.shape[:2] + (3840 - w.shape[2],), BF16)], axis=2)


def _cd_in_weight(w):
    w = w.astype(BF16)
    pieces = [w[..., :6144], w[..., 6160:10256], w[..., 10272:], w[..., 6144:6160],
              w[..., 10256:10272], jnp.zeros(w.shape[:2] + (12800 - w.shape[2],), BF16)]
    return jnp.concatenate(pieces, axis=2)


def _mla_q_weight(w):
    n, r, _ = w.shape
    w = jnp.pad(w.astype(BF16).reshape(n, r, HEADS, MLA_QK),
                ((0, 0), (0, 0), (0, 0), (0, 2 * HEAD_DIM - MLA_QK)))
    return w.reshape(n, r, HEADS * 2 * HEAD_DIM)


def _gla_gate_weight(w):
    return jnp.pad(w.astype(BF16), ((0, 0), (GLA_RANK, PAD_TO - 2 * GLA_RANK), (0, 0)))


def _rotary_tables(lp):
    pos = jnp.arange(lp, dtype=F32)
    inv_freq = ROPE_BASE ** (-jnp.arange(0, MLA_ROPE, 2, dtype=F32) / MLA_ROPE)
    ang = pos[:, None] * inv_freq[None, :]
    cos, sin = jnp.cos(ang), jnp.sin(ang)
    z32, z64 = jnp.zeros_like(cos), jnp.zeros((lp, HEAD_DIM - MLA_ROPE), F32)
    return (jnp.concatenate([cos, cos, z64], axis=1),
            jnp.concatenate([-sin, z32, z64], axis=1),
            jnp.concatenate([z32, sin, z64], axis=1))


def _ffn(h, norm, w_gate, w_up, w_down, layer, name):
    hidden = _norm_matmul(h, norm, [w_gate, w_up], layer, _swiglu_epilogue, BF16, ROW_TILE, 512,
                          name + "_up")
    return _matmul_residual([hidden], w_down, layer, h, 0.5, ROW_TILE, 512, name + "_down")


def _pool_mla_layer(h, b, lp, norm, w_in, pool_w, pool_scale, q_norm, wq, kv_norm, wkv,
                    q_gain, k_gain, w_out, layer, tables):
    ab = _norm_matmul(h, norm, [w_in], layer, _identity, F32, ROW_TILE, 768, "ab_in")
    y_pool = _pool_mixer(ab, pool_w, layer, pool_scale, lp)
    q, k, v = _mla_prep(ab.reshape(b, lp, -1), tables, q_norm, kv_norm, wq, wkv, layer,
                        q_gain, k_gain, MLA_QK ** -0.5 * LOG2E)
    y_mla = _mla_attention(q, k, v)
    return _matmul_residual([y_pool, y_mla.reshape(b * lp, -1)], w_out, layer, h, 1.0,
                            ROW_TILE, 512, "ab_out")


def _fox_gla_layer(h, b, lp, norm, w_in, fox_q_gain, fox_k_gain, fox_f_bias, wa, gla_b_a,
                   gla_o_norm, w_out, layer):
    cd = _norm_matmul(h, norm, [w_in], layer, _identity, F32, ROW_TILE, 512, "cd_in")
    cd3 = cd.reshape(b, lp, -1)
    ff_t = cd[:, 12288:12288 + HEADS].reshape(b, lp, HEADS).transpose(0, 2, 1).reshape(b * HEADS, lp)
    f = _forget_cumsum(ff_t, jnp.tile(fox_f_bias, b).reshape(b * HEADS, 1))
    y_fox = _fox_attention(cd3, f.reshape(b, HEADS, 1, lp), fox_q_gain, fox_k_gain,
                           HEAD_DIM ** -0.5 * LOG2E)
    y_gla = _gla(cd3, wa, layer, gla_b_a, gla_o_norm)
    return _matmul_residual([y_fox.reshape(b * lp, -1), y_gla.reshape(b * lp, -1)], w_out, layer,
                            h, 1.0, ROW_TILE, 512, "cd_out")


def kernel(x, meta_tokens, ffn1_norm, ffn1_w_gate, ffn1_w_up, ffn1_w_down, mix_norm, ffn2_norm, ffn2_w_gate, ffn2_w_up, ffn2_w_down, ab_w_in, pool_w, pool_scale, mla_q_norm, mla_w_q_up, mla_kv_norm, mla_w_kv_up, mla_q_gain, mla_k_gain, ab_w_out, cd_w_in, fox_q_gain, fox_k_gain, fox_f_bias, gla_w_a2, gla_b_a, gla_o_norm, cd_w_out):
    b, s, d = x.shape
    length = N_META + s
    lp = -(-length // PAD_TO) * PAD_TO
    meta = jnp.broadcast_to(meta_tokens.astype(x.dtype)[None], (b, N_META, d))
    h = jnp.concatenate([meta, x, jnp.zeros((b, lp - length, d), x.dtype)], axis=1)
    h = h.reshape(b * lp, d)
    tables = _rotary_tables(lp)

    ffn1 = [w.astype(BF16) for w in (ffn1_w_gate, ffn1_w_up, ffn1_w_down)]
    ffn2 = [w.astype(BF16) for w in (ffn2_w_gate, ffn2_w_up, ffn2_w_down)]
    ab_in, cd_in = _ab_in_weight(ab_w_in), _cd_in_weight(cd_w_in)
    ab_out, cd_out = ab_w_out.astype(BF16), cd_w_out.astype(BF16)
    pool_wb, wq, wkv = pool_w.astype(BF16), _mla_q_weight(mla_w_q_up), mla_w_kv_up.astype(BF16)
    wa = _gla_gate_weight(gla_w_a2)

    for layer in range(ffn1_norm.shape[0]):
        i = layer // 2
        h = _ffn(h, ffn1_norm[layer], *ffn1, layer, "ffn1")
        if layer % 2 == 0:
            h = _pool_mla_layer(h, b, lp, mix_norm[layer], ab_in, pool_wb, pool_scale[i],
                                mla_q_norm[i], wq, mla_kv_norm[i], wkv, mla_q_gain[i],
                                mla_k_gain[i], ab_out, i, tables)
        else:
            h = _fox_gla_layer(h, b, lp, mix_norm[layer], cd_in, fox_q_gain[i], fox_k_gain[i],
                               fox_f_bias[i], wa, gla_b_a[i], gla_o_norm[i], cd_out, i)
        h = _ffn(h, ffn2_norm[layer], *ffn2, layer, "ffn2")
    return h.reshape(b, lp, d)[:, N_META:N_META + s]
```

```python
import functools
import math

import jax
import jax.numpy as jnp
from jax import lax
from jax.experimental import pallas as pl
from jax.experimental.pallas import tpu as pltpu

F32 = jnp.float32
BF16 = jnp.bfloat16

N_META = 16
PAD_TO = 128
RMS_EPS = 1e-6
ROPE_BASE = 10000.0
CHUNK = 64

POOL_WINDOWS = (2, 4, 8, 16)
POOL_GROUP = 512
HEADS = 16
HEAD_DIM = 128
MLA_ROPE = 64
MLA_QK = 192
GLA_HEADS = 4
GLA_DK = 256
GLA_DV = 512
GLA_TAU = 16.0
GLA_RANK = 16

VMEM_BIG = 58 * 1024 * 1024
VMEM_MID = 40 * 1024 * 1024

ROW_TILE = 768
SEQ_TILE = 384
ATT_HEADS = 4
NEG_BIG = -1e30
LOG2E = math.log2(math.e)

NT_DIMS = (((1,), (1,)), ((), ()))
TN_DIMS = (((0,), (0,)), ((), ()))


def _silu(x):
    return x / (1.0 + jnp.exp(-x))


def _log_sigmoid(x):
    return jnp.minimum(x, 0.0) - jnp.log(1.0 + jnp.exp(-jnp.abs(x)))


def _rms(x, gain, n=None):
    n = x.shape[-1] if n is None else n
    ss = jnp.sum(x * x, axis=-1, keepdims=True)
    return x * lax.rsqrt(ss / n + RMS_EPS) * gain


def _params(sem, vmem):
    return pltpu.CompilerParams(dimension_semantics=sem, vmem_limit_bytes=vmem)


def _const_spec(shape, layer=None):
    if layer is None:
        nd = len(shape)
        return pl.BlockSpec(shape, lambda *_: (0,) * nd, pipeline_mode=pl.Buffered(1))
    nd = len(shape)
    return pl.BlockSpec((None,) + tuple(shape), lambda *_: (layer,) + (0,) * nd,
                        pipeline_mode=pl.Buffered(1))


def _norm_mm_kernel(a_ref, g_ref, *rest, n_w, epilogue, rows):
    w_refs, o_ref, xn_ref = rest[:n_w], rest[n_w], rest[n_w + 1]

    @pl.when(pl.program_id(1) == 0)
    def _():
        def body(c, carry):
            r0 = pl.multiple_of(c * rows, rows)
            a = a_ref[pl.ds(r0, rows), :]
            xn_ref[pl.ds(r0, rows), :] = _rms(a, g_ref[...]).astype(BF16)
            return carry
        lax.fori_loop(0, a_ref.shape[0] // rows, body, 0)

    xn = xn_ref[...]
    outs = [jnp.dot(xn, w[...], preferred_element_type=F32) for w in w_refs]
    o_ref[...] = epilogue(*outs).astype(o_ref.dtype)


def _norm_matmul(a, gain, ws, layer, epilogue, out_dtype, tm, tn, name):
    t, k = a.shape
    n = ws[0].shape[2]
    kern = functools.partial(_norm_mm_kernel, n_w=len(ws), epilogue=epilogue, rows=64)
    return pl.pallas_call(
        kern,
        grid=(t // tm, n // tn),
        in_specs=[pl.BlockSpec((tm, k), lambda i, j: (i, 0)),
                  pl.BlockSpec((1, k), lambda i, j: (0, 0))]
                 + [pl.BlockSpec((None, k, tn), lambda i, j: (layer, 0, j)) for _ in ws],
        out_specs=pl.BlockSpec((tm, tn), lambda i, j: (i, j)),
        out_shape=jax.ShapeDtypeStruct((t, n), out_dtype),
        scratch_shapes=[pltpu.VMEM((tm, k), BF16)],
        compiler_params=_params(("parallel", "arbitrary"), VMEM_BIG),
        name=name,
    )(a, gain.reshape(1, k), *ws)


def _swiglu_epilogue(g, u):
    return _silu(g) * u


def _identity(x):
    return x


def _mm_res_kernel(*refs, k_sizes, scale):
    n_a = len(k_sizes)
    a_refs, w_ref, r_ref, o_ref = refs[:n_a], refs[n_a], refs[n_a + 1], refs[n_a + 2]
    acc, off = None, 0
    for a_ref, ks in zip(a_refs, k_sizes):
        p = jnp.dot(a_ref[...], w_ref[off:off + ks, :], preferred_element_type=F32)
        acc = p if acc is None else acc + p
        off += ks
    if scale != 1.0:
        acc = scale * acc
    o_ref[...] = r_ref[...] + acc


def _matmul_residual(a_list, w, layer, res, scale, tm, tn, name):
    t, n = res.shape
    k_sizes = tuple(a.shape[1] for a in a_list)
    k = sum(k_sizes)
    kern = functools.partial(_mm_res_kernel, k_sizes=k_sizes, scale=scale)
    n_a = len(a_list)
    return pl.pallas_call(
        kern,
        grid=(t // tm, n // tn),
        in_specs=[pl.BlockSpec((tm, ks), lambda i, j: (i, 0)) for ks in k_sizes]
                 + [pl.BlockSpec((None, k, tn), lambda i, j: (layer, 0, j)),
                    pl.BlockSpec((tm, tn), lambda i, j: (i, j))],
        out_specs=pl.BlockSpec((tm, tn), lambda i, j: (i, j)),
        out_shape=jax.ShapeDtypeStruct((t, n), F32),
        input_output_aliases={n_a + 1: 0},
        compiler_params=_params(("parallel", "arbitrary"), VMEM_BIG),
        name=name,
    )(*a_list, w, res)


def _pool_kernel(x_ref, halo_ref, w_ref, s_ref, o_ref, xs_ref, *, tiles_per_seq):
    tm = x_ref.shape[0]
    halo = POOL_WINDOWS[-1]
    it = pl.program_id(0) % tiles_per_seq
    keep = (it > 0).astype(F32)
    xs_ref[0:halo, :] = halo_ref[...] * keep
    xs_ref[halo:halo + tm, :] = x_ref[...]
    pos = it * tm + lax.broadcasted_iota(jnp.int32, (tm, 1), 0)
    for g, win in enumerate(POOL_WINDOWS):
        c0 = g * POOL_GROUP
        x = xs_ref[halo:halo + tm, c0:c0 + POOL_GROUP]
        acc = x
        for j in range(1, win):
            acc = acc + xs_ref[halo - j:halo - j + tm, c0:c0 + POOL_GROUP]
        cnt = jnp.minimum(pos + 1, win).astype(F32)
        d = acc / cnt - x
        y = jnp.dot(d.astype(BF16), w_ref[g], preferred_element_type=F32)
        o_ref[:, c0:c0 + POOL_GROUP] = (y * s_ref[:, c0:c0 + POOL_GROUP]).astype(o_ref.dtype)


def _pool_mixer(ab, pool_w, layer, pool_scale, lp):
    t = ab.shape[0]
    tm, halo = SEQ_TILE, POOL_WINDOWS[-1]
    width = len(POOL_WINDOWS) * POOL_GROUP
    per_halo = tm // halo
    kern = functools.partial(_pool_kernel, tiles_per_seq=lp // tm)
    return pl.pallas_call(
        kern,
        grid=(t // tm,),
        in_specs=[pl.BlockSpec((tm, width), lambda i: (i, 0)),
                  pl.BlockSpec((halo, width), lambda i: (jnp.maximum(i * per_halo - 1, 0), 0)),
                  _const_spec((len(POOL_WINDOWS), POOL_GROUP, POOL_GROUP), layer),
                  _const_spec((1, width))],
        out_specs=pl.BlockSpec((tm, width), lambda i: (i, 0)),
        out_shape=jax.ShapeDtypeStruct((t, width), BF16),
        scratch_shapes=[pltpu.VMEM((tm + halo, width), F32)],
        compiler_params=_params(("parallel",), VMEM_MID),
        name="pool_mixer",
    )(ab, ab, pool_w, pool_scale.reshape(1, width))


def _rotary_slot(x, c_ref, s1_ref, s2_ref):
    return (x * c_ref[...] + pltpu.roll(x, 96, axis=1) * s1_ref[...]
            + pltpu.roll(x, 32, axis=1) * s2_ref[...])


def _ones_column(rows):
    lane = lax.broadcasted_iota(jnp.int32, (rows, HEAD_DIM), 1)
    return (lane == 0).astype(BF16)


def _mla_prep_kernel(cq_ref, ckv_ref, kpe_ref, c_ref, s1_ref, s2_ref, qn_ref, kvn_ref,
                     wq_ref, wkv_ref, qgn_ref, qgr_ref, kgn_ref, kgr_ref,
                     q_ref, k_ref, v_ref, *, q_scale):
    cq = _rms(cq_ref[...], qn_ref[...]).astype(BF16)
    ckv = _rms(ckv_ref[...], kvn_ref[...]).astype(BF16)
    kr = _rotary_slot(_rms(kpe_ref[...], kgr_ref[...], MLA_ROPE), c_ref, s1_ref, s2_ref).astype(BF16)
    ones = _ones_column(cq.shape[0])
    for h in range(HEADS):
        lo = h * 2 * HEAD_DIM
        mid, hi = lo + HEAD_DIM, lo + 2 * HEAD_DIM
        q = jnp.dot(cq, wq_ref[:, lo:hi], preferred_element_type=F32)
        qn = _rms(q[:, :HEAD_DIM], qgn_ref[...])
        qr = _rotary_slot(_rms(q[:, HEAD_DIM:], qgr_ref[...], MLA_ROPE), c_ref, s1_ref, s2_ref)
        q_ref[:, lo:mid] = (qn * q_scale).astype(BF16)
        q_ref[:, mid:hi] = (qr * q_scale).astype(BF16)
        kv = jnp.dot(ckv, wkv_ref[:, lo:hi], preferred_element_type=F32)
        k_ref[:, lo:mid] = _rms(kv[:, :HEAD_DIM], kgn_ref[...]).astype(BF16)
        k_ref[:, mid:hi] = kr
        v_ref[:, lo:mid] = kv[:, HEAD_DIM:].astype(BF16)
        v_ref[:, mid:hi] = ones


def _mla_prep(ab3, tables, q_norm, kv_norm, wq, wkv, layer, q_gain, k_gain, q_scale):
    b, lp, _ = ab3.shape
    tm = SEQ_TILE
    q_rank, kv_rank = wq.shape[1], wkv.shape[1]
    cat = HEADS * 2 * HEAD_DIM
    zeros = jnp.zeros((HEAD_DIM - MLA_ROPE,), F32)
    slot = lambda g: jnp.concatenate([g[HEAD_DIM:], zeros]).reshape(1, HEAD_DIM)
    row = lambda i_, j_: (j_, 0)
    out = jax.ShapeDtypeStruct((b, lp, cat), BF16)
    return pl.pallas_call(
        functools.partial(_mla_prep_kernel, q_scale=q_scale),
        grid=(b, lp // tm),
        in_specs=[pl.BlockSpec((None, tm, q_rank), lambda i, j: (i, j, 2048 // q_rank)),
                  pl.BlockSpec((None, tm, kv_rank), lambda i, j: (i, j, 3072 // kv_rank)),
                  pl.BlockSpec((None, tm, HEAD_DIM), lambda i, j: (i, j, 3584 // HEAD_DIM)),
                  pl.BlockSpec((tm, HEAD_DIM), row),
                  pl.BlockSpec((tm, HEAD_DIM), row),
                  pl.BlockSpec((tm, HEAD_DIM), row),
                  _const_spec((1, q_rank)), _const_spec((1, kv_rank)),
                  _const_spec((q_rank, cat), layer), _const_spec((kv_rank, cat), layer),
                  _const_spec((1, HEAD_DIM)), _const_spec((1, HEAD_DIM)),
                  _const_spec((1, HEAD_DIM)), _const_spec((1, HEAD_DIM))],
        out_specs=[pl.BlockSpec((None, tm, cat), lambda i, j: (i, j, 0))] * 3,
        out_shape=[out, out, out],
        compiler_params=_params(("parallel", "parallel"), VMEM_BIG),
        name="mla_prep",
    )(ab3, ab3, ab3, *tables, q_norm.reshape(1, -1), kv_norm.reshape(1, -1), wq, wkv,
      q_gain[:HEAD_DIM].reshape(1, HEAD_DIM), slot(q_gain),
      k_gain[:HEAD_DIM].reshape(1, HEAD_DIM), slot(k_gain))


def _softmax_sweep(qs, key_fn, val_fn, bias_fn, n_full, tq, diag_mask, tail, scratch):
    nh = len(qs)
    s_even, s_odd, m_ref, acc_ref = scratch

    def scores(h, start, width):
        s = lax.dot_general(qs[h], key_fn(h, start, width), NT_DIMS, preferred_element_type=F32)
        return bias_fn(h, s, start, width)

    def lanes(x, width):
        return jnp.concatenate([x] * (width // PAD_TO), axis=1)

    def absorb(h, s, vals):
        m = m_ref[h]
        m_new = jnp.maximum(m, jnp.max(s, axis=-1, keepdims=True))
        p = jnp.exp2(s - lanes(m_new, s.shape[1])).astype(BF16)
        pv = jnp.dot(p, vals, preferred_element_type=F32)
        acc_ref[h] = lanes(jnp.exp2(m - m_new), 2 * HEAD_DIM) * acc_ref[h] + pv
        m_ref[h] = m_new

    def half_step(c, cur, nxt, diagonal):
        start = pl.multiple_of(c * tq, tq)
        for h in range(nh):
            if not diagonal:
                nxt[h] = scores(h, start + tq, tq)
                absorb(h, cur[h], val_fn(h, start, tq))
                continue
            s = jnp.where(diag_mask(), cur[h], NEG_BIG)
            vals = val_fn(h, start, tq)
            if tail is not None:
                t_start, t_width, t_mask = tail
                s_tail = jnp.where(t_mask(), scores(h, t_start, t_width), NEG_BIG)
                s = jnp.concatenate([s, s_tail], axis=1)
                vals = jnp.concatenate([vals, val_fn(h, t_start, t_width)], axis=0)
            absorb(h, s, vals)

    def by_parity(c, diagonal):
        return lambda: lax.cond(c % 2 == 0, lambda: half_step(c, s_even, s_odd, diagonal),
                                lambda: half_step(c, s_odd, s_even, diagonal))

    def step(c, carry):
        lax.cond(c == n_full, by_parity(c, True), by_parity(c, False))
        return carry

    for h in range(nh):
        m_ref[h] = jnp.full((tq, PAD_TO), NEG_BIG, F32)
        acc_ref[h] = jnp.zeros((tq, 2 * HEAD_DIM), F32)
        s_even[h] = scores(h, 0, tq)
    lax.fori_loop(0, n_full + 1, step, 0)
    outs = []
    for h in range(nh):
        acc = acc_ref[h]
        outs.append(acc[:, :HEAD_DIM] / acc[:, HEAD_DIM:HEAD_DIM + 1])
    return outs


def _sweep_scratch(tq):
    return [pltpu.VMEM((ATT_HEADS, tq, tq), F32), pltpu.VMEM((ATT_HEADS, tq, tq), F32),
            pltpu.VMEM((ATT_HEADS, tq, PAD_TO), F32), pltpu.VMEM((ATT_HEADS, tq, 2 * HEAD_DIM), F32)]


def _mla_attn_kernel(q_ref, k_ref, v_ref, o_ref, *scratch):
    tq, lk = q_ref.shape[0], k_ref.shape[0]
    width = 2 * HEAD_DIM
    i = pl.program_id(2)
    q0 = i * tq
    shift = CHUNK - N_META

    def chunk_mask(k_start, k_width):
        qc = (q0 + shift + lax.broadcasted_iota(jnp.int32, (tq, k_width), 0)) // CHUNK
        kc = (k_start + shift + lax.broadcasted_iota(jnp.int32, (tq, k_width), 1)) // CHUNK
        return kc <= qc

    t_start = pl.multiple_of(jnp.minimum(q0 + tq, lk - PAD_TO), PAD_TO)

    def t_mask():
        t_kpos = t_start + lax.broadcasted_iota(jnp.int32, (tq, PAD_TO), 1)
        return (t_kpos >= q0 + tq) & chunk_mask(t_start, PAD_TO)

    qs = [q_ref[:, h * width:(h + 1) * width] for h in range(ATT_HEADS)]
    outs = _softmax_sweep(
        qs,
        lambda h, s0, w: k_ref[pl.ds(s0, w), h * width:(h + 1) * width],
        lambda h, s0, w: v_ref[pl.ds(s0, w), h * width:(h + 1) * width],
        lambda h, s, s0, w: s,
        i, tq, lambda: chunk_mask(q0, tq), (t_start, PAD_TO, t_mask), scratch)
    for h in range(ATT_HEADS):
        o_ref[:, h * HEAD_DIM:(h + 1) * HEAD_DIM] = outs[h].astype(o_ref.dtype)


def _mla_attention(q, k, v):
    b, lp, _ = q.shape
    tq = SEQ_TILE
    gw = ATT_HEADS * 2 * HEAD_DIM
    resident = pl.BlockSpec((None, lp, gw), lambda bi, g, i: (bi, 0, g))
    return pl.pallas_call(
        _mla_attn_kernel,
        grid=(b, HEADS // ATT_HEADS, lp // tq),
        in_specs=[pl.BlockSpec((None, tq, gw), lambda bi, g, i: (bi, i, g)), resident, resident],
        out_specs=pl.BlockSpec((None, tq, ATT_HEADS * HEAD_DIM), lambda bi, g, i: (bi, i, g)),
        out_shape=jax.ShapeDtypeStruct((b, lp, HEADS * HEAD_DIM), BF16),
        scratch_shapes=_sweep_scratch(tq),
        compiler_params=_params(("parallel", "parallel", "arbitrary"), VMEM_BIG),
        name="mla_attention",
    )(q, k, v)


def _fox_attn_kernel(q_ref, k_ref, v_ref, f_ref, qg_ref, kg_ref, o_ref, kn_ref, va_ref, fb_ref,
                     *scratch, q_scale):
    tq, lk = q_ref.shape[0], k_ref.shape[0]
    i = pl.program_id(2)
    q0 = pl.multiple_of(i * tq, tq)
    heads = [slice(h * HEAD_DIM, (h + 1) * HEAD_DIM) for h in range(ATT_HEADS)]

    @pl.when(i == 0)
    def _():
        ones = _ones_column(tq)

        def body(c, carry):
            rows = pl.ds(pl.multiple_of(c * tq, tq), tq)
            for h, sl in enumerate(heads):
                kn_ref[rows, sl] = _rms(k_ref[rows, sl], kg_ref[...]).astype(BF16)
                va_ref[h, rows, 0:HEAD_DIM] = v_ref[rows, sl].astype(BF16)
                va_ref[h, rows, HEAD_DIM:2 * HEAD_DIM] = ones
            return carry
        lax.fori_loop(0, lk // tq, body, 0)
        fb_ref[...] = f_ref[...] * LOG2E

    qs = [(_rms(q_ref[:, sl], qg_ref[...]) * q_scale).astype(BF16) for sl in heads]
    f0 = [jnp.max(fb_ref[h, :, pl.ds(q0, tq)], axis=-1, keepdims=True) for h in range(ATT_HEADS)]

    def causal():
        return (lax.broadcasted_iota(jnp.int32, (tq, tq), 1)
                <= lax.broadcasted_iota(jnp.int32, (tq, tq), 0))

    outs = _softmax_sweep(
        qs,
        lambda h, s0, w: kn_ref[pl.ds(s0, w), heads[h]],
        lambda h, s0, w: va_ref[h, pl.ds(s0, w), :],
        lambda h, s, s0, w: s - (fb_ref[h, :, pl.ds(s0, w)] - f0[h]),
        i, tq, causal, None, scratch)
    for h, sl in enumerate(heads):
        o_ref[:, sl] = outs[h].astype(o_ref.dtype)


def _fox_attention(cd3, f, q_gain, k_gain, q_scale):
    b, lp, _ = cd3.shape
    tq = SEQ_TILE
    gw = ATT_HEADS * HEAD_DIM
    groups = HEADS // ATT_HEADS
    kern = functools.partial(_fox_attn_kernel, q_scale=q_scale)
    return pl.pallas_call(
        kern,
        grid=(b, groups, lp // tq),
        in_specs=[pl.BlockSpec((None, tq, gw), lambda bi, g, i: (bi, i, g)),
                  pl.BlockSpec((None, lp, gw), lambda bi, g, i: (bi, 0, groups + g),
                               pipeline_mode=pl.Buffered(1)),
                  pl.BlockSpec((None, lp, gw), lambda bi, g, i: (bi, 0, 2 * groups + g),
                               pipeline_mode=pl.Buffered(1)),
                  pl.BlockSpec((None, ATT_HEADS, 1, lp), lambda bi, g, i: (bi, g, 0, 0)),
                  _const_spec((1, HEAD_DIM)), _const_spec((1, HEAD_DIM))],
        out_specs=pl.BlockSpec((None, tq, gw), lambda bi, g, i: (bi, i, g)),
        out_shape=jax.ShapeDtypeStruct((b, lp, HEADS * HEAD_DIM), BF16),
        scratch_shapes=[pltpu.VMEM((lp, gw), BF16),
                        pltpu.VMEM((ATT_HEADS, lp, 2 * HEAD_DIM), BF16),
                        pltpu.VMEM((ATT_HEADS, 1, lp), F32)] + _sweep_scratch(tq),
        compiler_params=_params(("parallel", "parallel", "arbitrary"), VMEM_BIG),
        name="fox_attention",
    )(cd3, cd3, cd3, f, q_gain.reshape(1, HEAD_DIM), k_gain.reshape(1, HEAD_DIM))


def _forget_cumsum_kernel(x_ref, b_ref, o_ref):
    rows, lp = x_ref.shape
    lane = lax.broadcasted_iota(jnp.int32, (rows, PAD_TO), 1)
    carry = jnp.zeros((rows, 1), F32)
    for c in range(lp // PAD_TO):
        sl = slice(c * PAD_TO, (c + 1) * PAD_TO)
        x = _log_sigmoid(x_ref[:, sl] + b_ref[...])
        step = 1
        while step < PAD_TO:
            x = x + jnp.where(lane >= step, pltpu.roll(x, step, axis=1), 0.0)
            step *= 2
        x = x + carry
        o_ref[:, sl] = x
        carry = x[:, PAD_TO - 1:PAD_TO]


def _forget_cumsum(ff_t, bias_col):
    rows, lp = ff_t.shape
    return pl.pallas_call(
        _forget_cumsum_kernel,
        out_shape=jax.ShapeDtypeStruct((rows, lp), F32),
        name="fox_forget_cumsum",
    )(ff_t, bias_col)


def _cumsum_rows(x):
    row = lax.broadcasted_iota(jnp.int32, x.shape, 0)
    step = 1
    while step < x.shape[0]:
        x = x + jnp.where(row >= step, pltpu.roll(x, step, axis=0), 0.0)
        step *= 2
    return x


def _gla_kernel(q_ref, k_ref, v_ref, r_ref, a_ref, wa_ref, ba_ref, on_ref, o_ref, st_ref):
    tm = q_ref.shape[0]

    @pl.when(pl.program_id(1) == 0)
    def _():
        st_ref[...] = jnp.zeros_like(st_ref)

    causal = (lax.broadcasted_iota(jnp.int32, (CHUNK, CHUNK), 1)
              <= lax.broadcasted_iota(jnp.int32, (CHUNK, CHUNK), 0))

    def body(c, carry):
        r0 = pl.multiple_of(c * CHUNK, CHUNK)
        rows = pl.ds(r0, CHUNK)
        gate = jnp.dot(a_ref[rows, :].astype(BF16), wa_ref[...], preferred_element_type=F32)
        for h in range(GLA_HEADS):
            ks = slice(h * GLA_DK, (h + 1) * GLA_DK)
            vs = slice(h * GLA_DV, (h + 1) * GLA_DV)
            g = _log_sigmoid(gate[:, ks] + ba_ref[:, ks]) / GLA_TAU
            bcum = _cumsum_rows(g)
            b_last = bcum[CHUNK - 1:CHUNK, :]
            k = k_ref[rows, ks]
            v = v_ref[rows, vs].astype(BF16)
            q_dec = ((q_ref[rows, ks] * (GLA_DK ** -0.5)) * jnp.exp(bcum)).astype(BF16)
            k_inv = (k * jnp.exp(-bcum)).astype(BF16)
            k_end = (k * jnp.exp(b_last - bcum)).astype(BF16)
            a = lax.dot_general(q_dec, k_inv, NT_DIMS, preferred_element_type=F32)
            a = jnp.where(causal, a, 0.0).astype(BF16)
            st = st_ref[h]
            o = (jnp.dot(a, v, preferred_element_type=F32)
                 + lax.dot_general(q_dec, st.astype(BF16), NT_DIMS, preferred_element_type=F32))
            st_ref[h] = st * jnp.exp(b_last) + lax.dot_general(v, k_end, TN_DIMS,
                                                               preferred_element_type=F32)
            o = _rms(o, on_ref[...])
            o_ref[rows, vs] = (o * _silu(r_ref[rows, vs])).astype(o_ref.dtype)
        return carry

    lax.fori_loop(0, tm // CHUNK, body, 0)


def _gla(cd3, wa, layer, ba, o_norm):
    b, lp, _ = cd3.shape
    tm = SEQ_TILE
    kw, vw = GLA_HEADS * GLA_DK, GLA_HEADS * GLA_DV
    return pl.pallas_call(
        _gla_kernel,
        grid=(b, lp // tm),
        in_specs=[pl.BlockSpec((None, tm, kw), lambda bi, i: (bi, i, 6144 // kw)),
                  pl.BlockSpec((None, tm, kw), lambda bi, i: (bi, i, 7168 // kw)),
                  pl.BlockSpec((None, tm, vw), lambda bi, i: (bi, i, 8192 // vw)),
                  pl.BlockSpec((None, tm, vw), lambda bi, i: (bi, i, 10240 // vw)),
                  pl.BlockSpec((None, tm, PAD_TO), lambda bi, i: (bi, i, 12288 // PAD_TO)),
                  _const_spec((PAD_TO, kw), layer), _const_spec((1, kw)), _const_spec((1, GLA_DV))],
        out_specs=pl.BlockSpec((None, tm, vw), lambda bi, i: (bi, i, 0)),
        out_shape=jax.ShapeDtypeStruct((b, lp, vw), BF16),
        scratch_shapes=[pltpu.VMEM((GLA_HEADS, GLA_DV, GLA_DK), F32)],
        compiler_params=_params(("parallel", "arbitrary"), VMEM_MID),
        name="gla",
    )(cd3, cd3, cd3, cd3, cd3, wa, ba.reshape(1, -1), o_norm.reshape(1, -1))


def _ab_in_weight(w):
    w = w.astype(BF16)
    return jnp.concatenate([w, jnp.zeros(w.shape[:2] + (3840 - w.shape[2],), BF16)], axis=2)


def _cd_repack_kernel(w_ref, o_ref):
    o_ref[:, 0:6144] = w_ref[:, 0:6144].astype(BF16)
    o_ref[:, 6144:10240] = w_ref[:, 6160:10256].astype(BF16)
    o_ref[:, 10240:12288] = w_ref[:, 10272:12320].astype(BF16)
    o_ref[:, 12288:12800] = jnp.zeros((o_ref.shape[0], 512), BF16)
    o_ref[:, 12288:12304] = w_ref[:, 6144:6160].astype(BF16)
    o_ref[:, 12304:12320] = w_ref[:, 10256:10272].astype(BF16)


def _cd_in_weight(w):
    n, k, cols = w.shape
    tr = 128
    return pl.pallas_call(
        _cd_repack_kernel,
        grid=(n, k // tr),
        in_specs=[pl.BlockSpec((None, tr, cols), lambda l, i: (l, i, 0))],
        out_specs=pl.BlockSpec((None, tr, 12800), lambda l, i: (l, i, 0)),
        out_shape=jax.ShapeDtypeStruct((n, k, 12800), BF16),
        compiler_params=_params(("parallel", "parallel"), VMEM_MID),
        name="cd_weight_repack",
    )(w)


def _mla_q_weight(w):
    n, r, _ = w.shape
    w = jnp.pad(w.astype(BF16).reshape(n, r, HEADS, MLA_QK),
                ((0, 0), (0, 0), (0, 0), (0, 2 * HEAD_DIM - MLA_QK)))
    return w.reshape(n, r, HEADS * 2 * HEAD_DIM)


def _gla_gate_weight(w):
    return jnp.pad(w.astype(BF16), ((0, 0), (GLA_RANK, PAD_TO - 2 * GLA_RANK), (0, 0)))


def _rotary_tables(lp):
    pos = jnp.arange(lp, dtype=F32)
    inv_freq = ROPE_BASE ** (-jnp.arange(0, MLA_ROPE, 2, dtype=F32) / MLA_ROPE)
    ang = pos[:, None] * inv_freq[None, :]
    cos, sin = jnp.cos(ang), jnp.sin(ang)
    z32, z64 = jnp.zeros_like(cos), jnp.zeros((lp, HEAD_DIM - MLA_ROPE), F32)
    return (jnp.concatenate([cos, cos, z64], axis=1),
            jnp.concatenate([-sin, z32, z64], axis=1),
            jnp.concatenate([z32, sin, z64], axis=1))


def _ffn(h, norm, w_gate, w_up, w_down, layer, name):
    hidden = _norm_matmul(h, norm, [w_gate, w_up], layer, _swiglu_epilogue, BF16, ROW_TILE, 512,
                          name + "_up")
    return _matmul_residual([hidden], w_down, layer, h, 0.5, ROW_TILE, 512, name + "_down")


def _pool_mla_layer(h, b, lp, norm, w_in, pool_w, pool_scale, q_norm, wq, kv_norm, wkv,
                    q_gain, k_gain, w_out, layer, tables):
    ab = _norm_matmul(h, norm, [w_in], layer, _identity, F32, ROW_TILE, 768, "ab_in")
    y_pool = _pool_mixer(ab, pool_w, layer, pool_scale, lp)
    q, k, v = _mla_prep(ab.reshape(b, lp, -1), tables, q_norm, kv_norm, wq, wkv, layer,
                        q_gain, k_gain, MLA_QK ** -0.5 * LOG2E)
    y_mla = _mla_attention(q, k, v)
    return _matmul_residual([y_pool, y_mla.reshape(b * lp, -1)], w_out, layer, h, 1.0,
                            ROW_TILE, 512, "ab_out")


def _fox_gla_layer(h, b, lp, norm, w_in, fox_q_gain, fox_k_gain, fox_f_bias, wa, gla_b_a,
                   gla_o_norm, w_out, layer):
    cd = _norm_matmul(h, norm, [w_in], layer, _identity, F32, ROW_TILE, 512, "cd_in")
    cd3 = cd.reshape(b, lp, -1)
    ff_t = cd[:, 12288:12288 + HEADS].reshape(b, lp, HEADS).transpose(0, 2, 1).reshape(b * HEADS, lp)
    f = _forget_cumsum(ff_t, jnp.tile(fox_f_bias, b).reshape(b * HEADS, 1))
    y_fox = _fox_attention(cd3, f.reshape(b, HEADS, 1, lp), fox_q_gain, fox_k_gain,
                           HEAD_DIM ** -0.5 * LOG2E)
    y_gla = _gla(cd3, wa, layer, gla_b_a, gla_o_norm)
    return _matmul_residual([y_fox.reshape(b * lp, -1), y_gla.reshape(b * lp, -1)], w_out, layer,
                            h, 1.0, ROW_TILE, 512, "cd_out")


def kernel(x, meta_tokens, ffn1_norm, ffn1_w_gate, ffn1_w_up, ffn1_w_down, mix_norm, ffn2_norm, ffn2_w_gate, ffn2_w_up, ffn2_w_down, ab_w_in, pool_w, pool_scale, mla_q_norm, mla_w_q_up, mla_kv_norm, mla_w_kv_up, mla_q_gain, mla_k_gain, ab_w_out, cd_w_in, fox_q_gain, fox_k_gain, fox_f_bias, gla_w_a2, gla_b_a, gla_o_norm, cd_w_out):
    b, s, d = x.shape
    length = N_META + s
    lp = -(-length // PAD_TO) * PAD_TO
    meta = jnp.broadcast_to(meta_tokens.astype(x.dtype)[None], (b, N_META, d))
    h = jnp.concatenate([meta, x, jnp.zeros((b, lp - length, d), x.dtype)], axis=1)
    h = h.reshape(b * lp, d)
    tables = _rotary_tables(lp)

    ffn1 = [w.astype(BF16) for w in (ffn1_w_gate, ffn1_w_up, ffn1_w_down)]
    ffn2 = [w.astype(BF16) for w in (ffn2_w_gate, ffn2_w_up, ffn2_w_down)]
    ab_in, cd_in = _ab_in_weight(ab_w_in), _cd_in_weight(cd_w_in)
    ab_out, cd_out = ab_w_out.astype(BF16), cd_w_out.astype(BF16)
    pool_wb, wq, wkv = pool_w.astype(BF16), _mla_q_weight(mla_w_q_up), mla_w_kv_up.astype(BF16)
    wa = _gla_gate_weight(gla_w_a2)

    for layer in range(ffn1_norm.shape[0]):
        i = layer // 2
        h = _ffn(h, ffn1_norm[layer], *ffn1, layer, "ffn1")
        if layer % 2 == 0:
            h = _pool_mla_layer(h, b, lp, mix_norm[layer], ab_in, pool_wb, pool_scale[i],
                                mla_q_norm[i], wq, mla_kv_norm[i], wkv, mla_q_gain[i],
                                mla_k_gain[i], ab_out, i, tables)
        else:
            h = _fox_gla_layer(h, b, lp, mix_norm[layer], cd_in, fox_q_gain[i], fox_k_gain[i],
                               fox_f_bias[i], wa, gla_b_a[i], gla_o_norm[i], cd_out, i)
        h = _ffn(h, ffn2_norm[layer], *ffn2, layer, "ffn2")
    return h.reshape(b, lp, d)[:, N_META:N_META + s]
```

```python
import functools
import math

import jax
import jax.numpy as jnp
from jax import lax
from jax.experimental import pallas as pl
from jax.experimental.pallas import tpu as pltpu

F32 = jnp.float32
BF16 = jnp.bfloat16

N_META = 16
PAD_TO = 128
RMS_EPS = 1e-6
ROPE_BASE = 10000.0
CHUNK = 64

POOL_WINDOWS = (2, 4, 8, 16)
POOL_GROUP = 512
HEADS = 16
HEAD_DIM = 128
MLA_ROPE = 64
MLA_QK = 192
GLA_HEADS = 4
GLA_DK = 256
GLA_DV = 512
GLA_TAU = 16.0
GLA_RANK = 16

VMEM_BIG = 58 * 1024 * 1024
VMEM_MID = 40 * 1024 * 1024

ROW_TILE = 768
SEQ_TILE = 384
ATT_HEADS = 4
NEG_BIG = -1e30
LOG2E = math.log2(math.e)

NT_DIMS = (((1,), (1,)), ((), ()))
TN_DIMS = (((0,), (0,)), ((), ()))


def _silu(x):
    return x / (1.0 + jnp.exp(-x))


def _log_sigmoid(x):
    return jnp.minimum(x, 0.0) - jnp.log(1.0 + jnp.exp(-jnp.abs(x)))


def _rms(x, gain, n=None):
    n = x.shape[-1] if n is None else n
    ss = jnp.sum(x * x, axis=-1, keepdims=True)
    return x * lax.rsqrt(ss / n + RMS_EPS) * gain


def _params(sem, vmem):
    return pltpu.CompilerParams(dimension_semantics=sem, vmem_limit_bytes=vmem)


def _const_spec(shape, layer=None):
    if layer is None:
        nd = len(shape)
        return pl.BlockSpec(shape, lambda *_: (0,) * nd, pipeline_mode=pl.Buffered(1))
    nd = len(shape)
    return pl.BlockSpec((None,) + tuple(shape), lambda *_: (layer,) + (0,) * nd,
                        pipeline_mode=pl.Buffered(1))


def _norm_mm_kernel(a_ref, g_ref, *rest, n_w, epilogue, rows, w_transposed):
    w_refs, o_ref, xn_ref = rest[:n_w], rest[n_w], rest[n_w + 1]
    dims = NT_DIMS if w_transposed else (((1,), (0,)), ((), ()))

    @pl.when(pl.program_id(1) == 0)
    def _():
        def body(c, carry):
            r0 = pl.multiple_of(c * rows, rows)
            a = a_ref[pl.ds(r0, rows), :]
            xn_ref[pl.ds(r0, rows), :] = _rms(a, g_ref[...]).astype(BF16)
            return carry
        lax.fori_loop(0, a_ref.shape[0] // rows, body, 0)

    xn = xn_ref[...]
    outs = [lax.dot_general(xn, w[...], dims, preferred_element_type=F32) for w in w_refs]
    o_ref[...] = epilogue(*outs).astype(o_ref.dtype)


def _norm_matmul(a, gain, ws, layer, epilogue, out_dtype, tm, tn, name, w_transposed=False):
    t, k = a.shape
    n = ws[0].shape[1 if w_transposed else 2]
    kern = functools.partial(_norm_mm_kernel, n_w=len(ws), epilogue=epilogue, rows=64,
                             w_transposed=w_transposed)
    if w_transposed:
        w_spec = pl.BlockSpec((None, tn, k), lambda i, j: (layer, j, 0))
    else:
        w_spec = pl.BlockSpec((None, k, tn), lambda i, j: (layer, 0, j))
    return pl.pallas_call(
        kern,
        grid=(t // tm, n // tn),
        in_specs=[pl.BlockSpec((tm, k), lambda i, j: (i, 0)),
                  pl.BlockSpec((1, k), lambda i, j: (0, 0))] + [w_spec for _ in ws],
        out_specs=pl.BlockSpec((tm, tn), lambda i, j: (i, j)),
        out_shape=jax.ShapeDtypeStruct((t, n), out_dtype),
        scratch_shapes=[pltpu.VMEM((tm, k), BF16)],
        compiler_params=_params(("parallel", "arbitrary"), VMEM_BIG),
        name=name,
    )(a, gain.reshape(1, k), *ws)


def _swiglu_epilogue(g, u):
    return _silu(g) * u


def _identity(x):
    return x


def _mm_res_kernel(*refs, k_sizes, scale):
    n_a = len(k_sizes)
    a_refs, w_ref, r_ref, o_ref = refs[:n_a], refs[n_a], refs[n_a + 1], refs[n_a + 2]
    acc, off = None, 0
    for a_ref, ks in zip(a_refs, k_sizes):
        p = jnp.dot(a_ref[...], w_ref[off:off + ks, :], preferred_element_type=F32)
        acc = p if acc is None else acc + p
        off += ks
    if scale != 1.0:
        acc = scale * acc
    o_ref[...] = r_ref[...] + acc


def _matmul_residual(a_list, w, layer, res, scale, tm, tn, name):
    t, n = res.shape
    k_sizes = tuple(a.shape[1] for a in a_list)
    k = sum(k_sizes)
    kern = functools.partial(_mm_res_kernel, k_sizes=k_sizes, scale=scale)
    n_a = len(a_list)
    return pl.pallas_call(
        kern,
        grid=(t // tm, n // tn),
        in_specs=[pl.BlockSpec((tm, ks), lambda i, j: (i, 0)) for ks in k_sizes]
                 + [pl.BlockSpec((None, k, tn), lambda i, j: (layer, 0, j)),
                    pl.BlockSpec((tm, tn), lambda i, j: (i, j))],
        out_specs=pl.BlockSpec((tm, tn), lambda i, j: (i, j)),
        out_shape=jax.ShapeDtypeStruct((t, n), F32),
        input_output_aliases={n_a + 1: 0},
        compiler_params=_params(("parallel", "arbitrary"), VMEM_BIG),
        name=name,
    )(*a_list, w, res)


def _pool_kernel(x_ref, halo_ref, w_ref, s_ref, o_ref, xs_ref, *, tiles_per_seq):
    tm = x_ref.shape[0]
    halo = POOL_WINDOWS[-1]
    it = pl.program_id(0) % tiles_per_seq
    keep = (it > 0).astype(F32)
    xs_ref[0:halo, :] = halo_ref[...] * keep
    xs_ref[halo:halo + tm, :] = x_ref[...]
    pos = it * tm + lax.broadcasted_iota(jnp.int32, (tm, 1), 0)
    for g, win in enumerate(POOL_WINDOWS):
        c0 = g * POOL_GROUP
        x = xs_ref[halo:halo + tm, c0:c0 + POOL_GROUP]
        acc = x
        for j in range(1, win):
            acc = acc + xs_ref[halo - j:halo - j + tm, c0:c0 + POOL_GROUP]
        cnt = jnp.minimum(pos + 1, win).astype(F32)
        d = acc / cnt - x
        y = jnp.dot(d.astype(BF16), w_ref[g], preferred_element_type=F32)
        o_ref[:, c0:c0 + POOL_GROUP] = (y * s_ref[:, c0:c0 + POOL_GROUP]).astype(o_ref.dtype)


def _pool_mixer(ab, pool_w, layer, pool_scale, lp):
    t = ab.shape[0]
    tm, halo = SEQ_TILE, POOL_WINDOWS[-1]
    width = len(POOL_WINDOWS) * POOL_GROUP
    per_halo = tm // halo
    kern = functools.partial(_pool_kernel, tiles_per_seq=lp // tm)
    return pl.pallas_call(
        kern,
        grid=(t // tm,),
        in_specs=[pl.BlockSpec((tm, width), lambda i: (i, 0)),
                  pl.BlockSpec((halo, width), lambda i: (jnp.maximum(i * per_halo - 1, 0), 0)),
                  _const_spec((len(POOL_WINDOWS), POOL_GROUP, POOL_GROUP), layer),
                  _const_spec((1, width))],
        out_specs=pl.BlockSpec((tm, width), lambda i: (i, 0)),
        out_shape=jax.ShapeDtypeStruct((t, width), BF16),
        scratch_shapes=[pltpu.VMEM((tm + halo, width), F32)],
        compiler_params=_params(("parallel",), VMEM_MID),
        name="pool_mixer",
    )(ab, ab, pool_w, pool_scale.reshape(1, width))


def _rotary_slot(x, c_ref, s1_ref, s2_ref):
    return (x * c_ref[...] + pltpu.roll(x, 96, axis=1) * s1_ref[...]
            + pltpu.roll(x, 32, axis=1) * s2_ref[...])


def _ones_column(rows):
    lane = lax.broadcasted_iota(jnp.int32, (rows, HEAD_DIM), 1)
    return (lane == 0).astype(BF16)


def _mla_prep_kernel(cq_ref, ckv_ref, kpe_ref, c_ref, s1_ref, s2_ref, qn_ref, kvn_ref,
                     wq_ref, wkv_ref, qgn_ref, qgr_ref, kgn_ref, kgr_ref,
                     q_ref, k_ref, v_ref, *, q_scale):
    cq = _rms(cq_ref[...], qn_ref[...]).astype(BF16)
    ckv = _rms(ckv_ref[...], kvn_ref[...]).astype(BF16)
    kr = _rotary_slot(_rms(kpe_ref[...], kgr_ref[...], MLA_ROPE), c_ref, s1_ref, s2_ref).astype(BF16)
    ones = _ones_column(cq.shape[0])
    for h in range(HEADS):
        lo = h * 2 * HEAD_DIM
        mid, hi = lo + HEAD_DIM, lo + 2 * HEAD_DIM
        q = jnp.dot(cq, wq_ref[:, lo:hi], preferred_element_type=F32)
        qn = _rms(q[:, :HEAD_DIM], qgn_ref[...])
        qr = _rotary_slot(_rms(q[:, HEAD_DIM:], qgr_ref[...], MLA_ROPE), c_ref, s1_ref, s2_ref)
        q_ref[:, lo:mid] = (qn * q_scale).astype(BF16)
        q_ref[:, mid:hi] = (qr * q_scale).astype(BF16)
        kv = jnp.dot(ckv, wkv_ref[:, lo:hi], preferred_element_type=F32)
        k_ref[:, lo:mid] = _rms(kv[:, :HEAD_DIM], kgn_ref[...]).astype(BF16)
        k_ref[:, mid:hi] = kr
        v_ref[:, lo:mid] = kv[:, HEAD_DIM:].astype(BF16)
        v_ref[:, mid:hi] = ones


def _mla_prep(ab3, tables, q_norm, kv_norm, wq, wkv, layer, q_gain, k_gain, q_scale):
    b, lp, _ = ab3.shape
    tm = SEQ_TILE
    q_rank, kv_rank = wq.shape[1], wkv.shape[1]
    cat = HEADS * 2 * HEAD_DIM
    zeros = jnp.zeros((HEAD_DIM - MLA_ROPE,), F32)
    slot = lambda g: jnp.concatenate([g[HEAD_DIM:], zeros]).reshape(1, HEAD_DIM)
    row = lambda i_, j_: (j_, 0)
    out = jax.ShapeDtypeStruct((b, lp, cat), BF16)
    return pl.pallas_call(
        functools.partial(_mla_prep_kernel, q_scale=q_scale),
        grid=(b, lp // tm),
        in_specs=[pl.BlockSpec((None, tm, q_rank), lambda i, j: (i, j, 2048 // q_rank)),
                  pl.BlockSpec((None, tm, kv_rank), lambda i, j: (i, j, 3072 // kv_rank)),
                  pl.BlockSpec((None, tm, HEAD_DIM), lambda i, j: (i, j, 3584 // HEAD_DIM)),
                  pl.BlockSpec((tm, HEAD_DIM), row),
                  pl.BlockSpec((tm, HEAD_DIM), row),
                  pl.BlockSpec((tm, HEAD_DIM), row),
                  _const_spec((1, q_rank)), _const_spec((1, kv_rank)),
                  _const_spec((q_rank, cat), layer), _const_spec((kv_rank, cat), layer),
                  _const_spec((1, HEAD_DIM)), _const_spec((1, HEAD_DIM)),
                  _const_spec((1, HEAD_DIM)), _const_spec((1, HEAD_DIM))],
        out_specs=[pl.BlockSpec((None, tm, cat), lambda i, j: (i, j, 0))] * 3,
        out_shape=[out, out, out],
        compiler_params=_params(("parallel", "parallel"), VMEM_BIG),
        name="mla_prep",
    )(ab3, ab3, ab3, *tables, q_norm.reshape(1, -1), kv_norm.reshape(1, -1), wq, wkv,
      q_gain[:HEAD_DIM].reshape(1, HEAD_DIM), slot(q_gain),
      k_gain[:HEAD_DIM].reshape(1, HEAD_DIM), slot(k_gain))


def _softmax_sweep(qs, key_fn, val_fn, bias_fn, n_full, tq, diag_mask, tail, scratch):
    nh = len(qs)
    s_even, s_odd, m_ref, acc_ref = scratch

    def scores(h, start, width):
        s = lax.dot_general(qs[h], key_fn(h, start, width), NT_DIMS, preferred_element_type=F32)
        return bias_fn(h, s, start, width)

    def lanes(x, width):
        return jnp.concatenate([x] * (width // PAD_TO), axis=1)

    def absorb(h, s, vals):
        m = m_ref[h]
        m_new = jnp.maximum(m, jnp.max(s, axis=-1, keepdims=True))
        p = jnp.exp2(s - lanes(m_new, s.shape[1])).astype(BF16)
        pv = jnp.dot(p, vals, preferred_element_type=F32)
        acc_ref[h] = lanes(jnp.exp2(m - m_new), 2 * HEAD_DIM) * acc_ref[h] + pv
        m_ref[h] = m_new

    def half_step(c, cur, nxt, diagonal):
        start = pl.multiple_of(c * tq, tq)
        for h in range(nh):
            if not diagonal:
                nxt[h] = scores(h, start + tq, tq)
                absorb(h, cur[h], val_fn(h, start, tq))
                continue
            s = jnp.where(diag_mask(), cur[h], NEG_BIG)
            vals = val_fn(h, start, tq)
            if tail is not None:
                t_start, t_width, t_mask = tail
                s_tail = jnp.where(t_mask(), scores(h, t_start, t_width), NEG_BIG)
                s = jnp.concatenate([s, s_tail], axis=1)
                vals = jnp.concatenate([vals, val_fn(h, t_start, t_width)], axis=0)
            absorb(h, s, vals)

    def by_parity(c, diagonal):
        return lambda: lax.cond(c % 2 == 0, lambda: half_step(c, s_even, s_odd, diagonal),
                                lambda: half_step(c, s_odd, s_even, diagonal))

    def step(c, carry):
        lax.cond(c == n_full, by_parity(c, True), by_parity(c, False))
        return carry

    for h in range(nh):
        m_ref[h] = jnp.full((tq, PAD_TO), NEG_BIG, F32)
        acc_ref[h] = jnp.zeros((tq, 2 * HEAD_DIM), F32)
        s_even[h] = scores(h, 0, tq)
    lax.fori_loop(0, n_full + 1, step, 0)
    outs = []
    for h in range(nh):
        acc = acc_ref[h]
        outs.append(acc[:, :HEAD_DIM] / acc[:, HEAD_DIM:HEAD_DIM + 1])
    return outs


def _sweep_scratch(tq):
    return [pltpu.VMEM((ATT_HEADS, tq, tq), F32), pltpu.VMEM((ATT_HEADS, tq, tq), F32),
            pltpu.VMEM((ATT_HEADS, tq, PAD_TO), F32), pltpu.VMEM((ATT_HEADS, tq, 2 * HEAD_DIM), F32)]


def _mla_attn_kernel(q_ref, k_ref, v_ref, o_ref, *scratch):
    tq, lk = q_ref.shape[0], k_ref.shape[0]
    width = 2 * HEAD_DIM
    i = pl.program_id(2)
    q0 = i * tq
    shift = CHUNK - N_META

    def chunk_mask(k_start, k_width):
        qc = (q0 + shift + lax.broadcasted_iota(jnp.int32, (tq, k_width), 0)) // CHUNK
        kc = (k_start + shift + lax.broadcasted_iota(jnp.int32, (tq, k_width), 1)) // CHUNK
        return kc <= qc

    t_start = pl.multiple_of(jnp.minimum(q0 + tq, lk - PAD_TO), PAD_TO)

    def t_mask():
        t_kpos = t_start + lax.broadcasted_iota(jnp.int32, (tq, PAD_TO), 1)
        return (t_kpos >= q0 + tq) & chunk_mask(t_start, PAD_TO)

    qs = [q_ref[:, h * width:(h + 1) * width] for h in range(ATT_HEADS)]
    outs = _softmax_sweep(
        qs,
        lambda h, s0, w: k_ref[pl.ds(s0, w), h * width:(h + 1) * width],
        lambda h, s0, w: v_ref[pl.ds(s0, w), h * width:(h + 1) * width],
        lambda h, s, s0, w: s,
        i, tq, lambda: chunk_mask(q0, tq), (t_start, PAD_TO, t_mask), scratch)
    for h in range(ATT_HEADS):
        o_ref[:, h * HEAD_DIM:(h + 1) * HEAD_DIM] = outs[h].astype(o_ref.dtype)


def _mla_attention(q, k, v):
    b, lp, _ = q.shape
    tq = SEQ_TILE
    gw = ATT_HEADS * 2 * HEAD_DIM
    resident = pl.BlockSpec((None, lp, gw), lambda bi, g, i: (bi, 0, g))
    return pl.pallas_call(
        _mla_attn_kernel,
        grid=(b, HEADS // ATT_HEADS, lp // tq),
        in_specs=[pl.BlockSpec((None, tq, gw), lambda bi, g, i: (bi, i, g)), resident, resident],
        out_specs=pl.BlockSpec((None, tq, ATT_HEADS * HEAD_DIM), lambda bi, g, i: (bi, i, g)),
        out_shape=jax.ShapeDtypeStruct((b, lp, HEADS * HEAD_DIM), BF16),
        scratch_shapes=_sweep_scratch(tq),
        compiler_params=_params(("parallel", "parallel", "arbitrary"), VMEM_BIG),
        name="mla_attention",
    )(q, k, v)


def _fox_attn_kernel(q_ref, k_ref, v_ref, f_ref, qg_ref, kg_ref, o_ref, kn_ref, va_ref, fb_ref,
                     *scratch, q_scale):
    tq, lk = q_ref.shape[0], k_ref.shape[0]
    i = pl.program_id(2)
    q0 = pl.multiple_of(i * tq, tq)
    heads = [slice(h * HEAD_DIM, (h + 1) * HEAD_DIM) for h in range(ATT_HEADS)]

    @pl.when(i == 0)
    def _():
        ones = _ones_column(tq)

        def body(c, carry):
            rows = pl.ds(pl.multiple_of(c * tq, tq), tq)
            for h, sl in enumerate(heads):
                kn_ref[rows, sl] = _rms(k_ref[rows, sl], kg_ref[...]).astype(BF16)
                va_ref[h, rows, 0:HEAD_DIM] = v_ref[rows, sl].astype(BF16)
                va_ref[h, rows, HEAD_DIM:2 * HEAD_DIM] = ones
            return carry
        lax.fori_loop(0, lk // tq, body, 0)
        fb_ref[...] = f_ref[...] * LOG2E

    qs = [(_rms(q_ref[:, sl], qg_ref[...]) * q_scale).astype(BF16) for sl in heads]
    f0 = [jnp.max(fb_ref[h, :, pl.ds(q0, tq)], axis=-1, keepdims=True) for h in range(ATT_HEADS)]

    def causal():
        return (lax.broadcasted_iota(jnp.int32, (tq, tq), 1)
                <= lax.broadcasted_iota(jnp.int32, (tq, tq), 0))

    outs = _softmax_sweep(
        qs,
        lambda h, s0, w: kn_ref[pl.ds(s0, w), heads[h]],
        lambda h, s0, w: va_ref[h, pl.ds(s0, w), :],
        lambda h, s, s0, w: s - (fb_ref[h, :, pl.ds(s0, w)] - f0[h]),
        i, tq, causal, None, scratch)
    for h, sl in enumerate(heads):
        o_ref[:, sl] = outs[h].astype(o_ref.dtype)


def _fox_attention(cd3, f, q_gain, k_gain, q_scale):
    b, lp, _ = cd3.shape
    tq = SEQ_TILE
    gw = ATT_HEADS * HEAD_DIM
    groups = HEADS // ATT_HEADS
    kern = functools.partial(_fox_attn_kernel, q_scale=q_scale)
    return pl.pallas_call(
        kern,
        grid=(b, groups, lp // tq),
        in_specs=[pl.BlockSpec((None, tq, gw), lambda bi, g, i: (bi, i, g)),
                  pl.BlockSpec((None, lp, gw), lambda bi, g, i: (bi, 0, groups + g),
                               pipeline_mode=pl.Buffered(1)),
                  pl.BlockSpec((None, lp, gw), lambda bi, g, i: (bi, 0, 2 * groups + g),
                               pipeline_mode=pl.Buffered(1)),
                  pl.BlockSpec((None, ATT_HEADS, 1, lp), lambda bi, g, i: (bi, g, 0, 0)),
                  _const_spec((1, HEAD_DIM)), _const_spec((1, HEAD_DIM))],
        out_specs=pl.BlockSpec((None, tq, gw), lambda bi, g, i: (bi, i, g)),
        out_shape=jax.ShapeDtypeStruct((b, lp, HEADS * HEAD_DIM), BF16),
        scratch_shapes=[pltpu.VMEM((lp, gw), BF16),
                        pltpu.VMEM((ATT_HEADS, lp, 2 * HEAD_DIM), BF16),
                        pltpu.VMEM((ATT_HEADS, 1, lp), F32)] + _sweep_scratch(tq),
        compiler_params=_params(("parallel", "parallel", "arbitrary"), VMEM_BIG),
        name="fox_attention",
    )(cd3, cd3, cd3, f, q_gain.reshape(1, HEAD_DIM), k_gain.reshape(1, HEAD_DIM))


def _forget_cumsum_kernel(x_ref, b_ref, o_ref):
    rows, lp = x_ref.shape
    lane = lax.broadcasted_iota(jnp.int32, (rows, PAD_TO), 1)
    carry = jnp.zeros((rows, 1), F32)
    for c in range(lp // PAD_TO):
        sl = slice(c * PAD_TO, (c + 1) * PAD_TO)
        x = _log_sigmoid(x_ref[:, sl] + b_ref[...])
        step = 1
        while step < PAD_TO:
            x = x + jnp.where(lane >= step, pltpu.roll(x, step, axis=1), 0.0)
            step *= 2
        x = x + carry
        o_ref[:, sl] = x
        carry = x[:, PAD_TO - 1:PAD_TO]


def _forget_cumsum(ff_t, bias_col):
    rows, lp = ff_t.shape
    return pl.pallas_call(
        _forget_cumsum_kernel,
        out_shape=jax.ShapeDtypeStruct((rows, lp), F32),
        name="fox_forget_cumsum",
    )(ff_t, bias_col)


def _cumsum_rows(x):
    row = lax.broadcasted_iota(jnp.int32, x.shape, 0)
    step = 1
    while step < x.shape[0]:
        x = x + jnp.where(row >= step, pltpu.roll(x, step, axis=0), 0.0)
        step *= 2
    return x


def _gla_kernel(q_ref, k_ref, v_ref, r_ref, a_ref, wa_ref, ba_ref, on_ref, o_ref, st_ref):
    tm = q_ref.shape[0]

    @pl.when(pl.program_id(1) == 0)
    def _():
        st_ref[...] = jnp.zeros_like(st_ref)

    causal = (lax.broadcasted_iota(jnp.int32, (CHUNK, CHUNK), 1)
              <= lax.broadcasted_iota(jnp.int32, (CHUNK, CHUNK), 0))

    def body(c, carry):
        r0 = pl.multiple_of(c * CHUNK, CHUNK)
        rows = pl.ds(r0, CHUNK)
        gate = jnp.dot(a_ref[rows, :].astype(BF16), wa_ref[...], preferred_element_type=F32)
        for h in range(GLA_HEADS):
            ks = slice(h * GLA_DK, (h + 1) * GLA_DK)
            vs = slice(h * GLA_DV, (h + 1) * GLA_DV)
            g = _log_sigmoid(gate[:, ks] + ba_ref[:, ks]) / GLA_TAU
            bcum = _cumsum_rows(g)
            b_last = bcum[CHUNK - 1:CHUNK, :]
            k = k_ref[rows, ks]
            v = v_ref[rows, vs].astype(BF16)
            q_dec = ((q_ref[rows, ks] * (GLA_DK ** -0.5)) * jnp.exp(bcum)).astype(BF16)
            k_inv = (k * jnp.exp(-bcum)).astype(BF16)
            k_end = (k * jnp.exp(b_last - bcum)).astype(BF16)
            a = lax.dot_general(q_dec, k_inv, NT_DIMS, preferred_element_type=F32)
            a = jnp.where(causal, a, 0.0).astype(BF16)
            st = st_ref[h]
            o = (jnp.dot(a, v, preferred_element_type=F32)
                 + lax.dot_general(q_dec, st.astype(BF16), NT_DIMS, preferred_element_type=F32))
            st_ref[h] = st * jnp.exp(b_last) + lax.dot_general(v, k_end, TN_DIMS,
                                                               preferred_element_type=F32)
            o = _rms(o, on_ref[...])
            o_ref[rows, vs] = (o * _silu(r_ref[rows, vs])).astype(o_ref.dtype)
        return carry

    lax.fori_loop(0, tm // CHUNK, body, 0)


def _gla(cd3, wa, layer, ba, o_norm):
    b, lp, _ = cd3.shape
    tm = SEQ_TILE
    kw, vw = GLA_HEADS * GLA_DK, GLA_HEADS * GLA_DV
    return pl.pallas_call(
        _gla_kernel,
        grid=(b, lp // tm),
        in_specs=[pl.BlockSpec((None, tm, kw), lambda bi, i: (bi, i, 6144 // kw)),
                  pl.BlockSpec((None, tm, kw), lambda bi, i: (bi, i, 7168 // kw)),
                  pl.BlockSpec((None, tm, vw), lambda bi, i: (bi, i, 8192 // vw)),
                  pl.BlockSpec((None, tm, vw), lambda bi, i: (bi, i, 10240 // vw)),
                  pl.BlockSpec((None, tm, PAD_TO), lambda bi, i: (bi, i, 12288 // PAD_TO)),
                  _const_spec((PAD_TO, kw), layer), _const_spec((1, kw)), _const_spec((1, GLA_DV))],
        out_specs=pl.BlockSpec((None, tm, vw), lambda bi, i: (bi, i, 0)),
        out_shape=jax.ShapeDtypeStruct((b, lp, vw), BF16),
        scratch_shapes=[pltpu.VMEM((GLA_HEADS, GLA_DV, GLA_DK), F32)],
        compiler_params=_params(("parallel", "arbitrary"), VMEM_MID),
        name="gla",
    )(cd3, cd3, cd3, cd3, cd3, wa, ba.reshape(1, -1), o_norm.reshape(1, -1))


def _ab_in_weight(w):
    wt = jnp.swapaxes(w, 1, 2)
    zeros = jnp.zeros((w.shape[0], 3840 - w.shape[2], w.shape[1]), w.dtype)
    return jnp.concatenate([wt, zeros], axis=1).astype(BF16)


def _cd_in_weight(w):
    wt = jnp.swapaxes(w, 1, 2)
    pieces = [wt[:, :6144], wt[:, 6160:10256], wt[:, 10272:], wt[:, 6144:6160],
              wt[:, 10256:10272], jnp.zeros((w.shape[0], 12800 - w.shape[2], w.shape[1]), w.dtype)]
    return jnp.concatenate(pieces, axis=1).astype(BF16)


def _mla_q_weight(w):
    n, r, _ = w.shape
    w = jnp.pad(w.astype(BF16).reshape(n, r, HEADS, MLA_QK),
                ((0, 0), (0, 0), (0, 0), (0, 2 * HEAD_DIM - MLA_QK)))
    return w.reshape(n, r, HEADS * 2 * HEAD_DIM)


def _gla_gate_weight(w):
    return jnp.pad(w.astype(BF16), ((0, 0), (GLA_RANK, PAD_TO - 2 * GLA_RANK), (0, 0)))


def _rotary_tables(lp):
    pos = jnp.arange(lp, dtype=F32)
    inv_freq = ROPE_BASE ** (-jnp.arange(0, MLA_ROPE, 2, dtype=F32) / MLA_ROPE)
    ang = pos[:, None] * inv_freq[None, :]
    cos, sin = jnp.cos(ang), jnp.sin(ang)
    z32, z64 = jnp.zeros_like(cos), jnp.zeros((lp, HEAD_DIM - MLA_ROPE), F32)
    return (jnp.concatenate([cos, cos, z64], axis=1),
            jnp.concatenate([-sin, z32, z64], axis=1),
            jnp.concatenate([z32, sin, z64], axis=1))


def _ffn(h, norm, w_gate, w_up, w_down, layer, name):
    hidden = _norm_matmul(h, norm, [w_gate, w_up], layer, _swiglu_epilogue, BF16, ROW_TILE, 512,
                          name + "_up")
    return _matmul_residual([hidden], w_down, layer, h, 0.5, ROW_TILE, 512, name + "_down")


def _pool_mla_layer(h, b, lp, norm, w_in, pool_w, pool_scale, q_norm, wq, kv_norm, wkv,
                    q_gain, k_gain, w_out, layer, tables):
    ab = _norm_matmul(h, norm, [w_in], layer, _identity, F32, ROW_TILE, 768, "ab_in",
                      w_transposed=True)
    y_pool = _pool_mixer(ab, pool_w, layer, pool_scale, lp)
    q, k, v = _mla_prep(ab.reshape(b, lp, -1), tables, q_norm, kv_norm, wq, wkv, layer,
                        q_gain, k_gain, MLA_QK ** -0.5 * LOG2E)
    y_mla = _mla_attention(q, k, v)
    return _matmul_residual([y_pool, y_mla.reshape(b * lp, -1)], w_out, layer, h, 1.0,
                            ROW_TILE, 512, "ab_out")


def _fox_gla_layer(h, b, lp, norm, w_in, fox_q_gain, fox_k_gain, fox_f_bias, wa, gla_b_a,
                   gla_o_norm, w_out, layer):
    cd = _norm_matmul(h, norm, [w_in], layer, _identity, F32, ROW_TILE, 512, "cd_in",
                      w_transposed=True)
    cd3 = cd.reshape(b, lp, -1)
    ff_t = cd[:, 12288:12288 + HEADS].reshape(b, lp, HEADS).transpose(0, 2, 1).reshape(b * HEADS, lp)
    f = _forget_cumsum(ff_t, jnp.tile(fox_f_bias, b).reshape(b * HEADS, 1))
    y_fox = _fox_attention(cd3, f.reshape(b, HEADS, 1, lp), fox_q_gain, fox_k_gain,
                           HEAD_DIM ** -0.5 * LOG2E)
    y_gla = _gla(cd3, wa, layer, gla_b_a, gla_o_norm)
    return _matmul_residual([y_fox.reshape(b * lp, -1), y_gla.reshape(b * lp, -1)], w_out, layer,
                            h, 1.0, ROW_TILE, 512, "cd_out")


def kernel(x, meta_tokens, ffn1_norm, ffn1_w_gate, ffn1_w_up, ffn1_w_down, mix_norm, ffn2_norm, ffn2_w_gate, ffn2_w_up, ffn2_w_down, ab_w_in, pool_w, pool_scale, mla_q_norm, mla_w_q_up, mla_kv_norm, mla_w_kv_up, mla_q_gain, mla_k_gain, ab_w_out, cd_w_in, fox_q_gain, fox_k_gain, fox_f_bias, gla_w_a2, gla_b_a, gla_o_norm, cd_w_out):
    b, s, d = x.shape
    length = N_META + s
    lp = -(-length // PAD_TO) * PAD_TO
    meta = jnp.broadcast_to(meta_tokens.astype(x.dtype)[None], (b, N_META, d))
    h = jnp.concatenate([meta, x, jnp.zeros((b, lp - length, d), x.dtype)], axis=1)
    h = h.reshape(b * lp, d)
    tables = _rotary_tables(lp)

    ffn1 = [w.astype(BF16) for w in (ffn1_w_gate, ffn1_w_up, ffn1_w_down)]
    ffn2 = [w.astype(BF16) for w in (ffn2_w_gate, ffn2_w_up, ffn2_w_down)]
    ab_in, cd_in = _ab_in_weight(ab_w_in), _cd_in_weight(cd_w_in)
    ab_out, cd_out = ab_w_out.astype(BF16), cd_w_out.astype(BF16)
    pool_wb, wq, wkv = pool_w.astype(BF16), _mla_q_weight(mla_w_q_up), mla_w_kv_up.astype(BF16)
    wa = _gla_gate_weight(gla_w_a2)

    for layer in range(ffn1_norm.shape[0]):
        i = layer // 2
        h = _ffn(h, ffn1_norm[layer], *ffn1, layer, "ffn1")
        if layer % 2 == 0:
            h = _pool_mla_layer(h, b, lp, mix_norm[layer], ab_in, pool_wb, pool_scale[i],
                                mla_q_norm[i], wq, mla_kv_norm[i], wkv, mla_q_gain[i],
                                mla_k_gain[i], ab_out, i, tables)
        else:
            h = _fox_gla_layer(h, b, lp, mix_norm[layer], cd_in, fox_q_gain[i], fox_k_gain[i],
                               fox_f_bias[i], wa, gla_b_a[i], gla_o_norm[i], cd_out, i)
        h = _ffn(h, ffn2_norm[layer], *ffn2, layer, "ffn2")
    return h.reshape(b, lp, d)[:, N_META:N_META + s]
```

```python
import functools
import math

import jax
import jax.numpy as jnp
from jax import lax
from jax.experimental import pallas as pl
from jax.experimental.pallas import tpu as pltpu

F32 = jnp.float32
BF16 = jnp.bfloat16

N_META = 16
PAD_TO = 128
RMS_EPS = 1e-6
ROPE_BASE = 10000.0
CHUNK = 64

POOL_WINDOWS = (2, 4, 8, 16)
POOL_GROUP = 512
HEADS = 16
HEAD_DIM = 128
MLA_ROPE = 64
MLA_QK = 192
GLA_HEADS = 4
GLA_DK = 256
GLA_DV = 512
GLA_TAU = 16.0
GLA_RANK = 16

VMEM_BIG = 58 * 1024 * 1024
VMEM_MID = 40 * 1024 * 1024

ROW_TILE = 768
RES_ROWS = ROW_TILE
RES_COLS = 512
SEQ_TILE = 384
ATT_HEADS = 4
NEG_BIG = -1e30
LOG2E = math.log2(math.e)

NT_DIMS = (((1,), (1,)), ((), ()))
TN_DIMS = (((0,), (0,)), ((), ()))


def _silu(x):
    return x / (1.0 + jnp.exp(-x))


def _log_sigmoid(x):
    return jnp.minimum(x, 0.0) - jnp.log(1.0 + jnp.exp(-jnp.abs(x)))


def _rms(x, gain, n=None):
    n = x.shape[-1] if n is None else n
    ss = jnp.sum(x * x, axis=-1, keepdims=True)
    return x * lax.rsqrt(ss / n + RMS_EPS) * gain


def _params(sem, vmem):
    return pltpu.CompilerParams(dimension_semantics=sem, vmem_limit_bytes=vmem)


def _const_spec(shape, layer=None):
    if layer is None:
        nd = len(shape)
        return pl.BlockSpec(shape, lambda *_: (0,) * nd, pipeline_mode=pl.Buffered(1))
    nd = len(shape)
    return pl.BlockSpec((None,) + tuple(shape), lambda *_: (layer,) + (0,) * nd,
                        pipeline_mode=pl.Buffered(1))


def _norm_mm_kernel(a_ref, g_ref, *rest, n_w, epilogue, rows, w_transposed):
    w_refs, o_ref, xn_ref = rest[:n_w], rest[n_w], rest[n_w + 1]
    dims = NT_DIMS if w_transposed else (((1,), (0,)), ((), ()))

    @pl.when(pl.program_id(1) == 0)
    def _():
        def body(c, carry):
            r0 = pl.multiple_of(c * rows, rows)
            a = a_ref[pl.ds(r0, rows), :]
            xn_ref[pl.ds(r0, rows), :] = _rms(a, g_ref[...]).astype(BF16)
            return carry
        lax.fori_loop(0, a_ref.shape[0] // rows, body, 0)

    xn = xn_ref[...]
    outs = [lax.dot_general(xn, w[...], dims, preferred_element_type=F32) for w in w_refs]
    o_ref[...] = epilogue(*outs).astype(o_ref.dtype)


def _norm_matmul(a, gain, ws, layer, epilogue, out_dtype, tm, tn, name, w_transposed=False):
    t, k = a.shape
    n = ws[0].shape[1 if w_transposed else 2]
    kern = functools.partial(_norm_mm_kernel, n_w=len(ws), epilogue=epilogue, rows=64,
                             w_transposed=w_transposed)
    if w_transposed:
        w_spec = pl.BlockSpec((None, tn, k), lambda i, j: (layer, j, 0))
    else:
        w_spec = pl.BlockSpec((None, k, tn), lambda i, j: (layer, 0, j))
    return pl.pallas_call(
        kern,
        grid=(t // tm, n // tn),
        in_specs=[pl.BlockSpec((tm, k), lambda i, j: (i, 0)),
                  pl.BlockSpec((1, k), lambda i, j: (0, 0))] + [w_spec for _ in ws],
        out_specs=pl.BlockSpec((tm, tn), lambda i, j: (i, j)),
        out_shape=jax.ShapeDtypeStruct((t, n), out_dtype),
        scratch_shapes=[pltpu.VMEM((tm, k), BF16)],
        compiler_params=_params(("parallel", "arbitrary"), VMEM_BIG),
        name=name,
    )(a, gain.reshape(1, k), *ws)


def _swiglu_epilogue(g, u):
    return _silu(g) * u


def _identity(x):
    return x


def _mm_res_kernel(*refs, k_sizes, scale):
    n_a = len(k_sizes)
    a_refs, w_ref, r_ref, o_ref = refs[:n_a], refs[n_a], refs[n_a + 1], refs[n_a + 2]
    acc, off = None, 0
    for a_ref, ks in zip(a_refs, k_sizes):
        p = jnp.dot(a_ref[...], w_ref[off:off + ks, :], preferred_element_type=F32)
        acc = p if acc is None else acc + p
        off += ks
    if scale != 1.0:
        acc = scale * acc
    o_ref[...] = r_ref[...] + acc


def _matmul_residual(a_list, w, layer, res, scale, tm, tn, name):
    t, n = res.shape
    k_sizes = tuple(a.shape[1] for a in a_list)
    k = sum(k_sizes)
    kern = functools.partial(_mm_res_kernel, k_sizes=k_sizes, scale=scale)
    n_a = len(a_list)
    return pl.pallas_call(
        kern,
        grid=(t // tm, n // tn),
        in_specs=[pl.BlockSpec((tm, ks), lambda i, j: (i, 0)) for ks in k_sizes]
                 + [pl.BlockSpec((None, k, tn), lambda i, j: (layer, 0, j)),
                    pl.BlockSpec((tm, tn), lambda i, j: (i, j))],
        out_specs=pl.BlockSpec((tm, tn), lambda i, j: (i, j)),
        out_shape=jax.ShapeDtypeStruct((t, n), F32),
        input_output_aliases={n_a + 1: 0},
        compiler_params=_params(("parallel", "arbitrary"), VMEM_BIG),
        name=name,
    )(*a_list, w, res)


def _pool_kernel(x_ref, halo_ref, w_ref, s_ref, o_ref, xs_ref, *, tiles_per_seq):
    tm = x_ref.shape[0]
    halo = POOL_WINDOWS[-1]
    it = pl.program_id(0) % tiles_per_seq
    keep = (it > 0).astype(F32)
    xs_ref[0:halo, :] = halo_ref[...] * keep
    xs_ref[halo:halo + tm, :] = x_ref[...]
    pos = it * tm + lax.broadcasted_iota(jnp.int32, (tm, 1), 0)
    for g, win in enumerate(POOL_WINDOWS):
        c0 = g * POOL_GROUP
        x = xs_ref[halo:halo + tm, c0:c0 + POOL_GROUP]
        acc = x
        for j in range(1, win):
            acc = acc + xs_ref[halo - j:halo - j + tm, c0:c0 + POOL_GROUP]
        cnt = jnp.minimum(pos + 1, win).astype(F32)
        d = acc / cnt - x
        y = jnp.dot(d.astype(BF16), w_ref[g], preferred_element_type=F32)
        o_ref[:, c0:c0 + POOL_GROUP] = (y * s_ref[:, c0:c0 + POOL_GROUP]).astype(o_ref.dtype)


def _pool_mixer(ab, pool_w, layer, pool_scale, lp):
    t = ab.shape[0]
    tm, halo = SEQ_TILE, POOL_WINDOWS[-1]
    width = len(POOL_WINDOWS) * POOL_GROUP
    per_halo = tm // halo
    kern = functools.partial(_pool_kernel, tiles_per_seq=lp // tm)
    return pl.pallas_call(
        kern,
        grid=(t // tm,),
        in_specs=[pl.BlockSpec((tm, width), lambda i: (i, 0)),
                  pl.BlockSpec((halo, width), lambda i: (jnp.maximum(i * per_halo - 1, 0), 0)),
                  _const_spec((len(POOL_WINDOWS), POOL_GROUP, POOL_GROUP), layer),
                  _const_spec((1, width))],
        out_specs=pl.BlockSpec((tm, width), lambda i: (i, 0)),
        out_shape=jax.ShapeDtypeStruct((t, width), BF16),
        scratch_shapes=[pltpu.VMEM((tm + halo, width), F32)],
        compiler_params=_params(("parallel",), VMEM_MID),
        name="pool_mixer",
    )(ab, ab, pool_w, pool_scale.reshape(1, width))


def _rotary_slot(x, c_ref, s1_ref, s2_ref):
    return (x * c_ref[...] + pltpu.roll(x, 96, axis=1) * s1_ref[...]
            + pltpu.roll(x, 32, axis=1) * s2_ref[...])


def _mla_prep_kernel(cq_ref, ckv_ref, kpe_ref, c_ref, s1_ref, s2_ref, qn_ref, kvn_ref,
                     wq_ref, wkv_ref, qgn_ref, qgr_ref, kgn_ref, kgr_ref,
                     q_ref, k_ref, v_ref, *, q_scale):
    cq = _rms(cq_ref[...], qn_ref[...]).astype(BF16)
    ckv = _rms(ckv_ref[...], kvn_ref[...]).astype(BF16)
    kr = _rotary_slot(_rms(kpe_ref[...], kgr_ref[...], MLA_ROPE), c_ref, s1_ref, s2_ref).astype(BF16)
    for h in range(HEADS):
        lo = h * 2 * HEAD_DIM
        mid, hi = lo + HEAD_DIM, lo + 2 * HEAD_DIM
        q = jnp.dot(cq, wq_ref[:, lo:hi], preferred_element_type=F32)
        qn = _rms(q[:, :HEAD_DIM], qgn_ref[...])
        qr = _rotary_slot(_rms(q[:, HEAD_DIM:], qgr_ref[...], MLA_ROPE), c_ref, s1_ref, s2_ref)
        q_ref[:, lo:mid] = (qn * q_scale).astype(BF16)
        q_ref[:, mid:hi] = (qr * q_scale).astype(BF16)
        kv = jnp.dot(ckv, wkv_ref[:, lo:hi], preferred_element_type=F32)
        k_ref[:, lo:mid] = _rms(kv[:, :HEAD_DIM], kgn_ref[...]).astype(BF16)
        k_ref[:, mid:hi] = kr
        v_ref[:, h * HEAD_DIM:(h + 1) * HEAD_DIM] = kv[:, HEAD_DIM:].astype(BF16)


def _mla_prep(ab3, tables, q_norm, kv_norm, wq, wkv, layer, q_gain, k_gain, q_scale):
    b, lp, _ = ab3.shape
    tm = SEQ_TILE
    q_rank, kv_rank = wq.shape[1], wkv.shape[1]
    cat = HEADS * 2 * HEAD_DIM
    zeros = jnp.zeros((HEAD_DIM - MLA_ROPE,), F32)
    slot = lambda g: jnp.concatenate([g[HEAD_DIM:], zeros]).reshape(1, HEAD_DIM)
    row = lambda i_, j_: (j_, 0)
    out = jax.ShapeDtypeStruct((b, lp, cat), BF16)
    return pl.pallas_call(
        functools.partial(_mla_prep_kernel, q_scale=q_scale),
        grid=(b, lp // tm),
        in_specs=[pl.BlockSpec((None, tm, q_rank), lambda i, j: (i, j, 2048 // q_rank)),
                  pl.BlockSpec((None, tm, kv_rank), lambda i, j: (i, j, 3072 // kv_rank)),
                  pl.BlockSpec((None, tm, HEAD_DIM), lambda i, j: (i, j, 3584 // HEAD_DIM)),
                  pl.BlockSpec((tm, HEAD_DIM), row),
                  pl.BlockSpec((tm, HEAD_DIM), row),
                  pl.BlockSpec((tm, HEAD_DIM), row),
                  _const_spec((1, q_rank)), _const_spec((1, kv_rank)),
                  _const_spec((q_rank, cat), layer), _const_spec((kv_rank, cat), layer),
                  _const_spec((1, HEAD_DIM)), _const_spec((1, HEAD_DIM)),
                  _const_spec((1, HEAD_DIM)), _const_spec((1, HEAD_DIM))],
        out_specs=[pl.BlockSpec((None, tm, cat), lambda i, j: (i, j, 0))] * 2
                  + [pl.BlockSpec((None, tm, HEADS * HEAD_DIM), lambda i, j: (i, j, 0))],
        out_shape=[out, out, jax.ShapeDtypeStruct((b, lp, HEADS * HEAD_DIM), BF16)],
        compiler_params=_params(("parallel", "parallel"), VMEM_BIG),
        name="mla_prep",
    )(ab3, ab3, ab3, *tables, q_norm.reshape(1, -1), kv_norm.reshape(1, -1), wq, wkv,
      q_gain[:HEAD_DIM].reshape(1, HEAD_DIM), slot(q_gain),
      k_gain[:HEAD_DIM].reshape(1, HEAD_DIM), slot(k_gain))


def _softmax_sweep(qs, key_fn, val_fn, bias_fn, n_full, tq, diag_mask, tail, scratch):
    nh = len(qs)
    s_even, s_odd, m_ref, l_ref, acc_ref = scratch

    def scores(h, start, width):
        s = lax.dot_general(qs[h], key_fn(h, start, width), NT_DIMS, preferred_element_type=F32)
        return bias_fn(h, s, start, width)

    def lanes(x, width):
        return jnp.concatenate([x] * (width // PAD_TO), axis=1)

    def absorb(h, s, vals):
        m = m_ref[h]
        m_new = jnp.maximum(m, jnp.max(s, axis=-1, keepdims=True))
        alpha = jnp.exp2(m - m_new)
        p = jnp.exp2(s - lanes(m_new, s.shape[1]))
        p_sum = p[:, :PAD_TO]
        for c in range(1, s.shape[1] // PAD_TO):
            p_sum = p_sum + p[:, c * PAD_TO:(c + 1) * PAD_TO]
        l_ref[h] = alpha * l_ref[h] + p_sum
        acc_ref[h] = alpha * acc_ref[h] + jnp.dot(p.astype(BF16), vals, preferred_element_type=F32)
        m_ref[h] = m_new

    def half_step(c, cur, nxt, diagonal):
        start = pl.multiple_of(c * tq, tq)
        for h in range(nh):
            if not diagonal:
                nxt[h] = scores(h, start + tq, tq)
                absorb(h, cur[h], val_fn(h, start, tq))
                continue
            s = jnp.where(diag_mask(), cur[h], NEG_BIG)
            vals = val_fn(h, start, tq)
            if tail is not None:
                t_start, t_width, t_mask = tail
                s_tail = jnp.where(t_mask(), scores(h, t_start, t_width), NEG_BIG)
                s = jnp.concatenate([s, s_tail], axis=1)
                vals = jnp.concatenate([vals, val_fn(h, t_start, t_width)], axis=0)
            absorb(h, s, vals)

    def by_parity(c, diagonal):
        return lambda: lax.cond(c % 2 == 0, lambda: half_step(c, s_even, s_odd, diagonal),
                                lambda: half_step(c, s_odd, s_even, diagonal))

    def step(c, carry):
        lax.cond(c == n_full, by_parity(c, True), by_parity(c, False))
        return carry

    for h in range(nh):
        m_ref[h] = jnp.full((tq, PAD_TO), NEG_BIG, F32)
        l_ref[h] = jnp.zeros((tq, PAD_TO), F32)
        acc_ref[h] = jnp.zeros((tq, HEAD_DIM), F32)
        s_even[h] = scores(h, 0, tq)
    lax.fori_loop(0, n_full + 1, step, 0)
    return [acc_ref[h] / jnp.sum(l_ref[h], axis=-1, keepdims=True) for h in range(nh)]


def _sweep_scratch(tq):
    return [pltpu.VMEM((ATT_HEADS, tq, tq), F32), pltpu.VMEM((ATT_HEADS, tq, tq), F32),
            pltpu.VMEM((ATT_HEADS, tq, PAD_TO), F32), pltpu.VMEM((ATT_HEADS, tq, PAD_TO), F32),
            pltpu.VMEM((ATT_HEADS, tq, HEAD_DIM), F32)]


def _mla_attn_kernel(q_ref, k_ref, v_ref, o_ref, *scratch):
    tq, lk = q_ref.shape[0], k_ref.shape[0]
    width = 2 * HEAD_DIM
    i = pl.program_id(2)
    q0 = i * tq
    shift = CHUNK - N_META

    def chunk_mask(k_start, k_width):
        qc = (q0 + shift + lax.broadcasted_iota(jnp.int32, (tq, k_width), 0)) // CHUNK
        kc = (k_start + shift + lax.broadcasted_iota(jnp.int32, (tq, k_width), 1)) // CHUNK
        return kc <= qc

    t_start = pl.multiple_of(jnp.minimum(q0 + tq, lk - PAD_TO), PAD_TO)

    def t_mask():
        t_kpos = t_start + lax.broadcasted_iota(jnp.int32, (tq, PAD_TO), 1)
        return (t_kpos >= q0 + tq) & chunk_mask(t_start, PAD_TO)

    qs = [q_ref[:, h * width:(h + 1) * width] for h in range(ATT_HEADS)]
    outs = _softmax_sweep(
        qs,
        lambda h, s0, w: k_ref[pl.ds(s0, w), h * width:(h + 1) * width],
        lambda h, s0, w: v_ref[pl.ds(s0, w), h * HEAD_DIM:(h + 1) * HEAD_DIM],
        lambda h, s, s0, w: s,
        i, tq, lambda: chunk_mask(q0, tq), (t_start, PAD_TO, t_mask), scratch)
    for h in range(ATT_HEADS):
        o_ref[:, h * HEAD_DIM:(h + 1) * HEAD_DIM] = outs[h].astype(o_ref.dtype)


def _mla_attention(q, k, v):
    b, lp, _ = q.shape
    tq = SEQ_TILE
    gw = ATT_HEADS * 2 * HEAD_DIM
    return pl.pallas_call(
        _mla_attn_kernel,
        grid=(b, HEADS // ATT_HEADS, lp // tq),
        in_specs=[pl.BlockSpec((None, tq, gw), lambda bi, g, i: (bi, i, g)),
                  pl.BlockSpec((None, lp, gw), lambda bi, g, i: (bi, 0, g)),
                  pl.BlockSpec((None, lp, ATT_HEADS * HEAD_DIM), lambda bi, g, i: (bi, 0, g))],
        out_specs=pl.BlockSpec((None, tq, ATT_HEADS * HEAD_DIM), lambda bi, g, i: (bi, i, g)),
        out_shape=jax.ShapeDtypeStruct((b, lp, HEADS * HEAD_DIM), BF16),
        scratch_shapes=_sweep_scratch(tq),
        compiler_params=_params(("parallel", "parallel", "arbitrary"), VMEM_BIG),
        name="mla_attention",
    )(q, k, v)


def _fox_attn_kernel(q_ref, k_ref, v_ref, f_ref, qg_ref, kg_ref, o_ref, kn_ref, va_ref, fb_ref,
                     *scratch, q_scale):
    tq, lk = q_ref.shape[0], k_ref.shape[0]
    i = pl.program_id(2)
    q0 = pl.multiple_of(i * tq, tq)
    heads = [slice(h * HEAD_DIM, (h + 1) * HEAD_DIM) for h in range(ATT_HEADS)]

    @pl.when(i == 0)
    def _():
        def body(c, carry):
            rows = pl.ds(pl.multiple_of(c * tq, tq), tq)
            for sl in heads:
                kn_ref[rows, sl] = _rms(k_ref[rows, sl], kg_ref[...]).astype(BF16)
            va_ref[rows, :] = v_ref[rows, :].astype(BF16)
            return carry
        lax.fori_loop(0, lk // tq, body, 0)
        fb_ref[...] = f_ref[...] * LOG2E

    qs = [(_rms(q_ref[:, sl], qg_ref[...]) * q_scale).astype(BF16) for sl in heads]
    f0 = [jnp.max(fb_ref[h, :, pl.ds(q0, tq)], axis=-1, keepdims=True) for h in range(ATT_HEADS)]

    def causal():
        return (lax.broadcasted_iota(jnp.int32, (tq, tq), 1)
                <= lax.broadcasted_iota(jnp.int32, (tq, tq), 0))

    outs = _softmax_sweep(
        qs,
        lambda h, s0, w: kn_ref[pl.ds(s0, w), heads[h]],
        lambda h, s0, w: va_ref[pl.ds(s0, w), heads[h]],
        lambda h, s, s0, w: s - (fb_ref[h, :, pl.ds(s0, w)] - f0[h]),
        i, tq, causal, None, scratch)
    for h, sl in enumerate(heads):
        o_ref[:, sl] = outs[h].astype(o_ref.dtype)


def _fox_attention(cd3, f, q_gain, k_gain, q_scale):
    b, lp, _ = cd3.shape
    tq = SEQ_TILE
    gw = ATT_HEADS * HEAD_DIM
    groups = HEADS // ATT_HEADS
    kern = functools.partial(_fox_attn_kernel, q_scale=q_scale)
    return pl.pallas_call(
        kern,
        grid=(b, groups, lp // tq),
        in_specs=[pl.BlockSpec((None, tq, gw), lambda bi, g, i: (bi, i, g)),
                  pl.BlockSpec((None, lp, gw), lambda bi, g, i: (bi, 0, groups + g)),
                  pl.BlockSpec((None, lp, gw), lambda bi, g, i: (bi, 0, 2 * groups + g)),
                  pl.BlockSpec((None, ATT_HEADS, 1, lp), lambda bi, g, i: (bi, g, 0, 0)),
                  _const_spec((1, HEAD_DIM)), _const_spec((1, HEAD_DIM))],
        out_specs=pl.BlockSpec((None, tq, gw), lambda bi, g, i: (bi, i, g)),
        out_shape=jax.ShapeDtypeStruct((b, lp, HEADS * HEAD_DIM), BF16),
        scratch_shapes=[pltpu.VMEM((lp, gw), BF16),
                        pltpu.VMEM((lp, gw), BF16),
                        pltpu.VMEM((ATT_HEADS, 1, lp), F32)] + _sweep_scratch(tq),
        compiler_params=_params(("parallel", "parallel", "arbitrary"), VMEM_BIG),
        name="fox_attention",
    )(cd3, cd3, cd3, f, q_gain.reshape(1, HEAD_DIM), k_gain.reshape(1, HEAD_DIM))


def _forget_cumsum_kernel(x_ref, b_ref, o_ref):
    rows, lp = x_ref.shape
    lane = lax.broadcasted_iota(jnp.int32, (rows, PAD_TO), 1)
    carry = jnp.zeros((rows, 1), F32)
    for c in range(lp // PAD_TO):
        sl = slice(c * PAD_TO, (c + 1) * PAD_TO)
        x = _log_sigmoid(x_ref[:, sl] + b_ref[...])
        step = 1
        while step < PAD_TO:
            x = x + jnp.where(lane >= step, pltpu.roll(x, step, axis=1), 0.0)
            step *= 2
        x = x + carry
        o_ref[:, sl] = x
        carry = x[:, PAD_TO - 1:PAD_TO]


def _forget_cumsum(ff_t, bias_col):
    rows, lp = ff_t.shape
    return pl.pallas_call(
        _forget_cumsum_kernel,
        out_shape=jax.ShapeDtypeStruct((rows, lp), F32),
        name="fox_forget_cumsum",
    )(ff_t, bias_col)


def _cumsum_rows(x):
    row = lax.broadcasted_iota(jnp.int32, x.shape, 0)
    step = 1
    while step < x.shape[0]:
        x = x + jnp.where(row >= step, pltpu.roll(x, step, axis=0), 0.0)
        step *= 2
    return x


def _gla_kernel(q_ref, k_ref, v_ref, r_ref, a_ref, wa_ref, ba_ref, on_ref, o_ref, st_ref):
    tm = q_ref.shape[0]

    @pl.when(pl.program_id(1) == 0)
    def _():
        st_ref[...] = jnp.zeros_like(st_ref)

    causal = (lax.broadcasted_iota(jnp.int32, (CHUNK, CHUNK), 1)
              <= lax.broadcasted_iota(jnp.int32, (CHUNK, CHUNK), 0))

    def body(c, carry):
        r0 = pl.multiple_of(c * CHUNK, CHUNK)
        rows = pl.ds(r0, CHUNK)
        gate = jnp.dot(a_ref[rows, :].astype(BF16), wa_ref[...], preferred_element_type=F32)
        for h in range(GLA_HEADS):
            ks = slice(h * GLA_DK, (h + 1) * GLA_DK)
            vs = slice(h * GLA_DV, (h + 1) * GLA_DV)
            g = _log_sigmoid(gate[:, ks] + ba_ref[:, ks]) / GLA_TAU
            bcum = _cumsum_rows(g)
            b_last = bcum[CHUNK - 1:CHUNK, :]
            k = k_ref[rows, ks]
            v = v_ref[rows, vs].astype(BF16)
            q_dec = ((q_ref[rows, ks] * (GLA_DK ** -0.5)) * jnp.exp(bcum)).astype(BF16)
            k_inv = (k * jnp.exp(-bcum)).astype(BF16)
            k_end = (k * jnp.exp(b_last - bcum)).astype(BF16)
            a = lax.dot_general(q_dec, k_inv, NT_DIMS, preferred_element_type=F32)
            a = jnp.where(causal, a, 0.0).astype(BF16)
            st = st_ref[h]
            o = (jnp.dot(a, v, preferred_element_type=F32)
                 + lax.dot_general(q_dec, st.astype(BF16), NT_DIMS, preferred_element_type=F32))
            st_ref[h] = st * jnp.exp(b_last) + lax.dot_general(v, k_end, TN_DIMS,
                                                               preferred_element_type=F32)
            o = _rms(o, on_ref[...])
            o_ref[rows, vs] = (o * _silu(r_ref[rows, vs])).astype(o_ref.dtype)
        return carry

    lax.fori_loop(0, tm // CHUNK, body, 0)


def _gla(cd3, wa, layer, ba, o_norm):
    b, lp, _ = cd3.shape
    tm = SEQ_TILE
    kw, vw = GLA_HEADS * GLA_DK, GLA_HEADS * GLA_DV
    return pl.pallas_call(
        _gla_kernel,
        grid=(b, lp // tm),
        in_specs=[pl.BlockSpec((None, tm, kw), lambda bi, i: (bi, i, 6144 // kw)),
                  pl.BlockSpec((None, tm, kw), lambda bi, i: (bi, i, 7168 // kw)),
                  pl.BlockSpec((None, tm, vw), lambda bi, i: (bi, i, 8192 // vw)),
                  pl.BlockSpec((None, tm, vw), lambda bi, i: (bi, i, 10240 // vw)),
                  pl.BlockSpec((None, tm, PAD_TO), lambda bi, i: (bi, i, 12288 // PAD_TO)),
                  _const_spec((PAD_TO, kw), layer), _const_spec((1, kw)), _const_spec((1, GLA_DV))],
        out_specs=pl.BlockSpec((None, tm, vw), lambda bi, i: (bi, i, 0)),
        out_shape=jax.ShapeDtypeStruct((b, lp, vw), BF16),
        scratch_shapes=[pltpu.VMEM((GLA_HEADS, GLA_DV, GLA_DK), F32)],
        compiler_params=_params(("parallel", "arbitrary"), VMEM_MID),
        name="gla",
    )(cd3, cd3, cd3, cd3, cd3, wa, ba.reshape(1, -1), o_norm.reshape(1, -1))


def _ab_in_weight(w):
    wt = jnp.swapaxes(w, 1, 2)
    zeros = jnp.zeros((w.shape[0], 3840 - w.shape[2], w.shape[1]), w.dtype)
    return jnp.concatenate([wt, zeros], axis=1).astype(BF16)


def _cd_in_weight(w):
    wt = jnp.swapaxes(w, 1, 2)
    pieces = [wt[:, :6144], wt[:, 6160:10256], wt[:, 10272:], wt[:, 6144:6160],
              wt[:, 10256:10272], jnp.zeros((w.shape[0], 12800 - w.shape[2], w.shape[1]), w.dtype)]
    return jnp.concatenate(pieces, axis=1).astype(BF16)


def _mla_q_weight(w):
    n, r, _ = w.shape
    w = jnp.pad(w.astype(BF16).reshape(n, r, HEADS, MLA_QK),
                ((0, 0), (0, 0), (0, 0), (0, 2 * HEAD_DIM - MLA_QK)))
    return w.reshape(n, r, HEADS * 2 * HEAD_DIM)


def _gla_gate_weight(w):
    return jnp.pad(w.astype(BF16), ((0, 0), (GLA_RANK, PAD_TO - 2 * GLA_RANK), (0, 0)))


def _rotary_tables(lp):
    pos = jnp.arange(lp, dtype=F32)
    inv_freq = ROPE_BASE ** (-jnp.arange(0, MLA_ROPE, 2, dtype=F32) / MLA_ROPE)
    ang = pos[:, None] * inv_freq[None, :]
    cos, sin = jnp.cos(ang), jnp.sin(ang)
    z32, z64 = jnp.zeros_like(cos), jnp.zeros((lp, HEAD_DIM - MLA_ROPE), F32)
    return (jnp.concatenate([cos, cos, z64], axis=1),
            jnp.concatenate([-sin, z32, z64], axis=1),
            jnp.concatenate([z32, sin, z64], axis=1))


def _ffn(h, norm, w_gate, w_up, w_down, layer, name):
    hidden = _norm_matmul(h, norm, [w_gate, w_up], layer, _swiglu_epilogue, BF16, ROW_TILE, 512,
                          name + "_up")
    return _matmul_residual([hidden], w_down, layer, h, 0.5, RES_ROWS, RES_COLS, name + "_down")


def _pool_mla_layer(h, b, lp, norm, w_in, pool_w, pool_scale, q_norm, wq, kv_norm, wkv,
                    q_gain, k_gain, w_out, layer, tables):
    ab = _norm_matmul(h, norm, [w_in], layer, _identity, F32, ROW_TILE, 768, "ab_in",
                      w_transposed=True)
    y_pool = _pool_mixer(ab, pool_w, layer, pool_scale, lp)
    q, k, v = _mla_prep(ab.reshape(b, lp, -1), tables, q_norm, kv_norm, wq, wkv, layer,
                        q_gain, k_gain, MLA_QK ** -0.5 * LOG2E)
    y_mla = _mla_attention(q, k, v)
    return _matmul_residual([y_pool, y_mla.reshape(b * lp, -1)], w_out, layer, h, 1.0,
                            RES_ROWS, RES_COLS, "ab_out")


def _fox_gla_layer(h, b, lp, norm, w_in, fox_q_gain, fox_k_gain, fox_f_bias, wa, gla_b_a,
                   gla_o_norm, w_out, layer):
    cd = _norm_matmul(h, norm, [w_in], layer, _identity, F32, ROW_TILE, 512, "cd_in",
                      w_transposed=True)
    cd3 = cd.reshape(b, lp, -1)
    ff_t = cd[:, 12288:12288 + HEADS].reshape(b, lp, HEADS).transpose(0, 2, 1).reshape(b * HEADS, lp)
    f = _forget_cumsum(ff_t, jnp.tile(fox_f_bias, b).reshape(b * HEADS, 1))
    y_fox = _fox_attention(cd3, f.reshape(b, HEADS, 1, lp), fox_q_gain, fox_k_gain,
                           HEAD_DIM ** -0.5 * LOG2E)
    y_gla = _gla(cd3, wa, layer, gla_b_a, gla_o_norm)
    return _matmul_residual([y_fox.reshape(b * lp, -1), y_gla.reshape(b * lp, -1)], w_out, layer,
                            h, 1.0, RES_ROWS, RES_COLS, "cd_out")


def kernel(x, meta_tokens, ffn1_norm, ffn1_w_gate, ffn1_w_up, ffn1_w_down, mix_norm, ffn2_norm, ffn2_w_gate, ffn2_w_up, ffn2_w_down, ab_w_in, pool_w, pool_scale, mla_q_norm, mla_w_q_up, mla_kv_norm, mla_w_kv_up, mla_q_gain, mla_k_gain, ab_w_out, cd_w_in, fox_q_gain, fox_k_gain, fox_f_bias, gla_w_a2, gla_b_a, gla_o_norm, cd_w_out):
    b, s, d = x.shape
    length = N_META + s
    lp = -(-length // PAD_TO) * PAD_TO
    meta = jnp.broadcast_to(meta_tokens.astype(x.dtype)[None], (b, N_META, d))
    h = jnp.concatenate([meta, x, jnp.zeros((b, lp - length, d), x.dtype)], axis=1)
    h = h.reshape(b * lp, d)
    tables = _rotary_tables(lp)

    ffn1 = [w.astype(BF16) for w in (ffn1_w_gate, ffn1_w_up, ffn1_w_down)]
    ffn2 = [w.astype(BF16) for w in (ffn2_w_gate, ffn2_w_up, ffn2_w_down)]
    ab_in, cd_in = _ab_in_weight(ab_w_in), _cd_in_weight(cd_w_in)
    ab_out, cd_out = ab_w_out.astype(BF16), cd_w_out.astype(BF16)
    pool_wb, wq, wkv = pool_w.astype(BF16), _mla_q_weight(mla_w_q_up), mla_w_kv_up.astype(BF16)
    wa = _gla_gate_weight(gla_w_a2)

    for layer in range(ffn1_norm.shape[0]):
        i = layer // 2
        h = _ffn(h, ffn1_norm[layer], *ffn1, layer, "ffn1")
        if layer % 2 == 0:
            h = _pool_mla_layer(h, b, lp, mix_norm[layer], ab_in, pool_wb, pool_scale[i],
                                mla_q_norm[i], wq, mla_kv_norm[i], wkv, mla_q_gain[i],
                                mla_k_gain[i], ab_out, i, tables)
        else:
            h = _fox_gla_layer(h, b, lp, mix_norm[layer], cd_in, fox_q_gain[i], fox_k_gain[i],
                               fox_f_bias[i], wa, gla_b_a[i], gla_o_norm[i], cd_out, i)
        h = _ffn(h, ffn2_norm[layer], *ffn2, layer, "ffn2")
    return h.reshape(b, lp, d)[:, N_META:N_META + s]
```

```python
import functools
import math

import jax
import jax.numpy as jnp
from jax import lax
from jax.experimental import pallas as pl
from jax.experimental.pallas import tpu as pltpu

F32 = jnp.float32
BF16 = jnp.bfloat16

N_META = 16
PAD_TO = 128
RMS_EPS = 1e-6
ROPE_BASE = 10000.0
CHUNK = 64

POOL_WINDOWS = (2, 4, 8, 16)
POOL_GROUP = 512
HEADS = 16
HEAD_DIM = 128
MLA_ROPE = 64
MLA_QK = 192
GLA_HEADS = 4
GLA_DK = 256
GLA_DV = 512
GLA_TAU = 16.0
GLA_RANK = 16

VMEM_BIG = 58 * 1024 * 1024
VMEM_MID = 40 * 1024 * 1024

ROW_TILE = 768
RES_ROWS = 1056
RES_COLS = 512
SEQ_TILE = 384
ATT_HEADS = 4
NEG_BIG = -1e30
LOG2E = math.log2(math.e)

NT_DIMS = (((1,), (1,)), ((), ()))
TN_DIMS = (((0,), (0,)), ((), ()))


def _silu(x):
    return x / (1.0 + jnp.exp(-x))


def _log_sigmoid(x):
    return jnp.minimum(x, 0.0) - jnp.log(1.0 + jnp.exp(-jnp.abs(x)))


def _rms(x, gain, n=None):
    n = x.shape[-1] if n is None else n
    ss = jnp.sum(x * x, axis=-1, keepdims=True)
    return x * lax.rsqrt(ss / n + RMS_EPS) * gain


def _params(sem, vmem):
    return pltpu.CompilerParams(dimension_semantics=sem, vmem_limit_bytes=vmem)


def _const_spec(shape, layer=None):
    if layer is None:
        nd = len(shape)
        return pl.BlockSpec(shape, lambda *_: (0,) * nd, pipeline_mode=pl.Buffered(1))
    nd = len(shape)
    return pl.BlockSpec((None,) + tuple(shape), lambda *_: (layer,) + (0,) * nd,
                        pipeline_mode=pl.Buffered(1))


def _norm_mm_kernel(a_ref, g_ref, *rest, n_w, epilogue, rows, w_transposed):
    w_refs, o_ref, xn_ref = rest[:n_w], rest[n_w], rest[n_w + 1]
    dims = NT_DIMS if w_transposed else (((1,), (0,)), ((), ()))

    @pl.when(pl.program_id(1) == 0)
    def _():
        def body(c, carry):
            r0 = pl.multiple_of(c * rows, rows)
            a = a_ref[pl.ds(r0, rows), :]
            xn_ref[pl.ds(r0, rows), :] = _rms(a, g_ref[...]).astype(BF16)
            return carry
        lax.fori_loop(0, a_ref.shape[0] // rows, body, 0)

    xn = xn_ref[...]
    outs = [lax.dot_general(xn, w[...], dims, preferred_element_type=F32) for w in w_refs]
    o_ref[...] = epilogue(*outs).astype(o_ref.dtype)


def _norm_matmul(a, gain, ws, layer, epilogue, out_dtype, tm, tn, name, w_transposed=False):
    t, k = a.shape
    n = ws[0].shape[1 if w_transposed else 2]
    kern = functools.partial(_norm_mm_kernel, n_w=len(ws), epilogue=epilogue, rows=64,
                             w_transposed=w_transposed)
    if w_transposed:
        w_spec = pl.BlockSpec((None, tn, k), lambda i, j: (layer, j, 0))
    else:
        w_spec = pl.BlockSpec((None, k, tn), lambda i, j: (layer, 0, j))
    return pl.pallas_call(
        kern,
        grid=(t // tm, n // tn),
        in_specs=[pl.BlockSpec((tm, k), lambda i, j: (i, 0)),
                  pl.BlockSpec((1, k), lambda i, j: (0, 0))] + [w_spec for _ in ws],
        out_specs=pl.BlockSpec((tm, tn), lambda i, j: (i, j)),
        out_shape=jax.ShapeDtypeStruct((t, n), out_dtype),
        scratch_shapes=[pltpu.VMEM((tm, k), BF16)],
        compiler_params=_params(("parallel", "arbitrary"), VMEM_BIG),
        name=name,
    )(a, gain.reshape(1, k), *ws)


def _swiglu_epilogue(g, u):
    return _silu(g) * u


def _identity(x):
    return x


def _mm_res_kernel(*refs, k_sizes, scale):
    n_a = len(k_sizes)
    a_refs, w_ref, r_ref, o_ref = refs[:n_a], refs[n_a], refs[n_a + 1], refs[n_a + 2]
    acc, off = None, 0
    for a_ref, ks in zip(a_refs, k_sizes):
        p = jnp.dot(a_ref[...], w_ref[off:off + ks, :], preferred_element_type=F32)
        acc = p if acc is None else acc + p
        off += ks
    if scale != 1.0:
        acc = scale * acc
    o_ref[...] = r_ref[...] + acc


def _matmul_residual(a_list, w, layer, res, scale, tm, tn, name):
    t, n = res.shape
    k_sizes = tuple(a.shape[1] for a in a_list)
    k = sum(k_sizes)
    kern = functools.partial(_mm_res_kernel, k_sizes=k_sizes, scale=scale)
    n_a = len(a_list)
    return pl.pallas_call(
        kern,
        grid=(t // tm, n // tn),
        in_specs=[pl.BlockSpec((tm, ks), lambda i, j: (i, 0)) for ks in k_sizes]
                 + [pl.BlockSpec((None, k, tn), lambda i, j: (layer, 0, j)),
                    pl.BlockSpec((tm, tn), lambda i, j: (i, j))],
        out_specs=pl.BlockSpec((tm, tn), lambda i, j: (i, j)),
        out_shape=jax.ShapeDtypeStruct((t, n), F32),
        input_output_aliases={n_a + 1: 0},
        compiler_params=_params(("parallel", "arbitrary"), VMEM_BIG),
        name=name,
    )(*a_list, w, res)


def _pool_kernel(x_ref, halo_ref, w_ref, s_ref, o_ref, xs_ref, *, tiles_per_seq):
    tm = x_ref.shape[0]
    halo = POOL_WINDOWS[-1]
    it = pl.program_id(0) % tiles_per_seq
    keep = (it > 0).astype(F32)
    xs_ref[0:halo, :] = halo_ref[...] * keep
    xs_ref[halo:halo + tm, :] = x_ref[...]
    pos = it * tm + lax.broadcasted_iota(jnp.int32, (tm, 1), 0)
    for g, win in enumerate(POOL_WINDOWS):
        c0 = g * POOL_GROUP
        x = xs_ref[halo:halo + tm, c0:c0 + POOL_GROUP]
        acc = x
        for j in range(1, win):
            acc = acc + xs_ref[halo - j:halo - j + tm, c0:c0 + POOL_GROUP]
        cnt = jnp.minimum(pos + 1, win).astype(F32)
        d = acc / cnt - x
        y = jnp.dot(d.astype(BF16), w_ref[g], preferred_element_type=F32)
        o_ref[:, c0:c0 + POOL_GROUP] = (y * s_ref[:, c0:c0 + POOL_GROUP]).astype(o_ref.dtype)


def _pool_mixer(ab, pool_w, layer, pool_scale, lp):
    t = ab.shape[0]
    tm, halo = SEQ_TILE, POOL_WINDOWS[-1]
    width = len(POOL_WINDOWS) * POOL_GROUP
    per_halo = tm // halo
    kern = functools.partial(_pool_kernel, tiles_per_seq=lp // tm)
    return pl.pallas_call(
        kern,
        grid=(t // tm,),
        in_specs=[pl.BlockSpec((tm, width), lambda i: (i, 0)),
                  pl.BlockSpec((halo, width), lambda i: (jnp.maximum(i * per_halo - 1, 0), 0)),
                  _const_spec((len(POOL_WINDOWS), POOL_GROUP, POOL_GROUP), layer),
                  _const_spec((1, width))],
        out_specs=pl.BlockSpec((tm, width), lambda i: (i, 0)),
        out_shape=jax.ShapeDtypeStruct((t, width), BF16),
        scratch_shapes=[pltpu.VMEM((tm + halo, width), F32)],
        compiler_params=_params(("parallel",), VMEM_MID),
        name="pool_mixer",
    )(ab, ab, pool_w, pool_scale.reshape(1, width))


def _rotary_slot(x, c_ref, s1_ref, s2_ref):
    return (x * c_ref[...] + pltpu.roll(x, 96, axis=1) * s1_ref[...]
            + pltpu.roll(x, 32, axis=1) * s2_ref[...])


def _mla_prep_kernel(cq_ref, ckv_ref, kpe_ref, c_ref, s1_ref, s2_ref, qn_ref, kvn_ref,
                     wq_ref, wkv_ref, qgn_ref, qgr_ref, kgn_ref, kgr_ref,
                     q_ref, k_ref, v_ref, *, q_scale):
    cq = _rms(cq_ref[...], qn_ref[...]).astype(BF16)
    ckv = _rms(ckv_ref[...], kvn_ref[...]).astype(BF16)
    kr = _rotary_slot(_rms(kpe_ref[...], kgr_ref[...], MLA_ROPE), c_ref, s1_ref, s2_ref).astype(BF16)
    for h in range(HEADS):
        lo = h * 2 * HEAD_DIM
        mid, hi = lo + HEAD_DIM, lo + 2 * HEAD_DIM
        q = jnp.dot(cq, wq_ref[:, lo:hi], preferred_element_type=F32)
        qn = _rms(q[:, :HEAD_DIM], qgn_ref[...])
        qr = _rotary_slot(_rms(q[:, HEAD_DIM:], qgr_ref[...], MLA_ROPE), c_ref, s1_ref, s2_ref)
        q_ref[:, lo:mid] = (qn * q_scale).astype(BF16)
        q_ref[:, mid:hi] = (qr * q_scale).astype(BF16)
        kv = jnp.dot(ckv, wkv_ref[:, lo:hi], preferred_element_type=F32)
        k_ref[:, lo:mid] = _rms(kv[:, :HEAD_DIM], kgn_ref[...]).astype(BF16)
        k_ref[:, mid:hi] = kr
        v_ref[:, h * HEAD_DIM:(h + 1) * HEAD_DIM] = kv[:, HEAD_DIM:].astype(BF16)


def _mla_prep(ab3, tables, q_norm, kv_norm, wq, wkv, layer, q_gain, k_gain, q_scale):
    b, lp, _ = ab3.shape
    tm = SEQ_TILE
    q_rank, kv_rank = wq.shape[1], wkv.shape[1]
    cat = HEADS * 2 * HEAD_DIM
    zeros = jnp.zeros((HEAD_DIM - MLA_ROPE,), F32)
    slot = lambda g: jnp.concatenate([g[HEAD_DIM:], zeros]).reshape(1, HEAD_DIM)
    row = lambda i_, j_: (j_, 0)
    out = jax.ShapeDtypeStruct((b, lp, cat), BF16)
    return pl.pallas_call(
        functools.partial(_mla_prep_kernel, q_scale=q_scale),
        grid=(b, lp // tm),
        in_specs=[pl.BlockSpec((None, tm, q_rank), lambda i, j: (i, j, 2048 // q_rank)),
                  pl.BlockSpec((None, tm, kv_rank), lambda i, j: (i, j, 3072 // kv_rank)),
                  pl.BlockSpec((None, tm, HEAD_DIM), lambda i, j: (i, j, 3584 // HEAD_DIM)),
                  pl.BlockSpec((tm, HEAD_DIM), row),
                  pl.BlockSpec((tm, HEAD_DIM), row),
                  pl.BlockSpec((tm, HEAD_DIM), row),
                  _const_spec((1, q_rank)), _const_spec((1, kv_rank)),
                  _const_spec((q_rank, cat), layer), _const_spec((kv_rank, cat), layer),
                  _const_spec((1, HEAD_DIM)), _const_spec((1, HEAD_DIM)),
                  _const_spec((1, HEAD_DIM)), _const_spec((1, HEAD_DIM))],
        out_specs=[pl.BlockSpec((None, tm, cat), lambda i, j: (i, j, 0))] * 2
                  + [pl.BlockSpec((None, tm, HEADS * HEAD_DIM), lambda i, j: (i, j, 0))],
        out_shape=[out, out, jax.ShapeDtypeStruct((b, lp, HEADS * HEAD_DIM), BF16)],
        compiler_params=_params(("parallel", "parallel"), VMEM_BIG),
        name="mla_prep",
    )(ab3, ab3, ab3, *tables, q_norm.reshape(1, -1), kv_norm.reshape(1, -1), wq, wkv,
      q_gain[:HEAD_DIM].reshape(1, HEAD_DIM), slot(q_gain),
      k_gain[:HEAD_DIM].reshape(1, HEAD_DIM), slot(k_gain))


def _softmax_sweep(qs, key_fn, val_fn, bias_fn, n_full, tq, diag_mask, tail, scratch):
    nh = len(qs)
    s_even, s_odd, m_ref, l_ref, acc_ref = scratch

    def scores(h, start, width):
        s = lax.dot_general(qs[h], key_fn(h, start, width), NT_DIMS, preferred_element_type=F32)
        return bias_fn(h, s, start, width)

    def lanes(x, width):
        return jnp.concatenate([x] * (width // PAD_TO), axis=1)

    def absorb(h, s, vals):
        m = m_ref[h]
        m_new = jnp.maximum(m, jnp.max(s, axis=-1, keepdims=True))
        alpha = jnp.exp2(m - m_new)
        p = jnp.exp2(s - lanes(m_new, s.shape[1]))
        p_sum = p[:, :PAD_TO]
        for c in range(1, s.shape[1] // PAD_TO):
            p_sum = p_sum + p[:, c * PAD_TO:(c + 1) * PAD_TO]
        l_ref[h] = alpha * l_ref[h] + p_sum
        acc_ref[h] = alpha * acc_ref[h] + jnp.dot(p.astype(BF16), vals, preferred_element_type=F32)
        m_ref[h] = m_new

    def half_step(c, cur, nxt, diagonal):
        start = pl.multiple_of(c * tq, tq)
        for h in range(nh):
            if not diagonal:
                nxt[h] = scores(h, start + tq, tq)
                absorb(h, cur[h], val_fn(h, start, tq))
                continue
            s = jnp.where(diag_mask(), cur[h], NEG_BIG)
            vals = val_fn(h, start, tq)
            if tail is not None:
                t_start, t_width, t_mask = tail
                s_tail = jnp.where(t_mask(), scores(h, t_start, t_width), NEG_BIG)
                s = jnp.concatenate([s, s_tail], axis=1)
                vals = jnp.concatenate([vals, val_fn(h, t_start, t_width)], axis=0)
            absorb(h, s, vals)

    def by_parity(c, diagonal):
        return lambda: lax.cond(c % 2 == 0, lambda: half_step(c, s_even, s_odd, diagonal),
                                lambda: half_step(c, s_odd, s_even, diagonal))

    def step(c, carry):
        lax.cond(c == n_full, by_parity(c, True), by_parity(c, False))
        return carry

    for h in range(nh):
        m_ref[h] = jnp.full((tq, PAD_TO), NEG_BIG, F32)
        l_ref[h] = jnp.zeros((tq, PAD_TO), F32)
        acc_ref[h] = jnp.zeros((tq, HEAD_DIM), F32)
        s_even[h] = scores(h, 0, tq)
    lax.fori_loop(0, n_full + 1, step, 0)
    return [acc_ref[h] / jnp.sum(l_ref[h], axis=-1, keepdims=True) for h in range(nh)]


def _sweep_scratch(tq):
    return [pltpu.VMEM((ATT_HEADS, tq, tq), F32), pltpu.VMEM((ATT_HEADS, tq, tq), F32),
            pltpu.VMEM((ATT_HEADS, tq, PAD_TO), F32), pltpu.VMEM((ATT_HEADS, tq, PAD_TO), F32),
            pltpu.VMEM((ATT_HEADS, tq, HEAD_DIM), F32)]


def _mla_attn_kernel(q_ref, k_ref, v_ref, o_ref, *scratch):
    tq, lk = q_ref.shape[0], k_ref.shape[0]
    width = 2 * HEAD_DIM
    i = pl.program_id(2)
    q0 = i * tq
    shift = CHUNK - N_META

    def chunk_mask(k_start, k_width):
        bits = CHUNK.bit_length() - 1
        qc = lax.shift_right_logical(
            q0 + shift + lax.broadcasted_iota(jnp.int32, (tq, k_width), 0), bits)
        kc = lax.shift_right_logical(
            k_start + shift + lax.broadcasted_iota(jnp.int32, (tq, k_width), 1), bits)
        return kc <= qc

    t_start = pl.multiple_of(jnp.minimum(q0 + tq, lk - PAD_TO), PAD_TO)

    def t_mask():
        t_kpos = t_start + lax.broadcasted_iota(jnp.int32, (tq, PAD_TO), 1)
        return (t_kpos >= q0 + tq) & chunk_mask(t_start, PAD_TO)

    qs = [q_ref[:, h * width:(h + 1) * width] for h in range(ATT_HEADS)]
    outs = _softmax_sweep(
        qs,
        lambda h, s0, w: k_ref[pl.ds(s0, w), h * width:(h + 1) * width],
        lambda h, s0, w: v_ref[pl.ds(s0, w), h * HEAD_DIM:(h + 1) * HEAD_DIM],
        lambda h, s, s0, w: s,
        i, tq, lambda: chunk_mask(q0, tq), (t_start, PAD_TO, t_mask), scratch)
    for h in range(ATT_HEADS):
        o_ref[:, h * HEAD_DIM:(h + 1) * HEAD_DIM] = outs[h].astype(o_ref.dtype)


def _mla_attention(q, k, v):
    b, lp, _ = q.shape
    tq = SEQ_TILE
    gw = ATT_HEADS * 2 * HEAD_DIM
    return pl.pallas_call(
        _mla_attn_kernel,
        grid=(b, HEADS // ATT_HEADS, lp // tq),
        in_specs=[pl.BlockSpec((None, tq, gw), lambda bi, g, i: (bi, i, g)),
                  pl.BlockSpec((None, lp, gw), lambda bi, g, i: (bi, 0, g)),
                  pl.BlockSpec((None, lp, ATT_HEADS * HEAD_DIM), lambda bi, g, i: (bi, 0, g))],
        out_specs=pl.BlockSpec((None, tq, ATT_HEADS * HEAD_DIM), lambda bi, g, i: (bi, i, g)),
        out_shape=jax.ShapeDtypeStruct((b, lp, HEADS * HEAD_DIM), BF16),
        scratch_shapes=_sweep_scratch(tq),
        compiler_params=_params(("parallel", "parallel", "arbitrary"), VMEM_BIG),
        name="mla_attention",
    )(q, k, v)


def _fox_attn_kernel(q_ref, k_ref, v_ref, f_ref, qg_ref, kg_ref, o_ref, kn_ref, va_ref, fb_ref,
                     *scratch, q_scale):
    tq, lk = q_ref.shape[0], k_ref.shape[0]
    i = pl.program_id(2)
    q0 = pl.multiple_of(i * tq, tq)
    heads = [slice(h * HEAD_DIM, (h + 1) * HEAD_DIM) for h in range(ATT_HEADS)]

    @pl.when(i == 0)
    def _():
        def body(c, carry):
            rows = pl.ds(pl.multiple_of(c * tq, tq), tq)
            for sl in heads:
                kn_ref[rows, sl] = _rms(k_ref[rows, sl], kg_ref[...]).astype(BF16)
            va_ref[rows, :] = v_ref[rows, :].astype(BF16)
            return carry
        lax.fori_loop(0, lk // tq, body, 0)
        fb_ref[...] = f_ref[...] * LOG2E

    qs = [(_rms(q_ref[:, sl], qg_ref[...]) * q_scale).astype(BF16) for sl in heads]
    f0 = [jnp.max(fb_ref[h, :, pl.ds(q0, tq)], axis=-1, keepdims=True) for h in range(ATT_HEADS)]

    def causal():
        return (lax.broadcasted_iota(jnp.int32, (tq, tq), 1)
                <= lax.broadcasted_iota(jnp.int32, (tq, tq), 0))

    outs = _softmax_sweep(
        qs,
        lambda h, s0, w: kn_ref[pl.ds(s0, w), heads[h]],
        lambda h, s0, w: va_ref[pl.ds(s0, w), heads[h]],
        lambda h, s, s0, w: s - (fb_ref[h, :, pl.ds(s0, w)] - f0[h]),
        i, tq, causal, None, scratch)
    for h, sl in enumerate(heads):
        o_ref[:, sl] = outs[h].astype(o_ref.dtype)


def _fox_attention(cd3, f, q_gain, k_gain, q_scale):
    b, lp, _ = cd3.shape
    tq = SEQ_TILE
    gw = ATT_HEADS * HEAD_DIM
    groups = HEADS // ATT_HEADS
    kern = functools.partial(_fox_attn_kernel, q_scale=q_scale)
    return pl.pallas_call(
        kern,
        grid=(b, groups, lp // tq),
        in_specs=[pl.BlockSpec((None, tq, gw), lambda bi, g, i: (bi, i, g)),
                  pl.BlockSpec((None, lp, gw), lambda bi, g, i: (bi, 0, groups + g)),
                  pl.BlockSpec((None, lp, gw), lambda bi, g, i: (bi, 0, 2 * groups + g)),
                  pl.BlockSpec((None, ATT_HEADS, 1, lp), lambda bi, g, i: (bi, g, 0, 0)),
                  _const_spec((1, HEAD_DIM)), _const_spec((1, HEAD_DIM))],
        out_specs=pl.BlockSpec((None, tq, gw), lambda bi, g, i: (bi, i, g)),
        out_shape=jax.ShapeDtypeStruct((b, lp, HEADS * HEAD_DIM), BF16),
        scratch_shapes=[pltpu.VMEM((lp, gw), BF16),
                        pltpu.VMEM((lp, gw), BF16),
                        pltpu.VMEM((ATT_HEADS, 1, lp), F32)] + _sweep_scratch(tq),
        compiler_params=_params(("parallel", "parallel", "arbitrary"), VMEM_BIG),
        name="fox_attention",
    )(cd3, cd3, cd3, f, q_gain.reshape(1, HEAD_DIM), k_gain.reshape(1, HEAD_DIM))


def _forget_cumsum_kernel(x_ref, b_ref, o_ref):
    rows, lp = x_ref.shape
    lane = lax.broadcasted_iota(jnp.int32, (rows, PAD_TO), 1)
    carry = jnp.zeros((rows, 1), F32)
    for c in range(lp // PAD_TO):
        sl = slice(c * PAD_TO, (c + 1) * PAD_TO)
        x = _log_sigmoid(x_ref[:, sl] + b_ref[...])
        step = 1
        while step < PAD_TO:
            x = x + jnp.where(lane >= step, pltpu.roll(x, step, axis=1), 0.0)
            step *= 2
        x = x + carry
        o_ref[:, sl] = x
        carry = x[:, PAD_TO - 1:PAD_TO]


def _forget_cumsum(ff_t, bias_col):
    rows, lp = ff_t.shape
    return pl.pallas_call(
        _forget_cumsum_kernel,
        out_shape=jax.ShapeDtypeStruct((rows, lp), F32),
        name="fox_forget_cumsum",
    )(ff_t, bias_col)


def _cumsum_rows(x):
    row = lax.broadcasted_iota(jnp.int32, x.shape, 0)
    step = 1
    while step < x.shape[0]:
        x = x + jnp.where(row >= step, pltpu.roll(x, step, axis=0), 0.0)
        step *= 2
    return x


def _gla_kernel(q_ref, k_ref, v_ref, r_ref, a_ref, wa_ref, ba_ref, on_ref, o_ref, st_ref):
    tm = q_ref.shape[0]

    @pl.when(pl.program_id(1) == 0)
    def _():
        st_ref[...] = jnp.zeros_like(st_ref)

    causal = (lax.broadcasted_iota(jnp.int32, (CHUNK, CHUNK), 1)
              <= lax.broadcasted_iota(jnp.int32, (CHUNK, CHUNK), 0))

    def body(c, carry):
        r0 = pl.multiple_of(c * CHUNK, CHUNK)
        rows = pl.ds(r0, CHUNK)
        gate = jnp.dot(a_ref[rows, :].astype(BF16), wa_ref[...], preferred_element_type=F32)
        for h in range(GLA_HEADS):
            ks = slice(h * GLA_DK, (h + 1) * GLA_DK)
            vs = slice(h * GLA_DV, (h + 1) * GLA_DV)
            g = _log_sigmoid(gate[:, ks] + ba_ref[:, ks]) / GLA_TAU
            bcum = _cumsum_rows(g)
            b_last = bcum[CHUNK - 1:CHUNK, :]
            k = k_ref[rows, ks]
            v = v_ref[rows, vs].astype(BF16)
            q_dec = ((q_ref[rows, ks] * (GLA_DK ** -0.5)) * jnp.exp(bcum)).astype(BF16)
            k_inv = (k * jnp.exp(-bcum)).astype(BF16)
            k_end = (k * jnp.exp(b_last - bcum)).astype(BF16)
            a = lax.dot_general(q_dec, k_inv, NT_DIMS, preferred_element_type=F32)
            a = jnp.where(causal, a, 0.0).astype(BF16)
            st = st_ref[h]
            o = (jnp.dot(a, v, preferred_element_type=F32)
                 + lax.dot_general(q_dec, st.astype(BF16), NT_DIMS, preferred_element_type=F32))
            st_ref[h] = st * jnp.exp(b_last) + lax.dot_general(v, k_end, TN_DIMS,
                                                               preferred_element_type=F32)
            o = _rms(o, on_ref[...])
            o_ref[rows, vs] = (o * _silu(r_ref[rows, vs])).astype(o_ref.dtype)
        return carry

    lax.fori_loop(0, tm // CHUNK, body, 0)


def _gla(cd3, wa, layer, ba, o_norm):
    b, lp, _ = cd3.shape
    tm = SEQ_TILE
    kw, vw = GLA_HEADS * GLA_DK, GLA_HEADS * GLA_DV
    return pl.pallas_call(
        _gla_kernel,
        grid=(b, lp // tm),
        in_specs=[pl.BlockSpec((None, tm, kw), lambda bi, i: (bi, i, 6144 // kw)),
                  pl.BlockSpec((None, tm, kw), lambda bi, i: (bi, i, 7168 // kw)),
                  pl.BlockSpec((None, tm, vw), lambda bi, i: (bi, i, 8192 // vw)),
                  pl.BlockSpec((None, tm, vw), lambda bi, i: (bi, i, 10240 // vw)),
                  pl.BlockSpec((None, tm, PAD_TO), lambda bi, i: (bi, i, 12288 // PAD_TO)),
                  _const_spec((PAD_TO, kw), layer), _const_spec((1, kw)), _const_spec((1, GLA_DV))],
        out_specs=pl.BlockSpec((None, tm, vw), lambda bi, i: (bi, i, 0)),
        out_shape=jax.ShapeDtypeStruct((b, lp, vw), BF16),
        scratch_shapes=[pltpu.VMEM((GLA_HEADS, GLA_DV, GLA_DK), F32)],
        compiler_params=_params(("parallel", "arbitrary"), VMEM_MID),
        name="gla",
    )(cd3, cd3, cd3, cd3, cd3, wa, ba.reshape(1, -1), o_norm.reshape(1, -1))


def _ab_in_weight(w):
    wt = jnp.swapaxes(w, 1, 2)
    zeros = jnp.zeros((w.shape[0], 3840 - w.shape[2], w.shape[1]), w.dtype)
    return jnp.concatenate([wt, zeros], axis=1).astype(BF16)


def _cd_in_weight(w):
    wt = jnp.swapaxes(w, 1, 2)
    pieces = [wt[:, :6144], wt[:, 6160:10256], wt[:, 10272:], wt[:, 6144:6160],
              wt[:, 10256:10272], jnp.zeros((w.shape[0], 12800 - w.shape[2], w.shape[1]), w.dtype)]
    return jnp.concatenate(pieces, axis=1).astype(BF16)


def _mla_q_weight(w):
    n, r, _ = w.shape
    w = jnp.pad(w.astype(BF16).reshape(n, r, HEADS, MLA_QK),
                ((0, 0), (0, 0), (0, 0), (0, 2 * HEAD_DIM - MLA_QK)))
    return w.reshape(n, r, HEADS * 2 * HEAD_DIM)


def _gla_gate_weight(w):
    return jnp.pad(w.astype(BF16), ((0, 0), (GLA_RANK, PAD_TO - 2 * GLA_RANK), (0, 0)))


def _rotary_tables(lp):
    pos = jnp.arange(lp, dtype=F32)
    inv_freq = ROPE_BASE ** (-jnp.arange(0, MLA_ROPE, 2, dtype=F32) / MLA_ROPE)
    ang = pos[:, None] * inv_freq[None, :]
    cos, sin = jnp.cos(ang), jnp.sin(ang)
    z32, z64 = jnp.zeros_like(cos), jnp.zeros((lp, HEAD_DIM - MLA_ROPE), F32)
    return (jnp.concatenate([cos, cos, z64], axis=1),
            jnp.concatenate([-sin, z32, z64], axis=1),
            jnp.concatenate([z32, sin, z64], axis=1))


def _ffn(h, norm, w_gate, w_up, w_down, layer, name):
    hidden = _norm_matmul(h, norm, [w_gate, w_up], layer, _swiglu_epilogue, BF16, ROW_TILE, 512,
                          name + "_up")
    return _matmul_residual([hidden], w_down, layer, h, 0.5, RES_ROWS, RES_COLS, name + "_down")


def _pool_mla_layer(h, b, lp, norm, w_in, pool_w, pool_scale, q_norm, wq, kv_norm, wkv,
                    q_gain, k_gain, w_out, layer, tables):
    ab = _norm_matmul(h, norm, [w_in], layer, _identity, F32, ROW_TILE, 768, "ab_in",
                      w_transposed=True)
    y_pool = _pool_mixer(ab, pool_w, layer, pool_scale, lp)
    q, k, v = _mla_prep(ab.reshape(b, lp, -1), tables, q_norm, kv_norm, wq, wkv, layer,
                        q_gain, k_gain, MLA_QK ** -0.5 * LOG2E)
    y_mla = _mla_attention(q, k, v)
    return _matmul_residual([y_pool, y_mla.reshape(b * lp, -1)], w_out, layer, h, 1.0,
                            RES_ROWS, RES_COLS, "ab_out")


def _fox_gla_layer(h, b, lp, norm, w_in, fox_q_gain, fox_k_gain, fox_f_bias, wa, gla_b_a,
                   gla_o_norm, w_out, layer):
    cd = _norm_matmul(h, norm, [w_in], layer, _identity, F32, ROW_TILE, 512, "cd_in",
                      w_transposed=True)
    cd3 = cd.reshape(b, lp, -1)
    ff_t = cd[:, 12288:12288 + HEADS].reshape(b, lp, HEADS).transpose(0, 2, 1).reshape(b * HEADS, lp)
    f = _forget_cumsum(ff_t, jnp.tile(fox_f_bias, b).reshape(b * HEADS, 1))
    y_fox = _fox_attention(cd3, f.reshape(b, HEADS, 1, lp), fox_q_gain, fox_k_gain,
                           HEAD_DIM ** -0.5 * LOG2E)
    y_gla = _gla(cd3, wa, layer, gla_b_a, gla_o_norm)
    return _matmul_residual([y_fox.reshape(b * lp, -1), y_gla.reshape(b * lp, -1)], w_out, layer,
                            h, 1.0, RES_ROWS, RES_COLS, "cd_out")


def kernel(x, meta_tokens, ffn1_norm, ffn1_w_gate, ffn1_w_up, ffn1_w_down, mix_norm, ffn2_norm, ffn2_w_gate, ffn2_w_up, ffn2_w_down, ab_w_in, pool_w, pool_scale, mla_q_norm, mla_w_q_up, mla_kv_norm, mla_w_kv_up, mla_q_gain, mla_k_gain, ab_w_out, cd_w_in, fox_q_gain, fox_k_gain, fox_f_bias, gla_w_a2, gla_b_a, gla_o_norm, cd_w_out):
    b, s, d = x.shape
    length = N_META + s
    lp = -(-length // PAD_TO) * PAD_TO
    meta = jnp.broadcast_to(meta_tokens.astype(x.dtype)[None], (b, N_META, d))
    h = jnp.concatenate([meta, x, jnp.zeros((b, lp - length, d), x.dtype)], axis=1)
    h = h.reshape(b * lp, d)
    tables = _rotary_tables(lp)

    ffn1 = [w.astype(BF16) for w in (ffn1_w_gate, ffn1_w_up, ffn1_w_down)]
    ffn2 = [w.astype(BF16) for w in (ffn2_w_gate, ffn2_w_up, ffn2_w_down)]
    ab_in, cd_in = _ab_in_weight(ab_w_in), _cd_in_weight(cd_w_in)
    ab_out, cd_out = ab_w_out.astype(BF16), cd_w_out.astype(BF16)
    pool_wb, wq, wkv = pool_w.astype(BF16), _mla_q_weight(mla_w_q_up), mla_w_kv_up.astype(BF16)
    wa = _gla_gate_weight(gla_w_a2)

    for layer in range(ffn1_norm.shape[0]):
        i = layer // 2
        h = _ffn(h, ffn1_norm[layer], *ffn1, layer, "ffn1")
        if layer % 2 == 0:
            h = _pool_mla_layer(h, b, lp, mix_norm[layer], ab_in, pool_wb, pool_scale[i],
                                mla_q_norm[i], wq, mla_kv_norm[i], wkv, mla_q_gain[i],
                                mla_k_gain[i], ab_out, i, tables)
        else:
            h = _fox_gla_layer(h, b, lp, mix_norm[layer], cd_in, fox_q_gain[i], fox_k_gain[i],
                               fox_f_bias[i], wa, gla_b_a[i], gla_o_norm[i], cd_out, i)
        h = _ffn(h, ffn2_norm[layer], *ffn2, layer, "ffn2")
    return h.reshape(b, lp, d)[:, N_META:N_META + s]
```

```python
import functools
import math

import jax
import jax.numpy as jnp
from jax import lax
from jax.experimental import pallas as pl
from jax.experimental.pallas import tpu as pltpu

F32 = jnp.float32
BF16 = jnp.bfloat16

N_META = 16
PAD_TO = 128
RMS_EPS = 1e-6
ROPE_BASE = 10000.0
CHUNK = 64

POOL_WINDOWS = (2, 4, 8, 16)
POOL_GROUP = 512
HEADS = 16
HEAD_DIM = 128
MLA_ROPE = 64
MLA_QK = 192
GLA_HEADS = 4
GLA_DK = 256
GLA_DV = 512
GLA_TAU = 16.0
GLA_RANK = 16

VMEM_BIG = 58 * 1024 * 1024
VMEM_MID = 40 * 1024 * 1024

ROW_TILE = 768
RES_ROWS = 1056
RES_COLS = 512
OUT_COLS = 1024
SEQ_TILE = 384
ATT_HEADS = 4
NEG_BIG = -1e30
LOG2E = math.log2(math.e)

NT_DIMS = (((1,), (1,)), ((), ()))
TN_DIMS = (((0,), (0,)), ((), ()))


def _silu(x):
    return x / (1.0 + jnp.exp(-x))


def _log_sigmoid(x):
    return jnp.minimum(x, 0.0) - jnp.log(1.0 + jnp.exp(-jnp.abs(x)))


def _rms(x, gain, n=None):
    n = x.shape[-1] if n is None else n
    ss = jnp.sum(x * x, axis=-1, keepdims=True)
    return x * lax.rsqrt(ss / n + RMS_EPS) * gain


def _params(sem, vmem):
    return pltpu.CompilerParams(dimension_semantics=sem, vmem_limit_bytes=vmem)


def _const_spec(shape, layer=None):
    if layer is None:
        nd = len(shape)
        return pl.BlockSpec(shape, lambda *_: (0,) * nd, pipeline_mode=pl.Buffered(1))
    nd = len(shape)
    return pl.BlockSpec((None,) + tuple(shape), lambda *_: (layer,) + (0,) * nd,
                        pipeline_mode=pl.Buffered(1))


def _norm_mm_kernel(a_ref, g_ref, *rest, n_w, epilogue, rows, w_transposed):
    w_refs, o_ref, xn_ref = rest[:n_w], rest[n_w], rest[n_w + 1]
    dims = NT_DIMS if w_transposed else (((1,), (0,)), ((), ()))

    @pl.when(pl.program_id(1) == 0)
    def _():
        def body(c, carry):
            r0 = pl.multiple_of(c * rows, rows)
            a = a_ref[pl.ds(r0, rows), :]
            xn_ref[pl.ds(r0, rows), :] = _rms(a, g_ref[...]).astype(BF16)
            return carry
        lax.fori_loop(0, a_ref.shape[0] // rows, body, 0)

    xn = xn_ref[...]
    outs = [lax.dot_general(xn, w[...], dims, preferred_element_type=F32) for w in w_refs]
    o_ref[...] = epilogue(*outs).astype(o_ref.dtype)


def _norm_matmul(a, gain, ws, layer, epilogue, out_dtype, tm, tn, name, w_transposed=False):
    t, k = a.shape
    n = ws[0].shape[1 if w_transposed else 2]
    kern = functools.partial(_norm_mm_kernel, n_w=len(ws), epilogue=epilogue, rows=64,
                             w_transposed=w_transposed)
    if w_transposed:
        w_spec = pl.BlockSpec((None, tn, k), lambda i, j: (layer, j, 0))
    else:
        w_spec = pl.BlockSpec((None, k, tn), lambda i, j: (layer, 0, j))
    return pl.pallas_call(
        kern,
        grid=(t // tm, n // tn),
        in_specs=[pl.BlockSpec((tm, k), lambda i, j: (i, 0)),
                  pl.BlockSpec((1, k), lambda i, j: (0, 0))] + [w_spec for _ in ws],
        out_specs=pl.BlockSpec((tm, tn), lambda i, j: (i, j)),
        out_shape=jax.ShapeDtypeStruct((t, n), out_dtype),
        scratch_shapes=[pltpu.VMEM((tm, k), BF16)],
        compiler_params=_params(("parallel", "arbitrary"), VMEM_BIG),
        name=name,
    )(a, gain.reshape(1, k), *ws)


def _swiglu_epilogue(g, u):
    return _silu(g) * u


def _identity(x):
    return x


def _mm_res_kernel(*refs, k_sizes, scale):
    n_a = len(k_sizes)
    a_refs, w_ref, r_ref, o_ref = refs[:n_a], refs[n_a], refs[n_a + 1], refs[n_a + 2]
    acc, off = None, 0
    for a_ref, ks in zip(a_refs, k_sizes):
        p = jnp.dot(a_ref[...], w_ref[off:off + ks, :], preferred_element_type=F32)
        acc = p if acc is None else acc + p
        off += ks
    if scale != 1.0:
        acc = scale * acc
    o_ref[...] = r_ref[...] + acc


def _matmul_residual(a_list, w, layer, res, scale, tm, tn, name):
    t, n = res.shape
    k_sizes = tuple(a.shape[1] for a in a_list)
    k = sum(k_sizes)
    kern = functools.partial(_mm_res_kernel, k_sizes=k_sizes, scale=scale)
    n_a = len(a_list)
    return pl.pallas_call(
        kern,
        grid=(t // tm, n // tn),
        in_specs=[pl.BlockSpec((tm, ks), lambda i, j: (i, 0)) for ks in k_sizes]
                 + [pl.BlockSpec((None, k, tn), lambda i, j: (layer, 0, j)),
                    pl.BlockSpec((tm, tn), lambda i, j: (i, j))],
        out_specs=pl.BlockSpec((tm, tn), lambda i, j: (i, j)),
        out_shape=jax.ShapeDtypeStruct((t, n), F32),
        input_output_aliases={n_a + 1: 0},
        compiler_params=_params(("parallel", "arbitrary"), VMEM_BIG),
        name=name,
    )(*a_list, w, res)


def _pool_kernel(x_ref, halo_ref, w_ref, s_ref, o_ref, xs_ref, *, tiles_per_seq):
    tm = x_ref.shape[0]
    halo = POOL_WINDOWS[-1]
    it = pl.program_id(0) % tiles_per_seq
    keep = (it > 0).astype(F32)
    xs_ref[0:halo, :] = halo_ref[...] * keep
    xs_ref[halo:halo + tm, :] = x_ref[...]
    pos = it * tm + lax.broadcasted_iota(jnp.int32, (tm, 1), 0)
    for g, win in enumerate(POOL_WINDOWS):
        c0 = g * POOL_GROUP
        x = xs_ref[halo:halo + tm, c0:c0 + POOL_GROUP]
        acc = x
        for j in range(1, win):
            acc = acc + xs_ref[halo - j:halo - j + tm, c0:c0 + POOL_GROUP]
        cnt = jnp.minimum(pos + 1, win).astype(F32)
        d = acc / cnt - x
        y = jnp.dot(d.astype(BF16), w_ref[g], preferred_element_type=F32)
        o_ref[:, c0:c0 + POOL_GROUP] = (y * s_ref[:, c0:c0 + POOL_GROUP]).astype(o_ref.dtype)


def _pool_mixer(ab, pool_w, layer, pool_scale, lp):
    t = ab.shape[0]
    tm, halo = SEQ_TILE, POOL_WINDOWS[-1]
    width = len(POOL_WINDOWS) * POOL_GROUP
    per_halo = tm // halo
    kern = functools.partial(_pool_kernel, tiles_per_seq=lp // tm)
    return pl.pallas_call(
        kern,
        grid=(t // tm,),
        in_specs=[pl.BlockSpec((tm, width), lambda i: (i, 0)),
                  pl.BlockSpec((halo, width), lambda i: (jnp.maximum(i * per_halo - 1, 0), 0)),
                  _const_spec((len(POOL_WINDOWS), POOL_GROUP, POOL_GROUP), layer),
                  _const_spec((1, width))],
        out_specs=pl.BlockSpec((tm, width), lambda i: (i, 0)),
        out_shape=jax.ShapeDtypeStruct((t, width), BF16),
        scratch_shapes=[pltpu.VMEM((tm + halo, width), F32)],
        compiler_params=_params(("parallel",), VMEM_MID),
        name="pool_mixer",
    )(ab, ab, pool_w, pool_scale.reshape(1, width))


def _rotary_slot(x, c_ref, s1_ref, s2_ref):
    return (x * c_ref[...] + pltpu.roll(x, 96, axis=1) * s1_ref[...]
            + pltpu.roll(x, 32, axis=1) * s2_ref[...])


def _mla_prep_kernel(cq_ref, ckv_ref, kpe_ref, c_ref, s1_ref, s2_ref, qn_ref, kvn_ref,
                     wq_ref, wkv_ref, qgn_ref, qgr_ref, kgn_ref, kgr_ref,
                     q_ref, k_ref, v_ref, *, q_scale):
    cq = _rms(cq_ref[...], qn_ref[...]).astype(BF16)
    ckv = _rms(ckv_ref[...], kvn_ref[...]).astype(BF16)
    kr = _rotary_slot(_rms(kpe_ref[...], kgr_ref[...], MLA_ROPE), c_ref, s1_ref, s2_ref).astype(BF16)
    for h in range(HEADS):
        lo = h * 2 * HEAD_DIM
        mid, hi = lo + HEAD_DIM, lo + 2 * HEAD_DIM
        q = jnp.dot(cq, wq_ref[:, lo:hi], preferred_element_type=F32)
        qn = _rms(q[:, :HEAD_DIM], qgn_ref[...])
        qr = _rotary_slot(_rms(q[:, HEAD_DIM:], qgr_ref[...], MLA_ROPE), c_ref, s1_ref, s2_ref)
        q_ref[:, lo:mid] = (qn * q_scale).astype(BF16)
        q_ref[:, mid:hi] = (qr * q_scale).astype(BF16)
        kv = jnp.dot(ckv, wkv_ref[:, lo:hi], preferred_element_type=F32)
        k_ref[:, lo:mid] = _rms(kv[:, :HEAD_DIM], kgn_ref[...]).astype(BF16)
        k_ref[:, mid:hi] = kr
        v_ref[:, h * HEAD_DIM:(h + 1) * HEAD_DIM] = kv[:, HEAD_DIM:].astype(BF16)


def _mla_prep(ab3, tables, q_norm, kv_norm, wq, wkv, layer, q_gain, k_gain, q_scale):
    b, lp, _ = ab3.shape
    tm = SEQ_TILE
    q_rank, kv_rank = wq.shape[1], wkv.shape[1]
    cat = HEADS * 2 * HEAD_DIM
    zeros = jnp.zeros((HEAD_DIM - MLA_ROPE,), F32)
    slot = lambda g: jnp.concatenate([g[HEAD_DIM:], zeros]).reshape(1, HEAD_DIM)
    row = lambda i_, j_: (j_, 0)
    out = jax.ShapeDtypeStruct((b, lp, cat), BF16)
    return pl.pallas_call(
        functools.partial(_mla_prep_kernel, q_scale=q_scale),
        grid=(b, lp // tm),
        in_specs=[pl.BlockSpec((None, tm, q_rank), lambda i, j: (i, j, 2048 // q_rank)),
                  pl.BlockSpec((None, tm, kv_rank), lambda i, j: (i, j, 3072 // kv_rank)),
                  pl.BlockSpec((None, tm, HEAD_DIM), lambda i, j: (i, j, 3584 // HEAD_DIM)),
                  pl.BlockSpec((tm, HEAD_DIM), row),
                  pl.BlockSpec((tm, HEAD_DIM), row),
                  pl.BlockSpec((tm, HEAD_DIM), row),
                  _const_spec((1, q_rank)), _const_spec((1, kv_rank)),
                  _const_spec((q_rank, cat), layer), _const_spec((kv_rank, cat), layer),
                  _const_spec((1, HEAD_DIM)), _const_spec((1, HEAD_DIM)),
                  _const_spec((1, HEAD_DIM)), _const_spec((1, HEAD_DIM))],
        out_specs=[pl.BlockSpec((None, tm, cat), lambda i, j: (i, j, 0))] * 2
                  + [pl.BlockSpec((None, tm, HEADS * HEAD_DIM), lambda i, j: (i, j, 0))],
        out_shape=[out, out, jax.ShapeDtypeStruct((b, lp, HEADS * HEAD_DIM), BF16)],
        compiler_params=_params(("parallel", "parallel"), VMEM_BIG),
        name="mla_prep",
    )(ab3, ab3, ab3, *tables, q_norm.reshape(1, -1), kv_norm.reshape(1, -1), wq, wkv,
      q_gain[:HEAD_DIM].reshape(1, HEAD_DIM), slot(q_gain),
      k_gain[:HEAD_DIM].reshape(1, HEAD_DIM), slot(k_gain))


def _softmax_sweep(qs, key_fn, val_fn, bias_fn, n_full, tq, diag_mask, tail, scratch):
    nh = len(qs)
    s_even, s_odd, m_ref, l_ref, acc_ref = scratch[:5]
    tail_ref = scratch[5] if tail is not None else None

    def scores(h, start, width):
        s = lax.dot_general(qs[h], key_fn(h, start, width), NT_DIMS, preferred_element_type=F32)
        return bias_fn(h, s, start, width)

    def lanes(x, width):
        return jnp.concatenate([x] * (width // PAD_TO), axis=1)

    def absorb(h, s, vals):
        m = m_ref[h]
        m_new = jnp.maximum(m, jnp.max(s, axis=-1, keepdims=True))
        alpha = jnp.exp2(m - m_new)
        p = jnp.exp2(s - lanes(m_new, s.shape[1]))
        p_sum = p[:, :PAD_TO]
        for c in range(1, s.shape[1] // PAD_TO):
            p_sum = p_sum + p[:, c * PAD_TO:(c + 1) * PAD_TO]
        l_ref[h] = alpha * l_ref[h] + p_sum
        acc_ref[h] = alpha * acc_ref[h] + jnp.dot(p.astype(BF16), vals, preferred_element_type=F32)
        m_ref[h] = m_new

    def half_step(c, cur, nxt, diagonal):
        start = pl.multiple_of(c * tq, tq)
        for h in range(nh):
            if not diagonal:
                nxt[h] = scores(h, start + tq, tq)
                absorb(h, cur[h], val_fn(h, start, tq))
                continue
            s = jnp.where(diag_mask(), cur[h], NEG_BIG)
            vals = val_fn(h, start, tq)
            if tail is not None:
                t_start, t_width, t_mask = tail
                s_tail = jnp.where(t_mask(), tail_ref[h], NEG_BIG)
                s = jnp.concatenate([s, s_tail], axis=1)
                vals = jnp.concatenate([vals, val_fn(h, t_start, t_width)], axis=0)
            absorb(h, s, vals)

    def by_parity(c, diagonal):
        return lambda: lax.cond(c % 2 == 0, lambda: half_step(c, s_even, s_odd, diagonal),
                                lambda: half_step(c, s_odd, s_even, diagonal))

    def step(c, carry):
        lax.cond(c == n_full, by_parity(c, True), by_parity(c, False))
        return carry

    for h in range(nh):
        m_ref[h] = jnp.full((tq, PAD_TO), NEG_BIG, F32)
        l_ref[h] = jnp.zeros((tq, PAD_TO), F32)
        acc_ref[h] = jnp.zeros((tq, HEAD_DIM), F32)
        s_even[h] = scores(h, 0, tq)
        if tail is not None:
            tail_ref[h] = scores(h, tail[0], tail[1])
    lax.fori_loop(0, n_full + 1, step, 0)
    return [acc_ref[h] / jnp.sum(l_ref[h], axis=-1, keepdims=True) for h in range(nh)]


def _sweep_scratch(tq, tail_width=0):
    bufs = [pltpu.VMEM((ATT_HEADS, tq, tq), F32), pltpu.VMEM((ATT_HEADS, tq, tq), F32),
            pltpu.VMEM((ATT_HEADS, tq, PAD_TO), F32), pltpu.VMEM((ATT_HEADS, tq, PAD_TO), F32),
            pltpu.VMEM((ATT_HEADS, tq, HEAD_DIM), F32)]
    if tail_width:
        bufs.append(pltpu.VMEM((ATT_HEADS, tq, tail_width), F32))
    return bufs


def _mla_attn_kernel(q_ref, k_ref, v_ref, o_ref, *scratch):
    tq, lk = q_ref.shape[0], k_ref.shape[0]
    width = 2 * HEAD_DIM
    i = pl.program_id(2)
    q0 = i * tq
    shift = CHUNK - N_META

    def chunk_mask(k_start, k_width):
        bits = CHUNK.bit_length() - 1
        qc = lax.shift_right_logical(
            q0 + shift + lax.broadcasted_iota(jnp.int32, (tq, k_width), 0), bits)
        kc = lax.shift_right_logical(
            k_start + shift + lax.broadcasted_iota(jnp.int32, (tq, k_width), 1), bits)
        return kc <= qc

    t_start = pl.multiple_of(jnp.minimum(q0 + tq, lk - PAD_TO), PAD_TO)

    def t_mask():
        t_kpos = t_start + lax.broadcasted_iota(jnp.int32, (tq, PAD_TO), 1)
        return (t_kpos >= q0 + tq) & chunk_mask(t_start, PAD_TO)

    qs = [q_ref[:, h * width:(h + 1) * width] for h in range(ATT_HEADS)]
    outs = _softmax_sweep(
        qs,
        lambda h, s0, w: k_ref[pl.ds(s0, w), h * width:(h + 1) * width],
        lambda h, s0, w: v_ref[pl.ds(s0, w), h * HEAD_DIM:(h + 1) * HEAD_DIM],
        lambda h, s, s0, w: s,
        i, tq, lambda: chunk_mask(q0, tq), (t_start, PAD_TO, t_mask), scratch)
    for h in range(ATT_HEADS):
        o_ref[:, h * HEAD_DIM:(h + 1) * HEAD_DIM] = outs[h].astype(o_ref.dtype)


def _mla_attention(q, k, v):
    b, lp, _ = q.shape
    tq = SEQ_TILE
    gw = ATT_HEADS * 2 * HEAD_DIM
    return pl.pallas_call(
        _mla_attn_kernel,
        grid=(b, HEADS // ATT_HEADS, lp // tq),
        in_specs=[pl.BlockSpec((None, tq, gw), lambda bi, g, i: (bi, i, g)),
                  pl.BlockSpec((None, lp, gw), lambda bi, g, i: (bi, 0, g)),
                  pl.BlockSpec((None, lp, ATT_HEADS * HEAD_DIM), lambda bi, g, i: (bi, 0, g))],
        out_specs=pl.BlockSpec((None, tq, ATT_HEADS * HEAD_DIM), lambda bi, g, i: (bi, i, g)),
        out_shape=jax.ShapeDtypeStruct((b, lp, HEADS * HEAD_DIM), BF16),
        scratch_shapes=_sweep_scratch(tq, PAD_TO),
        compiler_params=_params(("parallel", "parallel", "arbitrary"), VMEM_BIG),
        name="mla_attention",
    )(q, k, v)


def _fox_attn_kernel(q_ref, k_ref, v_ref, f_ref, qg_ref, kg_ref, o_ref, kn_ref, va_ref, fb_ref,
                     *scratch, q_scale):
    tq, lk = q_ref.shape[0], k_ref.shape[0]
    i = pl.program_id(2)
    q0 = pl.multiple_of(i * tq, tq)
    heads = [slice(h * HEAD_DIM, (h + 1) * HEAD_DIM) for h in range(ATT_HEADS)]

    @pl.when(i == 0)
    def _():
        def body(c, carry):
            rows = pl.ds(pl.multiple_of(c * tq, tq), tq)
            for sl in heads:
                kn_ref[rows, sl] = _rms(k_ref[rows, sl], kg_ref[...]).astype(BF16)
            va_ref[rows, :] = v_ref[rows, :].astype(BF16)
            return carry
        lax.fori_loop(0, lk // tq, body, 0)
        fb_ref[...] = f_ref[...] * LOG2E

    qs = [(_rms(q_ref[:, sl], qg_ref[...]) * q_scale).astype(BF16) for sl in heads]
    f0 = [jnp.max(fb_ref[h, :, pl.ds(q0, tq)], axis=-1, keepdims=True) for h in range(ATT_HEADS)]

    def causal():
        return (lax.broadcasted_iota(jnp.int32, (tq, tq), 1)
                <= lax.broadcasted_iota(jnp.int32, (tq, tq), 0))

    outs = _softmax_sweep(
        qs,
        lambda h, s0, w: kn_ref[pl.ds(s0, w), heads[h]],
        lambda h, s0, w: va_ref[pl.ds(s0, w), heads[h]],
        lambda h, s, s0, w: s - (fb_ref[h, :, pl.ds(s0, w)] - f0[h]),
        i, tq, causal, None, scratch)
    for h, sl in enumerate(heads):
        o_ref[:, sl] = outs[h].astype(o_ref.dtype)


def _fox_attention(cd3, f, q_gain, k_gain, q_scale):
    b, lp, _ = cd3.shape
    tq = SEQ_TILE
    gw = ATT_HEADS * HEAD_DIM
    groups = HEADS // ATT_HEADS
    kern = functools.partial(_fox_attn_kernel, q_scale=q_scale)
    return pl.pallas_call(
        kern,
        grid=(b, groups, lp // tq),
        in_specs=[pl.BlockSpec((None, tq, gw), lambda bi, g, i: (bi, i, g)),
                  pl.BlockSpec((None, lp, gw), lambda bi, g, i: (bi, 0, groups + g)),
                  pl.BlockSpec((None, lp, gw), lambda bi, g, i: (bi, 0, 2 * groups + g)),
                  pl.BlockSpec((None, ATT_HEADS, 1, lp), lambda bi, g, i: (bi, g, 0, 0)),
                  _const_spec((1, HEAD_DIM)), _const_spec((1, HEAD_DIM))],
        out_specs=pl.BlockSpec((None, tq, gw), lambda bi, g, i: (bi, i, g)),
        out_shape=jax.ShapeDtypeStruct((b, lp, HEADS * HEAD_DIM), BF16),
        scratch_shapes=[pltpu.VMEM((lp, gw), BF16),
                        pltpu.VMEM((lp, gw), BF16),
                        pltpu.VMEM((ATT_HEADS, 1, lp), F32)] + _sweep_scratch(tq),
        compiler_params=_params(("parallel", "parallel", "arbitrary"), VMEM_BIG),
        name="fox_attention",
    )(cd3, cd3, cd3, f, q_gain.reshape(1, HEAD_DIM), k_gain.reshape(1, HEAD_DIM))


def _forget_cumsum_kernel(x_ref, b_ref, o_ref):
    rows, lp = x_ref.shape
    lane = lax.broadcasted_iota(jnp.int32, (rows, PAD_TO), 1)
    carry = jnp.zeros((rows, 1), F32)
    for c in range(lp // PAD_TO):
        sl = slice(c * PAD_TO, (c + 1) * PAD_TO)
        x = _log_sigmoid(x_ref[:, sl] + b_ref[...])
        step = 1
        while step < PAD_TO:
            x = x + jnp.where(lane >= step, pltpu.roll(x, step, axis=1), 0.0)
            step *= 2
        x = x + carry
        o_ref[:, sl] = x
        carry = x[:, PAD_TO - 1:PAD_TO]


def _forget_cumsum(ff_t, bias_col):
    rows, lp = ff_t.shape
    return pl.pallas_call(
        _forget_cumsum_kernel,
        out_shape=jax.ShapeDtypeStruct((rows, lp), F32),
        name="fox_forget_cumsum",
    )(ff_t, bias_col)


def _cumsum_rows(x):
    row = lax.broadcasted_iota(jnp.int32, x.shape, 0)
    step = 1
    while step < x.shape[0]:
        x = x + jnp.where(row >= step, pltpu.roll(x, step, axis=0), 0.0)
        step *= 2
    return x


def _gla_kernel(q_ref, k_ref, v_ref, r_ref, a_ref, wa_ref, ba_ref, on_ref, o_ref, st_ref):
    tm = q_ref.shape[0]

    @pl.when(pl.program_id(1) == 0)
    def _():
        st_ref[...] = jnp.zeros_like(st_ref)

    causal = (lax.broadcasted_iota(jnp.int32, (CHUNK, CHUNK), 1)
              <= lax.broadcasted_iota(jnp.int32, (CHUNK, CHUNK), 0))

    def body(c, carry):
        r0 = pl.multiple_of(c * CHUNK, CHUNK)
        rows = pl.ds(r0, CHUNK)
        gate = jnp.dot(a_ref[rows, :].astype(BF16), wa_ref[...], preferred_element_type=F32)
        for h in range(GLA_HEADS):
            ks = slice(h * GLA_DK, (h + 1) * GLA_DK)
            vs = slice(h * GLA_DV, (h + 1) * GLA_DV)
            g = _log_sigmoid(gate[:, ks] + ba_ref[:, ks]) / GLA_TAU
            bcum = _cumsum_rows(g)
            b_last = bcum[CHUNK - 1:CHUNK, :]
            k = k_ref[rows, ks]
            v = v_ref[rows, vs].astype(BF16)
            q_dec = ((q_ref[rows, ks] * (GLA_DK ** -0.5)) * jnp.exp(bcum)).astype(BF16)
            k_inv = (k * jnp.exp(-bcum)).astype(BF16)
            k_end = (k * jnp.exp(b_last - bcum)).astype(BF16)
            a = lax.dot_general(q_dec, k_inv, NT_DIMS, preferred_element_type=F32)
            a = jnp.where(causal, a, 0.0).astype(BF16)
            st = st_ref[h]
            o = (jnp.dot(a, v, preferred_element_type=F32)
                 + lax.dot_general(q_dec, st.astype(BF16), NT_DIMS, preferred_element_type=F32))
            st_ref[h] = st * jnp.exp(b_last) + lax.dot_general(v, k_end, TN_DIMS,
                                                               preferred_element_type=F32)
            o = _rms(o, on_ref[...])
            o_ref[rows, vs] = (o * _silu(r_ref[rows, vs])).astype(o_ref.dtype)
        return carry

    lax.fori_loop(0, tm // CHUNK, body, 0)


def _gla(cd3, wa, layer, ba, o_norm):
    b, lp, _ = cd3.shape
    tm = SEQ_TILE
    kw, vw = GLA_HEADS * GLA_DK, GLA_HEADS * GLA_DV
    return pl.pallas_call(
        _gla_kernel,
        grid=(b, lp // tm),
        in_specs=[pl.BlockSpec((None, tm, kw), lambda bi, i: (bi, i, 6144 // kw)),
                  pl.BlockSpec((None, tm, kw), lambda bi, i: (bi, i, 7168 // kw)),
                  pl.BlockSpec((None, tm, vw), lambda bi, i: (bi, i, 8192 // vw)),
                  pl.BlockSpec((None, tm, vw), lambda bi, i: (bi, i, 10240 // vw)),
                  pl.BlockSpec((None, tm, PAD_TO), lambda bi, i: (bi, i, 12288 // PAD_TO)),
                  _const_spec((PAD_TO, kw), layer), _const_spec((1, kw)), _const_spec((1, GLA_DV))],
        out_specs=pl.BlockSpec((None, tm, vw), lambda bi, i: (bi, i, 0)),
        out_shape=jax.ShapeDtypeStruct((b, lp, vw), BF16),
        scratch_shapes=[pltpu.VMEM((GLA_HEADS, GLA_DV, GLA_DK), F32)],
        compiler_params=_params(("parallel", "arbitrary"), VMEM_MID),
        name="gla",
    )(cd3, cd3, cd3, cd3, cd3, wa, ba.reshape(1, -1), o_norm.reshape(1, -1))


def _ab_in_weight(w):
    wt = jnp.swapaxes(w, 1, 2)
    zeros = jnp.zeros((w.shape[0], 3840 - w.shape[2], w.shape[1]), w.dtype)
    return jnp.concatenate([wt, zeros], axis=1).astype(BF16)


def _cd_in_weight(w):
    wt = jnp.swapaxes(w, 1, 2)
    pieces = [wt[:, :6144], wt[:, 6160:10256], wt[:, 10272:], wt[:, 6144:6160],
              wt[:, 10256:10272], jnp.zeros((w.shape[0], 12800 - w.shape[2], w.shape[1]), w.dtype)]
    return jnp.concatenate(pieces, axis=1).astype(BF16)


def _mla_q_weight(w):
    n, r, _ = w.shape
    w = jnp.pad(w.astype(BF16).reshape(n, r, HEADS, MLA_QK),
                ((0, 0), (0, 0), (0, 0), (0, 2 * HEAD_DIM - MLA_QK)))
    return w.reshape(n, r, HEADS * 2 * HEAD_DIM)


def _gla_gate_weight(w):
    return jnp.pad(w.astype(BF16), ((0, 0), (GLA_RANK, PAD_TO - 2 * GLA_RANK), (0, 0)))


def _rotary_tables(lp):
    pos = jnp.arange(lp, dtype=F32)
    inv_freq = ROPE_BASE ** (-jnp.arange(0, MLA_ROPE, 2, dtype=F32) / MLA_ROPE)
    ang = pos[:, None] * inv_freq[None, :]
    cos, sin = jnp.cos(ang), jnp.sin(ang)
    z32, z64 = jnp.zeros_like(cos), jnp.zeros((lp, HEAD_DIM - MLA_ROPE), F32)
    return (jnp.concatenate([cos, cos, z64], axis=1),
            jnp.concatenate([-sin, z32, z64], axis=1),
            jnp.concatenate([z32, sin, z64], axis=1))


def _ffn(h, norm, w_gate, w_up, w_down, layer, name):
    hidden = _norm_matmul(h, norm, [w_gate, w_up], layer, _swiglu_epilogue, BF16, ROW_TILE, 512,
                          name + "_up")
    return _matmul_residual([hidden], w_down, layer, h, 0.5, RES_ROWS, RES_COLS, name + "_down")


def _pool_mla_layer(h, b, lp, norm, w_in, pool_w, pool_scale, q_norm, wq, kv_norm, wkv,
                    q_gain, k_gain, w_out, layer, tables):
    ab = _norm_matmul(h, norm, [w_in], layer, _identity, F32, ROW_TILE, 768, "ab_in",
                      w_transposed=True)
    y_pool = _pool_mixer(ab, pool_w, layer, pool_scale, lp)
    q, k, v = _mla_prep(ab.reshape(b, lp, -1), tables, q_norm, kv_norm, wq, wkv, layer,
                        q_gain, k_gain, MLA_QK ** -0.5 * LOG2E)
    y_mla = _mla_attention(q, k, v)
    return _matmul_residual([y_pool, y_mla.reshape(b * lp, -1)], w_out, layer, h, 1.0,
                            RES_ROWS, OUT_COLS, "ab_out")


def _fox_gla_layer(h, b, lp, norm, w_in, fox_q_gain, fox_k_gain, fox_f_bias, wa, gla_b_a,
                   gla_o_norm, w_out, layer):
    cd = _norm_matmul(h, norm, [w_in], layer, _identity, F32, ROW_TILE, 512, "cd_in",
                      w_transposed=True)
    cd3 = cd.reshape(b, lp, -1)
    ff_t = cd[:, 12288:12288 + HEADS].reshape(b, lp, HEADS).transpose(0, 2, 1).reshape(b * HEADS, lp)
    f = _forget_cumsum(ff_t, jnp.tile(fox_f_bias, b).reshape(b * HEADS, 1))
    y_fox = _fox_attention(cd3, f.reshape(b, HEADS, 1, lp), fox_q_gain, fox_k_gain,
                           HEAD_DIM ** -0.5 * LOG2E)
    y_gla = _gla(cd3, wa, layer, gla_b_a, gla_o_norm)
    return _matmul_residual([y_fox.reshape(b * lp, -1), y_gla.reshape(b * lp, -1)], w_out, layer,
                            h, 1.0, RES_ROWS, OUT_COLS, "cd_out")


def kernel(x, meta_tokens, ffn1_norm, ffn1_w_gate, ffn1_w_up, ffn1_w_down, mix_norm, ffn2_norm, ffn2_w_gate, ffn2_w_up, ffn2_w_down, ab_w_in, pool_w, pool_scale, mla_q_norm, mla_w_q_up, mla_kv_norm, mla_w_kv_up, mla_q_gain, mla_k_gain, ab_w_out, cd_w_in, fox_q_gain, fox_k_gain, fox_f_bias, gla_w_a2, gla_b_a, gla_o_norm, cd_w_out):
    b, s, d = x.shape
    length = N_META + s
    lp = -(-length // PAD_TO) * PAD_TO
    meta = jnp.broadcast_to(meta_tokens.astype(x.dtype)[None], (b, N_META, d))
    h = jnp.concatenate([meta, x, jnp.zeros((b, lp - length, d), x.dtype)], axis=1)
    h = h.reshape(b * lp, d)
    tables = _rotary_tables(lp)

    ffn1 = [w.astype(BF16) for w in (ffn1_w_gate, ffn1_w_up, ffn1_w_down)]
    ffn2 = [w.astype(BF16) for w in (ffn2_w_gate, ffn2_w_up, ffn2_w_down)]
    ab_in, cd_in = _ab_in_weight(ab_w_in), _cd_in_weight(cd_w_in)
    ab_out, cd_out = ab_w_out.astype(BF16), cd_w_out.astype(BF16)
    pool_wb, wq, wkv = pool_w.astype(BF16), _mla_q_weight(mla_w_q_up), mla_w_kv_up.astype(BF16)
    wa = _gla_gate_weight(gla_w_a2)

    for layer in range(ffn1_norm.shape[0]):
        i = layer // 2
        h = _ffn(h, ffn1_norm[layer], *ffn1, layer, "ffn1")
        if layer % 2 == 0:
            h = _pool_mla_layer(h, b, lp, mix_norm[layer], ab_in, pool_wb, pool_scale[i],
                                mla_q_norm[i], wq, mla_kv_norm[i], wkv, mla_q_gain[i],
                                mla_k_gain[i], ab_out, i, tables)
        else:
            h = _fox_gla_layer(h, b, lp, mix_norm[layer], cd_in, fox_q_gain[i], fox_k_gain[i],
                               fox_f_bias[i], wa, gla_b_a[i], gla_o_norm[i], cd_out, i)
        h = _ffn(h, ffn2_norm[layer], *ffn2, layer, "ffn2")
    return h.reshape(b, lp, d)[:, N_META:N_META + s]
```

```python
import functools
import math

import jax
import jax.numpy as jnp
from jax import lax
from jax.experimental import pallas as pl
from jax.experimental.pallas import tpu as pltpu

F32 = jnp.float32
BF16 = jnp.bfloat16

N_META = 16
PAD_TO = 128
RMS_EPS = 1e-6
ROPE_BASE = 10000.0
CHUNK = 64

POOL_WINDOWS = (2, 4, 8, 16)
POOL_GROUP = 512
HEADS = 16
HEAD_DIM = 128
MLA_ROPE = 64
MLA_QK = 192
GLA_HEADS = 4
GLA_DK = 256
GLA_DV = 512
GLA_TAU = 16.0
GLA_RANK = 16

VMEM_BIG = 58 * 1024 * 1024
VMEM_MID = 40 * 1024 * 1024

ROW_TILE = 768
RES_ROWS = 1056
RES_COLS = 512
OUT_COLS = 1024
SEQ_TILE = 384
ATT_HEADS = 4
NEG_BIG = -1e30
LOG2E = math.log2(math.e)

NT_DIMS = (((1,), (1,)), ((), ()))
TN_DIMS = (((0,), (0,)), ((), ()))


def _silu(x):
    return x / (1.0 + jnp.exp(-x))


def _log_sigmoid(x):
    return jnp.minimum(x, 0.0) - jnp.log(1.0 + jnp.exp(-jnp.abs(x)))


def _rms(x, gain, n=None):
    n = x.shape[-1] if n is None else n
    ss = jnp.sum(x * x, axis=-1, keepdims=True)
    return x * lax.rsqrt(ss / n + RMS_EPS) * gain


def _params(sem, vmem):
    return pltpu.CompilerParams(dimension_semantics=sem, vmem_limit_bytes=vmem)


def _const_spec(shape, layer=None):
    if layer is None:
        nd = len(shape)
        return pl.BlockSpec(shape, lambda *_: (0,) * nd, pipeline_mode=pl.Buffered(1))
    nd = len(shape)
    return pl.BlockSpec((None,) + tuple(shape), lambda *_: (layer,) + (0,) * nd,
                        pipeline_mode=pl.Buffered(1))


def _norm_mm_kernel(a_ref, g_ref, *rest, n_w, epilogue, rows, w_transposed, has_side):
    if has_side:
        w_side_ref, rest = rest[0], rest[1:]
    w_refs, o_ref = rest[:n_w], rest[n_w]
    side_ref = rest[n_w + 1] if has_side else None
    xn_ref = rest[-1]
    dims = NT_DIMS if w_transposed else (((1,), (0,)), ((), ()))

    @pl.when(pl.program_id(1) == 0)
    def _():
        def body(c, carry):
            r0 = pl.multiple_of(c * rows, rows)
            a = a_ref[pl.ds(r0, rows), :]
            xn_ref[pl.ds(r0, rows), :] = _rms(a, g_ref[...]).astype(BF16)
            return carry
        lax.fori_loop(0, a_ref.shape[0] // rows, body, 0)
        if has_side:
            side_ref[...] = lax.dot_general(xn_ref[...], w_side_ref[...], NT_DIMS,
                                            preferred_element_type=F32)

    xn = xn_ref[...]
    outs = [lax.dot_general(xn, w[...], dims, preferred_element_type=F32) for w in w_refs]
    o_ref[...] = epilogue(*outs).astype(o_ref.dtype)


def _norm_matmul(a, gain, ws, layer, epilogue, out_dtype, tm, tn, name, w_transposed=False,
                 w_side=None):
    t, k = a.shape
    n = ws[0].shape[1 if w_transposed else 2]
    has_side = w_side is not None
    kern = functools.partial(_norm_mm_kernel, n_w=len(ws), epilogue=epilogue, rows=64,
                             w_transposed=w_transposed, has_side=has_side)
    if w_transposed:
        w_spec = pl.BlockSpec((None, tn, k), lambda i, j: (layer, j, 0))
    else:
        w_spec = pl.BlockSpec((None, k, tn), lambda i, j: (layer, 0, j))
    main_spec = pl.BlockSpec((tm, tn), lambda i, j: (i, j))
    main_shape = jax.ShapeDtypeStruct((t, n), out_dtype)
    side_in, side_args, out_specs, out_shape = [], [], main_spec, main_shape
    if has_side:
        n_side = w_side.shape[1]
        side_in, side_args = [_const_spec((n_side, k), layer)], [w_side]
        out_specs = [main_spec, pl.BlockSpec((tm, n_side), lambda i, j: (i, 0))]
        out_shape = [main_shape, jax.ShapeDtypeStruct((t, n_side), F32)]
    return pl.pallas_call(
        kern,
        grid=(t // tm, n // tn),
        in_specs=[pl.BlockSpec((tm, k), lambda i, j: (i, 0)),
                  pl.BlockSpec((1, k), lambda i, j: (0, 0))] + side_in + [w_spec for _ in ws],
        out_specs=out_specs,
        out_shape=out_shape,
        scratch_shapes=[pltpu.VMEM((tm, k), BF16)],
        compiler_params=_params(("parallel", "arbitrary"), VMEM_BIG),
        name=name,
    )(a, gain.reshape(1, k), *side_args, *ws)


def _swiglu_epilogue(g, u):
    return _silu(g) * u


def _identity(x):
    return x


def _mm_res_kernel(*refs, k_sizes, scale):
    n_a = len(k_sizes)
    a_refs, w_ref, r_ref, o_ref = refs[:n_a], refs[n_a], refs[n_a + 1], refs[n_a + 2]
    acc, off = None, 0
    for a_ref, ks in zip(a_refs, k_sizes):
        p = jnp.dot(a_ref[...], w_ref[off:off + ks, :], preferred_element_type=F32)
        acc = p if acc is None else acc + p
        off += ks
    if scale != 1.0:
        acc = scale * acc
    o_ref[...] = r_ref[...] + acc


def _matmul_residual(a_list, w, layer, res, scale, tm, tn, name):
    t, n = res.shape
    k_sizes = tuple(a.shape[1] for a in a_list)
    k = sum(k_sizes)
    kern = functools.partial(_mm_res_kernel, k_sizes=k_sizes, scale=scale)
    n_a = len(a_list)
    return pl.pallas_call(
        kern,
        grid=(t // tm, n // tn),
        in_specs=[pl.BlockSpec((tm, ks), lambda i, j: (i, 0)) for ks in k_sizes]
                 + [pl.BlockSpec((None, k, tn), lambda i, j: (layer, 0, j)),
                    pl.BlockSpec((tm, tn), lambda i, j: (i, j))],
        out_specs=pl.BlockSpec((tm, tn), lambda i, j: (i, j)),
        out_shape=jax.ShapeDtypeStruct((t, n), F32),
        input_output_aliases={n_a + 1: 0},
        compiler_params=_params(("parallel", "arbitrary"), VMEM_BIG),
        name=name,
    )(*a_list, w, res)


def _pool_kernel(x_ref, halo_ref, w_ref, s_ref, o_ref, xs_ref, *, tiles_per_seq):
    tm = x_ref.shape[0]
    halo = POOL_WINDOWS[-1]
    it = pl.program_id(0) % tiles_per_seq
    keep = (it > 0).astype(F32)
    xs_ref[0:halo, :] = halo_ref[...] * keep
    xs_ref[halo:halo + tm, :] = x_ref[...]
    pos = it * tm + lax.broadcasted_iota(jnp.int32, (tm, 1), 0)
    for g, win in enumerate(POOL_WINDOWS):
        c0 = g * POOL_GROUP
        x = xs_ref[halo:halo + tm, c0:c0 + POOL_GROUP]
        acc = x
        for j in range(1, win):
            acc = acc + xs_ref[halo - j:halo - j + tm, c0:c0 + POOL_GROUP]
        cnt = jnp.minimum(pos + 1, win).astype(F32)
        d = acc / cnt - x
        y = jnp.dot(d.astype(BF16), w_ref[g], preferred_element_type=F32)
        o_ref[:, c0:c0 + POOL_GROUP] = (y * s_ref[:, c0:c0 + POOL_GROUP]).astype(o_ref.dtype)


def _pool_mixer(ab, pool_w, layer, pool_scale, lp):
    t = ab.shape[0]
    tm, halo = SEQ_TILE, POOL_WINDOWS[-1]
    width = len(POOL_WINDOWS) * POOL_GROUP
    per_halo = tm // halo
    kern = functools.partial(_pool_kernel, tiles_per_seq=lp // tm)
    return pl.pallas_call(
        kern,
        grid=(t // tm,),
        in_specs=[pl.BlockSpec((tm, width), lambda i: (i, 0)),
                  pl.BlockSpec((halo, width), lambda i: (jnp.maximum(i * per_halo - 1, 0), 0)),
                  _const_spec((len(POOL_WINDOWS), POOL_GROUP, POOL_GROUP), layer),
                  _const_spec((1, width))],
        out_specs=pl.BlockSpec((tm, width), lambda i: (i, 0)),
        out_shape=jax.ShapeDtypeStruct((t, width), BF16),
        scratch_shapes=[pltpu.VMEM((tm + halo, width), F32)],
        compiler_params=_params(("parallel",), VMEM_MID),
        name="pool_mixer",
    )(ab, ab, pool_w, pool_scale.reshape(1, width))


def _rotary_slot(x, c_ref, s1_ref, s2_ref):
    return (x * c_ref[...] + pltpu.roll(x, 96, axis=1) * s1_ref[...]
            + pltpu.roll(x, 32, axis=1) * s2_ref[...])


def _mla_prep_kernel(cq_ref, ckv_ref, kpe_ref, c_ref, s1_ref, s2_ref, qn_ref, kvn_ref,
                     wq_ref, wkv_ref, qgn_ref, qgr_ref, kgn_ref, kgr_ref,
                     q_ref, k_ref, v_ref, *, q_scale):
    cq = _rms(cq_ref[...], qn_ref[...]).astype(BF16)
    ckv = _rms(ckv_ref[...], kvn_ref[...]).astype(BF16)
    kr = _rotary_slot(_rms(kpe_ref[...], kgr_ref[...], MLA_ROPE), c_ref, s1_ref, s2_ref).astype(BF16)
    for h in range(HEADS):
        lo = h * 2 * HEAD_DIM
        mid, hi = lo + HEAD_DIM, lo + 2 * HEAD_DIM
        q = jnp.dot(cq, wq_ref[:, lo:hi], preferred_element_type=F32)
        qn = _rms(q[:, :HEAD_DIM], qgn_ref[...])
        qr = _rotary_slot(_rms(q[:, HEAD_DIM:], qgr_ref[...], MLA_ROPE), c_ref, s1_ref, s2_ref)
        q_ref[:, lo:mid] = (qn * q_scale).astype(BF16)
        q_ref[:, mid:hi] = (qr * q_scale).astype(BF16)
        kv = jnp.dot(ckv, wkv_ref[:, lo:hi], preferred_element_type=F32)
        k_ref[:, lo:mid] = _rms(kv[:, :HEAD_DIM], kgn_ref[...]).astype(BF16)
        k_ref[:, mid:hi] = kr
        v_ref[:, h * HEAD_DIM:(h + 1) * HEAD_DIM] = kv[:, HEAD_DIM:].astype(BF16)


def _mla_prep(ab3, tables, q_norm, kv_norm, wq, wkv, layer, q_gain, k_gain, q_scale):
    b, lp, _ = ab3.shape
    tm = SEQ_TILE
    q_rank, kv_rank = wq.shape[1], wkv.shape[1]
    cat = HEADS * 2 * HEAD_DIM
    zeros = jnp.zeros((HEAD_DIM - MLA_ROPE,), F32)
    slot = lambda g: jnp.concatenate([g[HEAD_DIM:], zeros]).reshape(1, HEAD_DIM)
    row = lambda i_, j_: (j_, 0)
    out = jax.ShapeDtypeStruct((b, lp, cat), BF16)
    return pl.pallas_call(
        functools.partial(_mla_prep_kernel, q_scale=q_scale),
        grid=(b, lp // tm),
        in_specs=[pl.BlockSpec((None, tm, q_rank), lambda i, j: (i, j, 2048 // q_rank)),
                  pl.BlockSpec((None, tm, kv_rank), lambda i, j: (i, j, 3072 // kv_rank)),
                  pl.BlockSpec((None, tm, HEAD_DIM), lambda i, j: (i, j, 3584 // HEAD_DIM)),
                  pl.BlockSpec((tm, HEAD_DIM), row),
                  pl.BlockSpec((tm, HEAD_DIM), row),
                  pl.BlockSpec((tm, HEAD_DIM), row),
                  _const_spec((1, q_rank)), _const_spec((1, kv_rank)),
                  _const_spec((q_rank, cat), layer), _const_spec((kv_rank, cat), layer),
                  _const_spec((1, HEAD_DIM)), _const_spec((1, HEAD_DIM)),
                  _const_spec((1, HEAD_DIM)), _const_spec((1, HEAD_DIM))],
        out_specs=[pl.BlockSpec((None, tm, cat), lambda i, j: (i, j, 0))] * 2
                  + [pl.BlockSpec((None, tm, HEADS * HEAD_DIM), lambda i, j: (i, j, 0))],
        out_shape=[out, out, jax.ShapeDtypeStruct((b, lp, HEADS * HEAD_DIM), BF16)],
        compiler_params=_params(("parallel", "parallel"), VMEM_BIG),
        name="mla_prep",
    )(ab3, ab3, ab3, *tables, q_norm.reshape(1, -1), kv_norm.reshape(1, -1), wq, wkv,
      q_gain[:HEAD_DIM].reshape(1, HEAD_DIM), slot(q_gain),
      k_gain[:HEAD_DIM].reshape(1, HEAD_DIM), slot(k_gain))


def _softmax_sweep(qs, key_fn, val_fn, bias_fn, n_full, tq, diag_mask, tail, scratch):
    nh = len(qs)
    s_even, s_odd, m_ref, l_ref, acc_ref = scratch[:5]
    tail_ref = scratch[5] if tail is not None else None

    def scores(h, start, width):
        s = lax.dot_general(qs[h], key_fn(h, start, width), NT_DIMS, preferred_element_type=F32)
        return bias_fn(h, s, start, width)

    def lanes(x, width):
        return jnp.concatenate([x] * (width // PAD_TO), axis=1)

    def absorb(h, s, vals):
        m = m_ref[h]
        m_new = jnp.maximum(m, jnp.max(s, axis=-1, keepdims=True))
        alpha = jnp.exp2(m - m_new)
        p = jnp.exp2(s - lanes(m_new, s.shape[1]))
        p_sum = p[:, :PAD_TO]
        for c in range(1, s.shape[1] // PAD_TO):
            p_sum = p_sum + p[:, c * PAD_TO:(c + 1) * PAD_TO]
        l_ref[h] = alpha * l_ref[h] + p_sum
        acc_ref[h] = alpha * acc_ref[h] + jnp.dot(p.astype(BF16), vals, preferred_element_type=F32)
        m_ref[h] = m_new

    def half_step(c, cur, nxt, diagonal):
        start = pl.multiple_of(c * tq, tq)
        for h in range(nh):
            if not diagonal:
                nxt[h] = scores(h, start + tq, tq)
                absorb(h, cur[h], val_fn(h, start, tq))
                continue
            s = jnp.where(diag_mask(), cur[h], NEG_BIG)
            vals = val_fn(h, start, tq)
            if tail is not None:
                t_start, t_width, t_mask = tail
                s_tail = jnp.where(t_mask(), tail_ref[h], NEG_BIG)
                s = jnp.concatenate([s, s_tail], axis=1)
                vals = jnp.concatenate([vals, val_fn(h, t_start, t_width)], axis=0)
            absorb(h, s, vals)

    def by_parity(c, diagonal):
        return lambda: lax.cond(c % 2 == 0, lambda: half_step(c, s_even, s_odd, diagonal),
                                lambda: half_step(c, s_odd, s_even, diagonal))

    def step(c, carry):
        lax.cond(c == n_full, by_parity(c, True), by_parity(c, False))
        return carry

    for h in range(nh):
        m_ref[h] = jnp.full((tq, PAD_TO), NEG_BIG, F32)
        l_ref[h] = jnp.zeros((tq, PAD_TO), F32)
        acc_ref[h] = jnp.zeros((tq, HEAD_DIM), F32)
        s_even[h] = scores(h, 0, tq)
        if tail is not None:
            tail_ref[h] = scores(h, tail[0], tail[1])
    lax.fori_loop(0, n_full + 1, step, 0)
    return [acc_ref[h] / jnp.sum(l_ref[h], axis=-1, keepdims=True) for h in range(nh)]


def _sweep_scratch(tq, tail_width=0):
    bufs = [pltpu.VMEM((ATT_HEADS, tq, tq), F32), pltpu.VMEM((ATT_HEADS, tq, tq), F32),
            pltpu.VMEM((ATT_HEADS, tq, PAD_TO), F32), pltpu.VMEM((ATT_HEADS, tq, PAD_TO), F32),
            pltpu.VMEM((ATT_HEADS, tq, HEAD_DIM), F32)]
    if tail_width:
        bufs.append(pltpu.VMEM((ATT_HEADS, tq, tail_width), F32))
    return bufs


def _mla_attn_kernel(q_ref, k_ref, v_ref, o_ref, *scratch):
    tq, lk = q_ref.shape[0], k_ref.shape[0]
    width = 2 * HEAD_DIM
    i = pl.program_id(2)
    q0 = i * tq
    shift = CHUNK - N_META

    def chunk_mask(k_start, k_width):
        bits = CHUNK.bit_length() - 1
        qc = lax.shift_right_logical(
            q0 + shift + lax.broadcasted_iota(jnp.int32, (tq, k_width), 0), bits)
        kc = lax.shift_right_logical(
            k_start + shift + lax.broadcasted_iota(jnp.int32, (tq, k_width), 1), bits)
        return kc <= qc

    t_start = pl.multiple_of(jnp.minimum(q0 + tq, lk - PAD_TO), PAD_TO)

    def t_mask():
        t_kpos = t_start + lax.broadcasted_iota(jnp.int32, (tq, PAD_TO), 1)
        return (t_kpos >= q0 + tq) & chunk_mask(t_start, PAD_TO)

    qs = [q_ref[:, h * width:(h + 1) * width] for h in range(ATT_HEADS)]
    outs = _softmax_sweep(
        qs,
        lambda h, s0, w: k_ref[pl.ds(s0, w), h * width:(h + 1) * width],
        lambda h, s0, w: v_ref[pl.ds(s0, w), h * HEAD_DIM:(h + 1) * HEAD_DIM],
        lambda h, s, s0, w: s,
        i, tq, lambda: chunk_mask(q0, tq), (t_start, PAD_TO, t_mask), scratch)
    for h in range(ATT_HEADS):
        o_ref[:, h * HEAD_DIM:(h + 1) * HEAD_DIM] = outs[h].astype(o_ref.dtype)


def _mla_attention(q, k, v):
    b, lp, _ = q.shape
    tq = SEQ_TILE
    gw = ATT_HEADS * 2 * HEAD_DIM
    return pl.pallas_call(
        _mla_attn_kernel,
        grid=(b, HEADS // ATT_HEADS, lp // tq),
        in_specs=[pl.BlockSpec((None, tq, gw), lambda bi, g, i: (bi, i, g)),
                  pl.BlockSpec((None, lp, gw), lambda bi, g, i: (bi, 0, g)),
                  pl.BlockSpec((None, lp, ATT_HEADS * HEAD_DIM), lambda bi, g, i: (bi, 0, g))],
        out_specs=pl.BlockSpec((None, tq, ATT_HEADS * HEAD_DIM), lambda bi, g, i: (bi, i, g)),
        out_shape=jax.ShapeDtypeStruct((b, lp, HEADS * HEAD_DIM), BF16),
        scratch_shapes=_sweep_scratch(tq, PAD_TO),
        compiler_params=_params(("parallel", "parallel", "arbitrary"), VMEM_BIG),
        name="mla_attention",
    )(q, k, v)


def _fox_attn_kernel(q_ref, k_ref, v_ref, f_ref, qg_ref, kg_ref, o_ref, kn_ref, va_ref, fb_ref,
                     *scratch, q_scale):
    tq, lk = q_ref.shape[0], k_ref.shape[0]
    i = pl.program_id(2)
    q0 = pl.multiple_of(i * tq, tq)
    heads = [slice(h * HEAD_DIM, (h + 1) * HEAD_DIM) for h in range(ATT_HEADS)]

    @pl.when(i == 0)
    def _():
        def body(c, carry):
            rows = pl.ds(pl.multiple_of(c * tq, tq), tq)
            for sl in heads:
                kn_ref[rows, sl] = _rms(k_ref[rows, sl], kg_ref[...]).astype(BF16)
            va_ref[rows, :] = v_ref[rows, :].astype(BF16)
            return carry
        lax.fori_loop(0, lk // tq, body, 0)
        fb_ref[...] = f_ref[...] * LOG2E

    qs = [(_rms(q_ref[:, sl], qg_ref[...]) * q_scale).astype(BF16) for sl in heads]
    f0 = [jnp.max(fb_ref[h, :, pl.ds(q0, tq)], axis=-1, keepdims=True) for h in range(ATT_HEADS)]

    def causal():
        return (lax.broadcasted_iota(jnp.int32, (tq, tq), 1)
                <= lax.broadcasted_iota(jnp.int32, (tq, tq), 0))

    outs = _softmax_sweep(
        qs,
        lambda h, s0, w: kn_ref[pl.ds(s0, w), heads[h]],
        lambda h, s0, w: va_ref[pl.ds(s0, w), heads[h]],
        lambda h, s, s0, w: s - (fb_ref[h, :, pl.ds(s0, w)] - f0[h]),
        i, tq, causal, None, scratch)
    for h, sl in enumerate(heads):
        o_ref[:, sl] = outs[h].astype(o_ref.dtype)


def _fox_attention(cd3, f, q_gain, k_gain, q_scale):
    b, lp, _ = cd3.shape
    tq = SEQ_TILE
    gw = ATT_HEADS * HEAD_DIM
    groups = HEADS // ATT_HEADS
    kern = functools.partial(_fox_attn_kernel, q_scale=q_scale)
    return pl.pallas_call(
        kern,
        grid=(b, groups, lp // tq),
        in_specs=[pl.BlockSpec((None, tq, gw), lambda bi, g, i: (bi, i, g)),
                  pl.BlockSpec((None, lp, gw), lambda bi, g, i: (bi, 0, groups + g)),
                  pl.BlockSpec((None, lp, gw), lambda bi, g, i: (bi, 0, 2 * groups + g)),
                  pl.BlockSpec((None, ATT_HEADS, 1, lp), lambda bi, g, i: (bi, g, 0, 0)),
                  _const_spec((1, HEAD_DIM)), _const_spec((1, HEAD_DIM))],
        out_specs=pl.BlockSpec((None, tq, gw), lambda bi, g, i: (bi, i, g)),
        out_shape=jax.ShapeDtypeStruct((b, lp, HEADS * HEAD_DIM), BF16),
        scratch_shapes=[pltpu.VMEM((lp, gw), BF16),
                        pltpu.VMEM((lp, gw), BF16),
                        pltpu.VMEM((ATT_HEADS, 1, lp), F32)] + _sweep_scratch(tq),
        compiler_params=_params(("parallel", "parallel", "arbitrary"), VMEM_BIG),
        name="fox_attention",
    )(cd3, cd3, cd3, f, q_gain.reshape(1, HEAD_DIM), k_gain.reshape(1, HEAD_DIM))


def _forget_cumsum_kernel(x_ref, b_ref, o_ref):
    rows, lp = x_ref.shape
    lane = lax.broadcasted_iota(jnp.int32, (rows, PAD_TO), 1)
    carry = jnp.zeros((rows, 1), F32)
    for c in range(lp // PAD_TO):
        sl = slice(c * PAD_TO, (c + 1) * PAD_TO)
        x = _log_sigmoid(x_ref[:, sl] + b_ref[...])
        step = 1
        while step < PAD_TO:
            x = x + jnp.where(lane >= step, pltpu.roll(x, step, axis=1), 0.0)
            step *= 2
        x = x + carry
        o_ref[:, sl] = x
        carry = x[:, PAD_TO - 1:PAD_TO]


def _forget_cumsum(ff_t, bias_col):
    rows, lp = ff_t.shape
    return pl.pallas_call(
        _forget_cumsum_kernel,
        out_shape=jax.ShapeDtypeStruct((rows, lp), F32),
        name="fox_forget_cumsum",
    )(ff_t, bias_col)


def _cumsum_rows(x):
    row = lax.broadcasted_iota(jnp.int32, x.shape, 0)
    step = 1
    while step < x.shape[0]:
        x = x + jnp.where(row >= step, pltpu.roll(x, step, axis=0), 0.0)
        step *= 2
    return x


def _gla_kernel(q_ref, k_ref, v_ref, r_ref, a_ref, wa_ref, ba_ref, on_ref, o_ref, st_ref):
    tm = q_ref.shape[0]

    @pl.when(pl.program_id(1) == 0)
    def _():
        st_ref[...] = jnp.zeros_like(st_ref)

    causal = (lax.broadcasted_iota(jnp.int32, (CHUNK, CHUNK), 1)
              <= lax.broadcasted_iota(jnp.int32, (CHUNK, CHUNK), 0))

    def body(c, carry):
        r0 = pl.multiple_of(c * CHUNK, CHUNK)
        rows = pl.ds(r0, CHUNK)
        gate = jnp.dot(a_ref[rows, :].astype(BF16), wa_ref[...], preferred_element_type=F32)
        for h in range(GLA_HEADS):
            ks = slice(h * GLA_DK, (h + 1) * GLA_DK)
            vs = slice(h * GLA_DV, (h + 1) * GLA_DV)
            g = _log_sigmoid(gate[:, ks] + ba_ref[:, ks]) / GLA_TAU
            bcum = _cumsum_rows(g)
            b_last = bcum[CHUNK - 1:CHUNK, :]
            k = k_ref[rows, ks]
            v = v_ref[rows, vs].astype(BF16)
            q_dec = ((q_ref[rows, ks] * (GLA_DK ** -0.5)) * jnp.exp(bcum)).astype(BF16)
            k_inv = (k * jnp.exp(-bcum)).astype(BF16)
            k_end = (k * jnp.exp(b_last - bcum)).astype(BF16)
            a = lax.dot_general(q_dec, k_inv, NT_DIMS, preferred_element_type=F32)
            a = jnp.where(causal, a, 0.0).astype(BF16)
            st = st_ref[h]
            o = (jnp.dot(a, v, preferred_element_type=F32)
                 + lax.dot_general(q_dec, st.astype(BF16), NT_DIMS, preferred_element_type=F32))
            st_ref[h] = st * jnp.exp(b_last) + lax.dot_general(v, k_end, TN_DIMS,
                                                               preferred_element_type=F32)
            o = _rms(o, on_ref[...])
            o_ref[rows, vs] = (o * _silu(r_ref[rows, vs])).astype(o_ref.dtype)
        return carry

    lax.fori_loop(0, tm // CHUNK, body, 0)


def _gla(cd3, gates3, wa, layer, ba, o_norm):
    b, lp, _ = cd3.shape
    tm = SEQ_TILE
    kw, vw = GLA_HEADS * GLA_DK, GLA_HEADS * GLA_DV
    return pl.pallas_call(
        _gla_kernel,
        grid=(b, lp // tm),
        in_specs=[pl.BlockSpec((None, tm, kw), lambda bi, i: (bi, i, 6144 // kw)),
                  pl.BlockSpec((None, tm, kw), lambda bi, i: (bi, i, 7168 // kw)),
                  pl.BlockSpec((None, tm, vw), lambda bi, i: (bi, i, 8192 // vw)),
                  pl.BlockSpec((None, tm, vw), lambda bi, i: (bi, i, 10240 // vw)),
                  pl.BlockSpec((None, tm, PAD_TO), lambda bi, i: (bi, i, 0)),
                  _const_spec((PAD_TO, kw), layer), _const_spec((1, kw)), _const_spec((1, GLA_DV))],
        out_specs=pl.BlockSpec((None, tm, vw), lambda bi, i: (bi, i, 0)),
        out_shape=jax.ShapeDtypeStruct((b, lp, vw), BF16),
        scratch_shapes=[pltpu.VMEM((GLA_HEADS, GLA_DV, GLA_DK), F32)],
        compiler_params=_params(("parallel", "arbitrary"), VMEM_MID),
        name="gla",
    )(cd3, cd3, cd3, cd3, gates3, wa, ba.reshape(1, -1), o_norm.reshape(1, -1))


def _ab_in_weight(w):
    wt = jnp.swapaxes(w, 1, 2)
    zeros = jnp.zeros((w.shape[0], 3840 - w.shape[2], w.shape[1]), w.dtype)
    return jnp.concatenate([wt, zeros], axis=1).astype(BF16)


def _cd_in_weight(w):
    wt = jnp.swapaxes(w, 1, 2)
    main = jnp.concatenate([wt[:, :6144], wt[:, 6160:10256], wt[:, 10272:]], axis=1)
    zeros = jnp.zeros((w.shape[0], PAD_TO - 2 * GLA_RANK, w.shape[1]), w.dtype)
    side = jnp.concatenate([wt[:, 6144:6160], wt[:, 10256:10272], zeros], axis=1)
    return main.astype(BF16), side.astype(BF16)


def _mla_q_weight(w):
    n, r, _ = w.shape
    w = jnp.pad(w.astype(BF16).reshape(n, r, HEADS, MLA_QK),
                ((0, 0), (0, 0), (0, 0), (0, 2 * HEAD_DIM - MLA_QK)))
    return w.reshape(n, r, HEADS * 2 * HEAD_DIM)


def _gla_gate_weight(w):
    return jnp.pad(w.astype(BF16), ((0, 0), (GLA_RANK, PAD_TO - 2 * GLA_RANK), (0, 0)))


def _rotary_tables(lp):
    pos = jnp.arange(lp, dtype=F32)
    inv_freq = ROPE_BASE ** (-jnp.arange(0, MLA_ROPE, 2, dtype=F32) / MLA_ROPE)
    ang = pos[:, None] * inv_freq[None, :]
    cos, sin = jnp.cos(ang), jnp.sin(ang)
    z32, z64 = jnp.zeros_like(cos), jnp.zeros((lp, HEAD_DIM - MLA_ROPE), F32)
    return (jnp.concatenate([cos, cos, z64], axis=1),
            jnp.concatenate([-sin, z32, z64], axis=1),
            jnp.concatenate([z32, sin, z64], axis=1))


def _ffn(h, norm, w_gate, w_up, w_down, layer, name):
    hidden = _norm_matmul(h, norm, [w_gate, w_up], layer, _swiglu_epilogue, BF16, ROW_TILE, 512,
                          name + "_up")
    return _matmul_residual([hidden], w_down, layer, h, 0.5, RES_ROWS, RES_COLS, name + "_down")


def _pool_mla_layer(h, b, lp, norm, w_in, pool_w, pool_scale, q_norm, wq, kv_norm, wkv,
                    q_gain, k_gain, w_out, layer, tables):
    ab = _norm_matmul(h, norm, [w_in], layer, _identity, F32, ROW_TILE, 768, "ab_in",
                      w_transposed=True)
    y_pool = _pool_mixer(ab, pool_w, layer, pool_scale, lp)
    q, k, v = _mla_prep(ab.reshape(b, lp, -1), tables, q_norm, kv_norm, wq, wkv, layer,
                        q_gain, k_gain, MLA_QK ** -0.5 * LOG2E)
    y_mla = _mla_attention(q, k, v)
    return _matmul_residual([y_pool, y_mla.reshape(b * lp, -1)], w_out, layer, h, 1.0,
                            RES_ROWS, OUT_COLS, "ab_out")


def _fox_gla_layer(h, b, lp, norm, w_in, fox_q_gain, fox_k_gain, fox_f_bias, wa, gla_b_a,
                   gla_o_norm, w_out, layer):
    w_main, w_side = w_in
    cd, gates = _norm_matmul(h, norm, [w_main], layer, _identity, F32, ROW_TILE, 768, "cd_in",
                             w_transposed=True, w_side=w_side)
    cd3 = cd.reshape(b, lp, -1)
    ff_t = gates[:, :HEADS].reshape(b, lp, HEADS).transpose(0, 2, 1).reshape(b * HEADS, lp)
    f = _forget_cumsum(ff_t, jnp.tile(fox_f_bias, b).reshape(b * HEADS, 1))
    y_fox = _fox_attention(cd3, f.reshape(b, HEADS, 1, lp), fox_q_gain, fox_k_gain,
                           HEAD_DIM ** -0.5 * LOG2E)
    y_gla = _gla(cd3, gates.reshape(b, lp, -1), wa, layer, gla_b_a, gla_o_norm)
    return _matmul_residual([y_fox.reshape(b * lp, -1), y_gla.reshape(b * lp, -1)], w_out, layer,
                            h, 1.0, RES_ROWS, OUT_COLS, "cd_out")


def kernel(x, meta_tokens, ffn1_norm, ffn1_w_gate, ffn1_w_up, ffn1_w_down, mix_norm, ffn2_norm, ffn2_w_gate, ffn2_w_up, ffn2_w_down, ab_w_in, pool_w, pool_scale, mla_q_norm, mla_w_q_up, mla_kv_norm, mla_w_kv_up, mla_q_gain, mla_k_gain, ab_w_out, cd_w_in, fox_q_gain, fox_k_gain, fox_f_bias, gla_w_a2, gla_b_a, gla_o_norm, cd_w_out):
    b, s, d = x.shape
    length = N_META + s
    lp = -(-length // PAD_TO) * PAD_TO
    meta = jnp.broadcast_to(meta_tokens.astype(x.dtype)[None], (b, N_META, d))
    h = jnp.concatenate([meta, x, jnp.zeros((b, lp - length, d), x.dtype)], axis=1)
    h = h.reshape(b * lp, d)
    tables = _rotary_tables(lp)

    ffn1 = [w.astype(BF16) for w in (ffn1_w_gate, ffn1_w_up, ffn1_w_down)]
    ffn2 = [w.astype(BF16) for w in (ffn2_w_gate, ffn2_w_up, ffn2_w_down)]
    ab_in, cd_in = _ab_in_weight(ab_w_in), _cd_in_weight(cd_w_in)
    ab_out, cd_out = ab_w_out.astype(BF16), cd_w_out.astype(BF16)
    pool_wb, wq, wkv = pool_w.astype(BF16), _mla_q_weight(mla_w_q_up), mla_w_kv_up.astype(BF16)
    wa = _gla_gate_weight(gla_w_a2)

    for layer in range(ffn1_norm.shape[0]):
        i = layer // 2
        h = _ffn(h, ffn1_norm[layer], *ffn1, layer, "ffn1")
        if layer % 2 == 0:
            h = _pool_mla_layer(h, b, lp, mix_norm[layer], ab_in, pool_wb, pool_scale[i],
                                mla_q_norm[i], wq, mla_kv_norm[i], wkv, mla_q_gain[i],
                                mla_k_gain[i], ab_out, i, tables)
        else:
            h = _fox_gla_layer(h, b, lp, mix_norm[layer], cd_in, fox_q_gain[i], fox_k_gain[i],
                               fox_f_bias[i], wa, gla_b_a[i], gla_o_norm[i], cd_out, i)
        h = _ffn(h, ffn2_norm[layer], *ffn2, layer, "ffn2")
    return h.reshape(b, lp, d)[:, N_META:N_META + s]
```

```python
import functools
import math

import jax
import jax.numpy as jnp
from jax import lax
from jax.experimental import pallas as pl
from jax.experimental.pallas import tpu as pltpu

F32 = jnp.float32
BF16 = jnp.bfloat16

N_META = 16
PAD_TO = 128
RMS_EPS = 1e-6
ROPE_BASE = 10000.0
CHUNK = 64

POOL_WINDOWS = (2, 4, 8, 16)
POOL_GROUP = 512
HEADS = 16
HEAD_DIM = 128
MLA_ROPE = 64
MLA_QK = 192
GLA_HEADS = 4
GLA_DK = 256
GLA_DV = 512
GLA_TAU = 16.0
GLA_RANK = 16

VMEM_BIG = 58 * 1024 * 1024
VMEM_MID = 40 * 1024 * 1024

ROW_TILE = 768
RES_ROWS = 1056
RES_COLS = 512
OUT_COLS = 1024
SEQ_TILE = 384
ATT_HEADS = 4
NEG_BIG = -1e30
LOG2E = math.log2(math.e)

NT_DIMS = (((1,), (1,)), ((), ()))
TN_DIMS = (((0,), (0,)), ((), ()))


def _silu(x):
    return x / (1.0 + jnp.exp(-x))


def _log_sigmoid(x):
    return jnp.minimum(x, 0.0) - jnp.log(1.0 + jnp.exp(-jnp.abs(x)))


def _rms(x, gain, n=None):
    n = x.shape[-1] if n is None else n
    ss = jnp.sum(x * x, axis=-1, keepdims=True)
    return x * lax.rsqrt(ss / n + RMS_EPS) * gain


def _params(sem, vmem):
    return pltpu.CompilerParams(dimension_semantics=sem, vmem_limit_bytes=vmem)


def _const_spec(shape, layer=None):
    if layer is None:
        nd = len(shape)
        return pl.BlockSpec(shape, lambda *_: (0,) * nd, pipeline_mode=pl.Buffered(1))
    nd = len(shape)
    return pl.BlockSpec((None,) + tuple(shape), lambda *_: (layer,) + (0,) * nd,
                        pipeline_mode=pl.Buffered(1))


def _norm_mm_kernel(a_ref, g_ref, *rest, n_w, epilogue, rows, w_transposed, has_side):
    if has_side:
        w_side_ref, rest = rest[0], rest[1:]
    w_refs, o_ref = rest[:n_w], rest[n_w]
    side_ref = rest[n_w + 1] if has_side else None
    xn_ref = rest[-1]
    dims = NT_DIMS if w_transposed else (((1,), (0,)), ((), ()))

    @pl.when(pl.program_id(1) == 0)
    def _():
        def body(c, carry):
            r0 = pl.multiple_of(c * rows, rows)
            a = a_ref[pl.ds(r0, rows), :]
            xn_ref[pl.ds(r0, rows), :] = _rms(a, g_ref[...]).astype(BF16)
            return carry
        lax.fori_loop(0, a_ref.shape[0] // rows, body, 0)
        if has_side:
            side_ref[...] = lax.dot_general(xn_ref[...], w_side_ref[...], NT_DIMS,
                                            preferred_element_type=F32)

    xn = xn_ref[...]
    outs = [lax.dot_general(xn, w[...], dims, preferred_element_type=F32) for w in w_refs]
    o_ref[...] = epilogue(*outs).astype(o_ref.dtype)


def _norm_matmul(a, gain, ws, layer, epilogue, out_dtype, tm, tn, name, w_transposed=False,
                 n=None, side_rows=0):
    t, k = a.shape
    n = ws[0].shape[1 if w_transposed else 2] if n is None else n
    has_side = side_rows > 0
    kern = functools.partial(_norm_mm_kernel, n_w=len(ws), epilogue=epilogue, rows=64,
                             w_transposed=w_transposed, has_side=has_side)
    if w_transposed:
        w_spec = pl.BlockSpec((None, tn, k), lambda i, j: (layer, j, 0))
    else:
        w_spec = pl.BlockSpec((None, k, tn), lambda i, j: (layer, 0, j))
    main_spec = pl.BlockSpec((tm, tn), lambda i, j: (i, j))
    main_shape = jax.ShapeDtypeStruct((t, n), out_dtype)
    side_in, side_args, out_specs, out_shape = [], [], main_spec, main_shape
    if has_side:
        assert w_transposed and n % side_rows == 0
        side_in = [pl.BlockSpec((None, side_rows, k), lambda i, j: (layer, n // side_rows, 0),
                                pipeline_mode=pl.Buffered(1))]
        side_args = [ws[0]]
        out_specs = [main_spec, pl.BlockSpec((tm, side_rows), lambda i, j: (i, 0))]
        out_shape = [main_shape, jax.ShapeDtypeStruct((t, side_rows), F32)]
    return pl.pallas_call(
        kern,
        grid=(t // tm, n // tn),
        in_specs=[pl.BlockSpec((tm, k), lambda i, j: (i, 0)),
                  pl.BlockSpec((1, k), lambda i, j: (0, 0))] + side_in + [w_spec for _ in ws],
        out_specs=out_specs,
        out_shape=out_shape,
        scratch_shapes=[pltpu.VMEM((tm, k), BF16)],
        compiler_params=_params(("parallel", "arbitrary"), VMEM_BIG),
        name=name,
    )(a, gain.reshape(1, k), *side_args, *ws)


def _swiglu_epilogue(g, u):
    return _silu(g) * u


def _identity(x):
    return x


def _mm_res_kernel(*refs, k_sizes, scale):
    n_a = len(k_sizes)
    a_refs, w_ref, r_ref, o_ref = refs[:n_a], refs[n_a], refs[n_a + 1], refs[n_a + 2]
    acc, off = None, 0
    for a_ref, ks in zip(a_refs, k_sizes):
        p = jnp.dot(a_ref[...], w_ref[off:off + ks, :], preferred_element_type=F32)
        acc = p if acc is None else acc + p
        off += ks
    if scale != 1.0:
        acc = scale * acc
    o_ref[...] = r_ref[...] + acc


def _matmul_residual(a_list, w, layer, res, scale, tm, tn, name):
    t, n = res.shape
    k_sizes = tuple(a.shape[1] for a in a_list)
    k = sum(k_sizes)
    kern = functools.partial(_mm_res_kernel, k_sizes=k_sizes, scale=scale)
    n_a = len(a_list)
    return pl.pallas_call(
        kern,
        grid=(t // tm, n // tn),
        in_specs=[pl.BlockSpec((tm, ks), lambda i, j: (i, 0)) for ks in k_sizes]
                 + [pl.BlockSpec((None, k, tn), lambda i, j: (layer, 0, j)),
                    pl.BlockSpec((tm, tn), lambda i, j: (i, j))],
        out_specs=pl.BlockSpec((tm, tn), lambda i, j: (i, j)),
        out_shape=jax.ShapeDtypeStruct((t, n), F32),
        input_output_aliases={n_a + 1: 0},
        compiler_params=_params(("parallel", "arbitrary"), VMEM_BIG),
        name=name,
    )(*a_list, w, res)


def _pool_kernel(x_ref, halo_ref, w_ref, s_ref, o_ref, xs_ref, *, tiles_per_seq):
    tm = x_ref.shape[0]
    halo = POOL_WINDOWS[-1]
    it = pl.program_id(0) % tiles_per_seq
    keep = (it > 0).astype(F32)
    xs_ref[0:halo, :] = halo_ref[...] * keep
    xs_ref[halo:halo + tm, :] = x_ref[...]
    pos = it * tm + lax.broadcasted_iota(jnp.int32, (tm, 1), 0)
    for g, win in enumerate(POOL_WINDOWS):
        c0 = g * POOL_GROUP
        x = xs_ref[halo:halo + tm, c0:c0 + POOL_GROUP]
        acc = x
        for j in range(1, win):
            acc = acc + xs_ref[halo - j:halo - j + tm, c0:c0 + POOL_GROUP]
        cnt = jnp.minimum(pos + 1, win).astype(F32)
        d = acc / cnt - x
        y = jnp.dot(d.astype(BF16), w_ref[g], preferred_element_type=F32)
        o_ref[:, c0:c0 + POOL_GROUP] = (y * s_ref[:, c0:c0 + POOL_GROUP]).astype(o_ref.dtype)


def _pool_mixer(ab, pool_w, layer, pool_scale, lp):
    t = ab.shape[0]
    tm, halo = SEQ_TILE, POOL_WINDOWS[-1]
    width = len(POOL_WINDOWS) * POOL_GROUP
    per_halo = tm // halo
    kern = functools.partial(_pool_kernel, tiles_per_seq=lp // tm)
    return pl.pallas_call(
        kern,
        grid=(t // tm,),
        in_specs=[pl.BlockSpec((tm, width), lambda i: (i, 0)),
                  pl.BlockSpec((halo, width), lambda i: (jnp.maximum(i * per_halo - 1, 0), 0)),
                  _const_spec((len(POOL_WINDOWS), POOL_GROUP, POOL_GROUP), layer),
                  _const_spec((1, width))],
        out_specs=pl.BlockSpec((tm, width), lambda i: (i, 0)),
        out_shape=jax.ShapeDtypeStruct((t, width), BF16),
        scratch_shapes=[pltpu.VMEM((tm + halo, width), F32)],
        compiler_params=_params(("parallel",), VMEM_MID),
        name="pool_mixer",
    )(ab, ab, pool_w, pool_scale.reshape(1, width))


def _rotary_slot(x, c_ref, s1_ref, s2_ref):
    return (x * c_ref[...] + pltpu.roll(x, 96, axis=1) * s1_ref[...]
            + pltpu.roll(x, 32, axis=1) * s2_ref[...])


def _mla_prep_kernel(cq_ref, ckv_ref, kpe_ref, c_ref, s1_ref, s2_ref, qn_ref, kvn_ref,
                     wq_ref, wkv_ref, qgn_ref, qgr_ref, kgn_ref, kgr_ref,
                     q_ref, k_ref, v_ref, *, q_scale):
    cq = _rms(cq_ref[...], qn_ref[...]).astype(BF16)
    ckv = _rms(ckv_ref[...], kvn_ref[...]).astype(BF16)
    kr = _rotary_slot(_rms(kpe_ref[...], kgr_ref[...], MLA_ROPE), c_ref, s1_ref, s2_ref).astype(BF16)
    for h in range(HEADS):
        lo = h * 2 * HEAD_DIM
        mid, hi = lo + HEAD_DIM, lo + 2 * HEAD_DIM
        q = jnp.dot(cq, wq_ref[:, lo:hi], preferred_element_type=F32)
        qn = _rms(q[:, :HEAD_DIM], qgn_ref[...])
        qr = _rotary_slot(_rms(q[:, HEAD_DIM:], qgr_ref[...], MLA_ROPE), c_ref, s1_ref, s2_ref)
        q_ref[:, lo:mid] = (qn * q_scale).astype(BF16)
        q_ref[:, mid:hi] = (qr * q_scale).astype(BF16)
        kv = jnp.dot(ckv, wkv_ref[:, lo:hi], preferred_element_type=F32)
        k_ref[:, lo:mid] = _rms(kv[:, :HEAD_DIM], kgn_ref[...]).astype(BF16)
        k_ref[:, mid:hi] = kr
        v_ref[:, h * HEAD_DIM:(h + 1) * HEAD_DIM] = kv[:, HEAD_DIM:].astype(BF16)


def _mla_prep(ab3, tables, q_norm, kv_norm, wq, wkv, layer, q_gain, k_gain, q_scale):
    b, lp, _ = ab3.shape
    tm = SEQ_TILE
    q_rank, kv_rank = wq.shape[1], wkv.shape[1]
    cat = HEADS * 2 * HEAD_DIM
    zeros = jnp.zeros((HEAD_DIM - MLA_ROPE,), F32)
    slot = lambda g: jnp.concatenate([g[HEAD_DIM:], zeros]).reshape(1, HEAD_DIM)
    row = lambda i_, j_: (j_, 0)
    out = jax.ShapeDtypeStruct((b, lp, cat), BF16)
    return pl.pallas_call(
        functools.partial(_mla_prep_kernel, q_scale=q_scale),
        grid=(b, lp // tm),
        in_specs=[pl.BlockSpec((None, tm, q_rank), lambda i, j: (i, j, 2048 // q_rank)),
                  pl.BlockSpec((None, tm, kv_rank), lambda i, j: (i, j, 3072 // kv_rank)),
                  pl.BlockSpec((None, tm, HEAD_DIM), lambda i, j: (i, j, 3584 // HEAD_DIM)),
                  pl.BlockSpec((tm, HEAD_DIM), row),
                  pl.BlockSpec((tm, HEAD_DIM), row),
                  pl.BlockSpec((tm, HEAD_DIM), row),
                  _const_spec((1, q_rank)), _const_spec((1, kv_rank)),
                  _const_spec((q_rank, cat), layer), _const_spec((kv_rank, cat), layer),
                  _const_spec((1, HEAD_DIM)), _const_spec((1, HEAD_DIM)),
                  _const_spec((1, HEAD_DIM)), _const_spec((1, HEAD_DIM))],
        out_specs=[pl.BlockSpec((None, tm, cat), lambda i, j: (i, j, 0))] * 2
                  + [pl.BlockSpec((None, tm, HEADS * HEAD_DIM), lambda i, j: (i, j, 0))],
        out_shape=[out, out, jax.ShapeDtypeStruct((b, lp, HEADS * HEAD_DIM), BF16)],
        compiler_params=_params(("parallel", "parallel"), VMEM_BIG),
        name="mla_prep",
    )(ab3, ab3, ab3, *tables, q_norm.reshape(1, -1), kv_norm.reshape(1, -1), wq, wkv,
      q_gain[:HEAD_DIM].reshape(1, HEAD_DIM), slot(q_gain),
      k_gain[:HEAD_DIM].reshape(1, HEAD_DIM), slot(k_gain))


def _softmax_sweep(qs, key_fn, val_fn, bias_fn, n_full, tq, diag_mask, tail, scratch):
    nh = len(qs)
    s_even, s_odd, m_ref, l_ref, acc_ref = scratch[:5]
    tail_ref = scratch[5] if tail is not None else None

    def scores(h, start, width):
        s = lax.dot_general(qs[h], key_fn(h, start, width), NT_DIMS, preferred_element_type=F32)
        return bias_fn(h, s, start, width)

    def lanes(x, width):
        return jnp.concatenate([x] * (width // PAD_TO), axis=1)

    def absorb(h, s, vals):
        m = m_ref[h]
        m_new = jnp.maximum(m, jnp.max(s, axis=-1, keepdims=True))
        alpha = jnp.exp2(m - m_new)
        p = jnp.exp2(s - lanes(m_new, s.shape[1]))
        p_sum = p[:, :PAD_TO]
        for c in range(1, s.shape[1] // PAD_TO):
            p_sum = p_sum + p[:, c * PAD_TO:(c + 1) * PAD_TO]
        l_ref[h] = alpha * l_ref[h] + p_sum
        acc_ref[h] = alpha * acc_ref[h] + jnp.dot(p.astype(BF16), vals, preferred_element_type=F32)
        m_ref[h] = m_new

    def half_step(c, cur, nxt, diagonal):
        start = pl.multiple_of(c * tq, tq)
        for h in range(nh):
            if not diagonal:
                nxt[h] = scores(h, start + tq, tq)
                absorb(h, cur[h], val_fn(h, start, tq))
                continue
            s = jnp.where(diag_mask(), cur[h], NEG_BIG)
            vals = val_fn(h, start, tq)
            if tail is not None:
                t_start, t_width, t_mask = tail
                s_tail = jnp.where(t_mask(), tail_ref[h], NEG_BIG)
                s = jnp.concatenate([s, s_tail], axis=1)
                vals = jnp.concatenate([vals, val_fn(h, t_start, t_width)], axis=0)
            absorb(h, s, vals)

    def by_parity(c, diagonal):
        return lambda: lax.cond(c % 2 == 0, lambda: half_step(c, s_even, s_odd, diagonal),
                                lambda: half_step(c, s_odd, s_even, diagonal))

    def step(c, carry):
        lax.cond(c == n_full, by_parity(c, True), by_parity(c, False))
        return carry

    for h in range(nh):
        m_ref[h] = jnp.full((tq, PAD_TO), NEG_BIG, F32)
        l_ref[h] = jnp.zeros((tq, PAD_TO), F32)
        acc_ref[h] = jnp.zeros((tq, HEAD_DIM), F32)
        s_even[h] = scores(h, 0, tq)
        if tail is not None:
            tail_ref[h] = scores(h, tail[0], tail[1])
    lax.fori_loop(0, n_full + 1, step, 0)
    return [acc_ref[h] / jnp.sum(l_ref[h], axis=-1, keepdims=True) for h in range(nh)]


def _sweep_scratch(tq, tail_width=0):
    bufs = [pltpu.VMEM((ATT_HEADS, tq, tq), F32), pltpu.VMEM((ATT_HEADS, tq, tq), F32),
            pltpu.VMEM((ATT_HEADS, tq, PAD_TO), F32), pltpu.VMEM((ATT_HEADS, tq, PAD_TO), F32),
            pltpu.VMEM((ATT_HEADS, tq, HEAD_DIM), F32)]
    if tail_width:
        bufs.append(pltpu.VMEM((ATT_HEADS, tq, tail_width), F32))
    return bufs


def _mla_attn_kernel(q_ref, k_ref, v_ref, o_ref, *scratch):
    tq, lk = q_ref.shape[0], k_ref.shape[0]
    width = 2 * HEAD_DIM
    i = pl.program_id(2)
    q0 = i * tq
    shift = CHUNK - N_META

    def chunk_mask(k_start, k_width):
        bits = CHUNK.bit_length() - 1
        qc = lax.shift_right_logical(
            q0 + shift + lax.broadcasted_iota(jnp.int32, (tq, k_width), 0), bits)
        kc = lax.shift_right_logical(
            k_start + shift + lax.broadcasted_iota(jnp.int32, (tq, k_width), 1), bits)
        return kc <= qc

    t_start = pl.multiple_of(jnp.minimum(q0 + tq, lk - PAD_TO), PAD_TO)

    def t_mask():
        t_kpos = t_start + lax.broadcasted_iota(jnp.int32, (tq, PAD_TO), 1)
        return (t_kpos >= q0 + tq) & chunk_mask(t_start, PAD_TO)

    qs = [q_ref[:, h * width:(h + 1) * width] for h in range(ATT_HEADS)]
    outs = _softmax_sweep(
        qs,
        lambda h, s0, w: k_ref[pl.ds(s0, w), h * width:(h + 1) * width],
        lambda h, s0, w: v_ref[pl.ds(s0, w), h * HEAD_DIM:(h + 1) * HEAD_DIM],
        lambda h, s, s0, w: s,
        i, tq, lambda: chunk_mask(q0, tq), (t_start, PAD_TO, t_mask), scratch)
    for h in range(ATT_HEADS):
        o_ref[:, h * HEAD_DIM:(h + 1) * HEAD_DIM] = outs[h].astype(o_ref.dtype)


def _mla_attention(q, k, v):
    b, lp, _ = q.shape
    tq = SEQ_TILE
    gw = ATT_HEADS * 2 * HEAD_DIM
    return pl.pallas_call(
        _mla_attn_kernel,
        grid=(b, HEADS // ATT_HEADS, lp // tq),
        in_specs=[pl.BlockSpec((None, tq, gw), lambda bi, g, i: (bi, i, g)),
                  pl.BlockSpec((None, lp, gw), lambda bi, g, i: (bi, 0, g)),
                  pl.BlockSpec((None, lp, ATT_HEADS * HEAD_DIM), lambda bi, g, i: (bi, 0, g))],
        out_specs=pl.BlockSpec((None, tq, ATT_HEADS * HEAD_DIM), lambda bi, g, i: (bi, i, g)),
        out_shape=jax.ShapeDtypeStruct((b, lp, HEADS * HEAD_DIM), BF16),
        scratch_shapes=_sweep_scratch(tq, PAD_TO),
        compiler_params=_params(("parallel", "parallel", "arbitrary"), VMEM_BIG),
        name="mla_attention",
    )(q, k, v)


def _fox_attn_kernel(q_ref, k_ref, v_ref, f_ref, qg_ref, kg_ref, o_ref, kn_ref, va_ref, fb_ref,
                     *scratch, q_scale):
    tq, lk = q_ref.shape[0], k_ref.shape[0]
    i = pl.program_id(2)
    q0 = pl.multiple_of(i * tq, tq)
    heads = [slice(h * HEAD_DIM, (h + 1) * HEAD_DIM) for h in range(ATT_HEADS)]

    @pl.when(i == 0)
    def _():
        def body(c, carry):
            rows = pl.ds(pl.multiple_of(c * tq, tq), tq)
            for sl in heads:
                kn_ref[rows, sl] = _rms(k_ref[rows, sl], kg_ref[...]).astype(BF16)
            va_ref[rows, :] = v_ref[rows, :].astype(BF16)
            return carry
        lax.fori_loop(0, lk // tq, body, 0)
        fb_ref[...] = f_ref[...] * LOG2E

    qs = [(_rms(q_ref[:, sl], qg_ref[...]) * q_scale).astype(BF16) for sl in heads]
    f0 = [jnp.max(fb_ref[h, :, pl.ds(q0, tq)], axis=-1, keepdims=True) for h in range(ATT_HEADS)]

    def causal():
        return (lax.broadcasted_iota(jnp.int32, (tq, tq), 1)
                <= lax.broadcasted_iota(jnp.int32, (tq, tq), 0))

    outs = _softmax_sweep(
        qs,
        lambda h, s0, w: kn_ref[pl.ds(s0, w), heads[h]],
        lambda h, s0, w: va_ref[pl.ds(s0, w), heads[h]],
        lambda h, s, s0, w: s - (fb_ref[h, :, pl.ds(s0, w)] - f0[h]),
        i, tq, causal, None, scratch)
    for h, sl in enumerate(heads):
        o_ref[:, sl] = outs[h].astype(o_ref.dtype)


def _fox_attention(cd3, f, q_gain, k_gain, q_scale):
    b, lp, _ = cd3.shape
    tq = SEQ_TILE
    gw = ATT_HEADS * HEAD_DIM
    groups = HEADS // ATT_HEADS
    kern = functools.partial(_fox_attn_kernel, q_scale=q_scale)
    return pl.pallas_call(
        kern,
        grid=(b, groups, lp // tq),
        in_specs=[pl.BlockSpec((None, tq, gw), lambda bi, g, i: (bi, i, g)),
                  pl.BlockSpec((None, lp, gw), lambda bi, g, i: (bi, 0, groups + g)),
                  pl.BlockSpec((None, lp, gw), lambda bi, g, i: (bi, 0, 2 * groups + g)),
                  pl.BlockSpec((None, ATT_HEADS, 1, lp), lambda bi, g, i: (bi, g, 0, 0)),
                  _const_spec((1, HEAD_DIM)), _const_spec((1, HEAD_DIM))],
        out_specs=pl.BlockSpec((None, tq, gw), lambda bi, g, i: (bi, i, g)),
        out_shape=jax.ShapeDtypeStruct((b, lp, HEADS * HEAD_DIM), BF16),
        scratch_shapes=[pltpu.VMEM((lp, gw), BF16),
                        pltpu.VMEM((lp, gw), BF16),
                        pltpu.VMEM((ATT_HEADS, 1, lp), F32)] + _sweep_scratch(tq),
        compiler_params=_params(("parallel", "parallel", "arbitrary"), VMEM_BIG),
        name="fox_attention",
    )(cd3, cd3, cd3, f, q_gain.reshape(1, HEAD_DIM), k_gain.reshape(1, HEAD_DIM))


def _forget_cumsum_kernel(x_ref, b_ref, o_ref):
    rows, lp = x_ref.shape
    lane = lax.broadcasted_iota(jnp.int32, (rows, PAD_TO), 1)
    carry = jnp.zeros((rows, 1), F32)
    for c in range(lp // PAD_TO):
        sl = slice(c * PAD_TO, (c + 1) * PAD_TO)
        x = _log_sigmoid(x_ref[:, sl] + b_ref[...])
        step = 1
        while step < PAD_TO:
            x = x + jnp.where(lane >= step, pltpu.roll(x, step, axis=1), 0.0)
            step *= 2
        x = x + carry
        o_ref[:, sl] = x
        carry = x[:, PAD_TO - 1:PAD_TO]


def _forget_cumsum(ff_t, bias_col):
    rows, lp = ff_t.shape
    return pl.pallas_call(
        _forget_cumsum_kernel,
        out_shape=jax.ShapeDtypeStruct((rows, lp), F32),
        name="fox_forget_cumsum",
    )(ff_t, bias_col)


def _cumsum_rows(x):
    row = lax.broadcasted_iota(jnp.int32, x.shape, 0)
    step = 1
    while step < x.shape[0]:
        x = x + jnp.where(row >= step, pltpu.roll(x, step, axis=0), 0.0)
        step *= 2
    return x


def _gla_kernel(q_ref, k_ref, v_ref, r_ref, a_ref, wa_ref, ba_ref, on_ref, o_ref, st_ref):
    tm = q_ref.shape[0]

    @pl.when(pl.program_id(1) == 0)
    def _():
        st_ref[...] = jnp.zeros_like(st_ref)

    causal = (lax.broadcasted_iota(jnp.int32, (CHUNK, CHUNK), 1)
              <= lax.broadcasted_iota(jnp.int32, (CHUNK, CHUNK), 0))

    def body(c, carry):
        r0 = pl.multiple_of(c * CHUNK, CHUNK)
        rows = pl.ds(r0, CHUNK)
        gate = jnp.dot(a_ref[rows, :].astype(BF16), wa_ref[...], preferred_element_type=F32)
        for h in range(GLA_HEADS):
            ks = slice(h * GLA_DK, (h + 1) * GLA_DK)
            vs = slice(h * GLA_DV, (h + 1) * GLA_DV)
            g = _log_sigmoid(gate[:, ks] + ba_ref[:, ks]) / GLA_TAU
            bcum = _cumsum_rows(g)
            b_last = bcum[CHUNK - 1:CHUNK, :]
            k = k_ref[rows, ks]
            v = v_ref[rows, vs].astype(BF16)
            q_dec = ((q_ref[rows, ks] * (GLA_DK ** -0.5)) * jnp.exp(bcum)).astype(BF16)
            k_inv = (k * jnp.exp(-bcum)).astype(BF16)
            k_end = (k * jnp.exp(b_last - bcum)).astype(BF16)
            a = lax.dot_general(q_dec, k_inv, NT_DIMS, preferred_element_type=F32)
            a = jnp.where(causal, a, 0.0).astype(BF16)
            st = st_ref[h]
            o = (jnp.dot(a, v, preferred_element_type=F32)
                 + lax.dot_general(q_dec, st.astype(BF16), NT_DIMS, preferred_element_type=F32))
            st_ref[h] = st * jnp.exp(b_last) + lax.dot_general(v, k_end, TN_DIMS,
                                                               preferred_element_type=F32)
            o = _rms(o, on_ref[...])
            o_ref[rows, vs] = (o * _silu(r_ref[rows, vs])).astype(o_ref.dtype)
        return carry

    lax.fori_loop(0, tm // CHUNK, body, 0)


def _gla(cd3, gates3, wa, layer, ba, o_norm):
    b, lp, _ = cd3.shape
    tm = SEQ_TILE
    kw, vw = GLA_HEADS * GLA_DK, GLA_HEADS * GLA_DV
    return pl.pallas_call(
        _gla_kernel,
        grid=(b, lp // tm),
        in_specs=[pl.BlockSpec((None, tm, kw), lambda bi, i: (bi, i, 6144 // kw)),
                  pl.BlockSpec((None, tm, kw), lambda bi, i: (bi, i, 7168 // kw)),
                  pl.BlockSpec((None, tm, vw), lambda bi, i: (bi, i, 8192 // vw)),
                  pl.BlockSpec((None, tm, vw), lambda bi, i: (bi, i, 10240 // vw)),
                  pl.BlockSpec((None, tm, PAD_TO), lambda bi, i: (bi, i, 0)),
                  _const_spec((PAD_TO, kw), layer), _const_spec((1, kw)), _const_spec((1, GLA_DV))],
        out_specs=pl.BlockSpec((None, tm, vw), lambda bi, i: (bi, i, 0)),
        out_shape=jax.ShapeDtypeStruct((b, lp, vw), BF16),
        scratch_shapes=[pltpu.VMEM((GLA_HEADS, GLA_DV, GLA_DK), F32)],
        compiler_params=_params(("parallel", "arbitrary"), VMEM_MID),
        name="gla",
    )(cd3, cd3, cd3, cd3, gates3, wa, ba.reshape(1, -1), o_norm.reshape(1, -1))


def _ab_in_weight(w):
    wt = jnp.swapaxes(w, 1, 2)
    zeros = jnp.zeros((w.shape[0], 3840 - w.shape[2], w.shape[1]), w.dtype)
    return jnp.concatenate([wt, zeros], axis=1).astype(BF16)


CD_MAIN = 12288
CD_ROWS = CD_MAIN + PAD_TO


def _cd_in_weight(w):
    wt = jnp.swapaxes(w, 1, 2)
    pieces = [wt[:, :6144], wt[:, 6160:10256], wt[:, 10272:], wt[:, 6144:6160],
              wt[:, 10256:10272], jnp.zeros((w.shape[0], CD_ROWS - w.shape[2], w.shape[1]), w.dtype)]
    return jnp.concatenate(pieces, axis=1).astype(BF16)


def _mla_q_weight(w):
    n, r, _ = w.shape
    w = jnp.pad(w.astype(BF16).reshape(n, r, HEADS, MLA_QK),
                ((0, 0), (0, 0), (0, 0), (0, 2 * HEAD_DIM - MLA_QK)))
    return w.reshape(n, r, HEADS * 2 * HEAD_DIM)


def _gla_gate_weight(w):
    return jnp.pad(w.astype(BF16), ((0, 0), (GLA_RANK, PAD_TO - 2 * GLA_RANK), (0, 0)))


def _rotary_tables(lp):
    pos = jnp.arange(lp, dtype=F32)
    inv_freq = ROPE_BASE ** (-jnp.arange(0, MLA_ROPE, 2, dtype=F32) / MLA_ROPE)
    ang = pos[:, None] * inv_freq[None, :]
    cos, sin = jnp.cos(ang), jnp.sin(ang)
    z32, z64 = jnp.zeros_like(cos), jnp.zeros((lp, HEAD_DIM - MLA_ROPE), F32)
    return (jnp.concatenate([cos, cos, z64], axis=1),
            jnp.concatenate([-sin, z32, z64], axis=1),
            jnp.concatenate([z32, sin, z64], axis=1))


def _ffn(h, norm, w_gate, w_up, w_down, layer, name):
    hidden = _norm_matmul(h, norm, [w_gate, w_up], layer, _swiglu_epilogue, BF16, ROW_TILE, 512,
                          name + "_up")
    return _matmul_residual([hidden], w_down, layer, h, 0.5, RES_ROWS, RES_COLS, name + "_down")


def _pool_mla_layer(h, b, lp, norm, w_in, pool_w, pool_scale, q_norm, wq, kv_norm, wkv,
                    q_gain, k_gain, w_out, layer, tables):
    ab = _norm_matmul(h, norm, [w_in], layer, _identity, F32, ROW_TILE, 768, "ab_in",
                      w_transposed=True)
    y_pool = _pool_mixer(ab, pool_w, layer, pool_scale, lp)
    q, k, v = _mla_prep(ab.reshape(b, lp, -1), tables, q_norm, kv_norm, wq, wkv, layer,
                        q_gain, k_gain, MLA_QK ** -0.5 * LOG2E)
    y_mla = _mla_attention(q, k, v)
    return _matmul_residual([y_pool, y_mla.reshape(b * lp, -1)], w_out, layer, h, 1.0,
                            RES_ROWS, OUT_COLS, "ab_out")


def _fox_gla_layer(h, b, lp, norm, w_in, fox_q_gain, fox_k_gain, fox_f_bias, wa, gla_b_a,
                   gla_o_norm, w_out, layer):
    cd, gates = _norm_matmul(h, norm, [w_in], layer, _identity, F32, ROW_TILE, 768, "cd_in",
                             w_transposed=True, n=CD_MAIN, side_rows=PAD_TO)
    cd3 = cd.reshape(b, lp, -1)
    ff_t = gates[:, :HEADS].reshape(b, lp, HEADS).transpose(0, 2, 1).reshape(b * HEADS, lp)
    f = _forget_cumsum(ff_t, jnp.tile(fox_f_bias, b).reshape(b * HEADS, 1))
    y_fox = _fox_attention(cd3, f.reshape(b, HEADS, 1, lp), fox_q_gain, fox_k_gain,
                           HEAD_DIM ** -0.5 * LOG2E)
    y_gla = _gla(cd3, gates.reshape(b, lp, -1), wa, layer, gla_b_a, gla_o_norm)
    return _matmul_residual([y_fox.reshape(b * lp, -1), y_gla.reshape(b * lp, -1)], w_out, layer,
                            h, 1.0, RES_ROWS, OUT_COLS, "cd_out")


def kernel(x, meta_tokens, ffn1_norm, ffn1_w_gate, ffn1_w_up, ffn1_w_down, mix_norm, ffn2_norm, ffn2_w_gate, ffn2_w_up, ffn2_w_down, ab_w_in, pool_w, pool_scale, mla_q_norm, mla_w_q_up, mla_kv_norm, mla_w_kv_up, mla_q_gain, mla_k_gain, ab_w_out, cd_w_in, fox_q_gain, fox_k_gain, fox_f_bias, gla_w_a2, gla_b_a, gla_o_norm, cd_w_out):
    b, s, d = x.shape
    length = N_META + s
    lp = -(-length // PAD_TO) * PAD_TO
    meta = jnp.broadcast_to(meta_tokens.astype(x.dtype)[None], (b, N_META, d))
    h = jnp.concatenate([meta, x, jnp.zeros((b, lp - length, d), x.dtype)], axis=1)
    h = h.reshape(b * lp, d)
    tables = _rotary_tables(lp)

    ffn1 = [w.astype(BF16) for w in (ffn1_w_gate, ffn1_w_up, ffn1_w_down)]
    ffn2 = [w.astype(BF16) for w in (ffn2_w_gate, ffn2_w_up, ffn2_w_down)]
    ab_in, cd_in = _ab_in_weight(ab_w_in), _cd_in_weight(cd_w_in)
    ab_out, cd_out = ab_w_out.astype(BF16), cd_w_out.astype(BF16)
    pool_wb, wq, wkv = pool_w.astype(BF16), _mla_q_weight(mla_w_q_up), mla_w_kv_up.astype(BF16)
    wa = _gla_gate_weight(gla_w_a2)

    for layer in range(ffn1_norm.shape[0]):
        i = layer // 2
        h = _ffn(h, ffn1_norm[layer], *ffn1, layer, "ffn1")
        if layer % 2 == 0:
            h = _pool_mla_layer(h, b, lp, mix_norm[layer], ab_in, pool_wb, pool_scale[i],
                                mla_q_norm[i], wq, mla_kv_norm[i], wkv, mla_q_gain[i],
                                mla_k_gain[i], ab_out, i, tables)
        else:
            h = _fox_gla_layer(h, b, lp, mix_norm[layer], cd_in, fox_q_gain[i], fox_k_gain[i],
                               fox_f_bias[i], wa, gla_b_a[i], gla_o_norm[i], cd_out, i)
        h = _ffn(h, ffn2_norm[layer], *ffn2, layer, "ffn2")
    return h.reshape(b, lp, d)[:, N_META:N_META + s]
```

```python
import functools
import math

import jax
import jax.numpy as jnp
from jax import lax
from jax.experimental import pallas as pl
from jax.experimental.pallas import tpu as pltpu

F32 = jnp.float32
BF16 = jnp.bfloat16

N_META = 16
PAD_TO = 128
RMS_EPS = 1e-6
ROPE_BASE = 10000.0
CHUNK = 64

POOL_WINDOWS = (2, 4, 8, 16)
POOL_GROUP = 512
HEADS = 16
HEAD_DIM = 128
MLA_ROPE = 64
MLA_QK = 192
GLA_HEADS = 4
GLA_DK = 256
GLA_DV = 512
GLA_TAU = 16.0
GLA_RANK = 16

VMEM_BIG = 61 * 1024 * 1024
VMEM_MID = 40 * 1024 * 1024

ROW_TILE = 768
RES_ROWS = 1056
OUT_COLS = 1024
DOWN_ROWS = 704
SEQ_TILE = 384
ATT_HEADS = 4
NEG_BIG = -1e30
LOG2E = math.log2(math.e)

NT_DIMS = (((1,), (1,)), ((), ()))
TN_DIMS = (((0,), (0,)), ((), ()))


def _silu(x):
    return x / (1.0 + jnp.exp(-x))


def _log_sigmoid(x):
    return jnp.minimum(x, 0.0) - jnp.log(1.0 + jnp.exp(-jnp.abs(x)))


def _rms(x, gain, n=None):
    n = x.shape[-1] if n is None else n
    ss = jnp.sum(x * x, axis=-1, keepdims=True)
    return x * lax.rsqrt(ss / n + RMS_EPS) * gain


def _params(sem, vmem):
    return pltpu.CompilerParams(dimension_semantics=sem, vmem_limit_bytes=vmem)


def _const_spec(shape, layer=None):
    if layer is None:
        nd = len(shape)
        return pl.BlockSpec(shape, lambda *_: (0,) * nd, pipeline_mode=pl.Buffered(1))
    nd = len(shape)
    return pl.BlockSpec((None,) + tuple(shape), lambda *_: (layer,) + (0,) * nd,
                        pipeline_mode=pl.Buffered(1))


def _norm_mm_kernel(a_ref, g_ref, *rest, n_w, epilogue, rows, w_transposed, has_side):
    if has_side:
        w_side_ref, rest = rest[0], rest[1:]
    w_refs, o_ref = rest[:n_w], rest[n_w]
    side_ref = rest[n_w + 1] if has_side else None
    xn_ref = rest[-1]
    dims = NT_DIMS if w_transposed else (((1,), (0,)), ((), ()))

    @pl.when(pl.program_id(1) == 0)
    def _():
        def body(c, carry):
            r0 = pl.multiple_of(c * rows, rows)
            a = a_ref[pl.ds(r0, rows), :]
            xn_ref[pl.ds(r0, rows), :] = _rms(a, g_ref[...]).astype(BF16)
            return carry
        lax.fori_loop(0, a_ref.shape[0] // rows, body, 0)
        if has_side:
            side_ref[...] = lax.dot_general(xn_ref[...], w_side_ref[...], NT_DIMS,
                                            preferred_element_type=F32)

    xn = xn_ref[...]
    outs = [lax.dot_general(xn, w[...], dims, preferred_element_type=F32) for w in w_refs]
    o_ref[...] = epilogue(*outs).astype(o_ref.dtype)


def _norm_matmul(a, gain, ws, layer, epilogue, out_dtype, tm, tn, name, w_transposed=False,
                 n=None, side_rows=0):
    t, k = a.shape
    n = ws[0].shape[1 if w_transposed else 2] if n is None else n
    has_side = side_rows > 0
    kern = functools.partial(_norm_mm_kernel, n_w=len(ws), epilogue=epilogue, rows=64,
                             w_transposed=w_transposed, has_side=has_side)
    if w_transposed:
        w_spec = pl.BlockSpec((None, tn, k), lambda i, j: (layer, j, 0))
    else:
        w_spec = pl.BlockSpec((None, k, tn), lambda i, j: (layer, 0, j))
    main_spec = pl.BlockSpec((tm, tn), lambda i, j: (i, j))
    main_shape = jax.ShapeDtypeStruct((t, n), out_dtype)
    side_in, side_args, out_specs, out_shape = [], [], main_spec, main_shape
    if has_side:
        assert w_transposed and n % side_rows == 0
        side_in = [pl.BlockSpec((None, side_rows, k), lambda i, j: (layer, n // side_rows, 0),
                                pipeline_mode=pl.Buffered(1))]
        side_args = [ws[0]]
        out_specs = [main_spec, pl.BlockSpec((tm, side_rows), lambda i, j: (i, 0))]
        out_shape = [main_shape, jax.ShapeDtypeStruct((t, side_rows), F32)]
    return pl.pallas_call(
        kern,
        grid=(t // tm, n // tn),
        in_specs=[pl.BlockSpec((tm, k), lambda i, j: (i, 0)),
                  pl.BlockSpec((1, k), lambda i, j: (0, 0))] + side_in + [w_spec for _ in ws],
        out_specs=out_specs,
        out_shape=out_shape,
        scratch_shapes=[pltpu.VMEM((tm, k), BF16)],
        compiler_params=_params(("parallel", "arbitrary"), VMEM_BIG),
        name=name,
    )(a, gain.reshape(1, k), *side_args, *ws)


def _swiglu_epilogue(g, u):
    return _silu(g) * u


def _identity(x):
    return x


def _mm_res_kernel(*refs, k_sizes, scale):
    n_a = len(k_sizes)
    a_refs, w_ref, r_ref, o_ref = refs[:n_a], refs[n_a], refs[n_a + 1], refs[n_a + 2]
    acc, off = None, 0
    for a_ref, ks in zip(a_refs, k_sizes):
        p = jnp.dot(a_ref[...], w_ref[off:off + ks, :], preferred_element_type=F32)
        acc = p if acc is None else acc + p
        off += ks
    if scale != 1.0:
        acc = scale * acc
    o_ref[...] = r_ref[...] + acc


def _matmul_residual(a_list, w, layer, res, scale, tm, tn, name):
    t, n = res.shape
    k_sizes = tuple(a.shape[1] for a in a_list)
    k = sum(k_sizes)
    kern = functools.partial(_mm_res_kernel, k_sizes=k_sizes, scale=scale)
    n_a = len(a_list)
    return pl.pallas_call(
        kern,
        grid=(t // tm, n // tn),
        in_specs=[pl.BlockSpec((tm, ks), lambda i, j: (i, 0)) for ks in k_sizes]
                 + [pl.BlockSpec((None, k, tn), lambda i, j: (layer, 0, j)),
                    pl.BlockSpec((tm, tn), lambda i, j: (i, j))],
        out_specs=pl.BlockSpec((tm, tn), lambda i, j: (i, j)),
        out_shape=jax.ShapeDtypeStruct((t, n), F32),
        input_output_aliases={n_a + 1: 0},
        compiler_params=_params(("parallel", "arbitrary"), VMEM_BIG),
        name=name,
    )(*a_list, w, res)


def _pool_kernel(x_ref, halo_ref, w_ref, s_ref, o_ref, xs_ref, *, tiles_per_seq):
    tm = x_ref.shape[0]
    halo = POOL_WINDOWS[-1]
    it = pl.program_id(0) % tiles_per_seq
    keep = (it > 0).astype(F32)
    xs_ref[0:halo, :] = halo_ref[...] * keep
    xs_ref[halo:halo + tm, :] = x_ref[...]
    pos = it * tm + lax.broadcasted_iota(jnp.int32, (tm, 1), 0)
    for g, win in enumerate(POOL_WINDOWS):
        c0 = g * POOL_GROUP
        x = xs_ref[halo:halo + tm, c0:c0 + POOL_GROUP]
        acc = x
        for j in range(1, win):
            acc = acc + xs_ref[halo - j:halo - j + tm, c0:c0 + POOL_GROUP]
        cnt = jnp.minimum(pos + 1, win).astype(F32)
        d = acc / cnt - x
        y = jnp.dot(d.astype(BF16), w_ref[g], preferred_element_type=F32)
        o_ref[:, c0:c0 + POOL_GROUP] = (y * s_ref[:, c0:c0 + POOL_GROUP]).astype(o_ref.dtype)


def _pool_mixer(ab, pool_w, layer, pool_scale, lp):
    t = ab.shape[0]
    tm, halo = SEQ_TILE, POOL_WINDOWS[-1]
    width = len(POOL_WINDOWS) * POOL_GROUP
    per_halo = tm // halo
    kern = functools.partial(_pool_kernel, tiles_per_seq=lp // tm)
    return pl.pallas_call(
        kern,
        grid=(t // tm,),
        in_specs=[pl.BlockSpec((tm, width), lambda i: (i, 0)),
                  pl.BlockSpec((halo, width), lambda i: (jnp.maximum(i * per_halo - 1, 0), 0)),
                  _const_spec((len(POOL_WINDOWS), POOL_GROUP, POOL_GROUP), layer),
                  _const_spec((1, width))],
        out_specs=pl.BlockSpec((tm, width), lambda i: (i, 0)),
        out_shape=jax.ShapeDtypeStruct((t, width), BF16),
        scratch_shapes=[pltpu.VMEM((tm + halo, width), F32)],
        compiler_params=_params(("parallel",), VMEM_MID),
        name="pool_mixer",
    )(ab, ab, pool_w, pool_scale.reshape(1, width))


def _rotary_slot(x, c_ref, s1_ref, s2_ref):
    return (x * c_ref[...] + pltpu.roll(x, 96, axis=1) * s1_ref[...]
            + pltpu.roll(x, 32, axis=1) * s2_ref[...])


def _mla_prep_kernel(cq_ref, ckv_ref, kpe_ref, c_ref, s1_ref, s2_ref, qn_ref, kvn_ref,
                     wq_ref, wkv_ref, qgn_ref, qgr_ref, kgn_ref, kgr_ref,
                     q_ref, k_ref, v_ref, *, q_scale):
    cq = _rms(cq_ref[...], qn_ref[...]).astype(BF16)
    ckv = _rms(ckv_ref[...], kvn_ref[...]).astype(BF16)
    kr = _rotary_slot(_rms(kpe_ref[...], kgr_ref[...], MLA_ROPE), c_ref, s1_ref, s2_ref).astype(BF16)
    for h in range(HEADS):
        lo = h * 2 * HEAD_DIM
        mid, hi = lo + HEAD_DIM, lo + 2 * HEAD_DIM
        q = jnp.dot(cq, wq_ref[:, lo:hi], preferred_element_type=F32)
        qn = _rms(q[:, :HEAD_DIM], qgn_ref[...])
        qr = _rotary_slot(_rms(q[:, HEAD_DIM:], qgr_ref[...], MLA_ROPE), c_ref, s1_ref, s2_ref)
        q_ref[:, lo:mid] = (qn * q_scale).astype(BF16)
        q_ref[:, mid:hi] = (qr * q_scale).astype(BF16)
        kv = jnp.dot(ckv, wkv_ref[:, lo:hi], preferred_element_type=F32)
        k_ref[:, lo:mid] = _rms(kv[:, :HEAD_DIM], kgn_ref[...]).astype(BF16)
        k_ref[:, mid:hi] = kr
        v_ref[:, h * HEAD_DIM:(h + 1) * HEAD_DIM] = kv[:, HEAD_DIM:].astype(BF16)


def _mla_prep(ab3, tables, q_norm, kv_norm, wq, wkv, layer, q_gain, k_gain, q_scale):
    b, lp, _ = ab3.shape
    tm = SEQ_TILE
    q_rank, kv_rank = wq.shape[1], wkv.shape[1]
    cat = HEADS * 2 * HEAD_DIM
    zeros = jnp.zeros((HEAD_DIM - MLA_ROPE,), F32)
    slot = lambda g: jnp.concatenate([g[HEAD_DIM:], zeros]).reshape(1, HEAD_DIM)
    row = lambda i_, j_: (j_, 0)
    out = jax.ShapeDtypeStruct((b, lp, cat), BF16)
    return pl.pallas_call(
        functools.partial(_mla_prep_kernel, q_scale=q_scale),
        grid=(b, lp // tm),
        in_specs=[pl.BlockSpec((None, tm, q_rank), lambda i, j: (i, j, 2048 // q_rank)),
                  pl.BlockSpec((None, tm, kv_rank), lambda i, j: (i, j, 3072 // kv_rank)),
                  pl.BlockSpec((None, tm, HEAD_DIM), lambda i, j: (i, j, 3584 // HEAD_DIM)),
                  pl.BlockSpec((tm, HEAD_DIM), row),
                  pl.BlockSpec((tm, HEAD_DIM), row),
                  pl.BlockSpec((tm, HEAD_DIM), row),
                  _const_spec((1, q_rank)), _const_spec((1, kv_rank)),
                  _const_spec((q_rank, cat), layer), _const_spec((kv_rank, cat), layer),
                  _const_spec((1, HEAD_DIM)), _const_spec((1, HEAD_DIM)),
                  _const_spec((1, HEAD_DIM)), _const_spec((1, HEAD_DIM))],
        out_specs=[pl.BlockSpec((None, tm, cat), lambda i, j: (i, j, 0))] * 2
                  + [pl.BlockSpec((None, tm, HEADS * HEAD_DIM), lambda i, j: (i, j, 0))],
        out_shape=[out, out, jax.ShapeDtypeStruct((b, lp, HEADS * HEAD_DIM), BF16)],
        compiler_params=_params(("parallel", "parallel"), VMEM_BIG),
        name="mla_prep",
    )(ab3, ab3, ab3, *tables, q_norm.reshape(1, -1), kv_norm.reshape(1, -1), wq, wkv,
      q_gain[:HEAD_DIM].reshape(1, HEAD_DIM), slot(q_gain),
      k_gain[:HEAD_DIM].reshape(1, HEAD_DIM), slot(k_gain))


def _softmax_sweep(qs, key_fn, val_fn, bias_fn, n_full, tq, diag_mask, tail, scratch):
    nh = len(qs)
    s_even, s_odd, m_ref, l_ref, acc_ref = scratch[:5]
    tail_ref = scratch[5] if tail is not None else None

    def scores(h, start, width):
        s = lax.dot_general(qs[h], key_fn(h, start, width), NT_DIMS, preferred_element_type=F32)
        return bias_fn(h, s, start, width)

    def lanes(x, width):
        return jnp.concatenate([x] * (width // PAD_TO), axis=1)

    def absorb(h, s, vals):
        m = m_ref[h]
        m_new = jnp.maximum(m, jnp.max(s, axis=-1, keepdims=True))
        alpha = jnp.exp2(m - m_new)
        p = jnp.exp2(s - lanes(m_new, s.shape[1]))
        p_sum = p[:, :PAD_TO]
        for c in range(1, s.shape[1] // PAD_TO):
            p_sum = p_sum + p[:, c * PAD_TO:(c + 1) * PAD_TO]
        l_ref[h] = alpha * l_ref[h] + p_sum
        acc_ref[h] = alpha * acc_ref[h] + jnp.dot(p.astype(BF16), vals, preferred_element_type=F32)
        m_ref[h] = m_new

    def half_step(c, cur, nxt, diagonal):
        start = pl.multiple_of(c * tq, tq)
        for h in range(nh):
            if not diagonal:
                nxt[h] = scores(h, start + tq, tq)
                absorb(h, cur[h], val_fn(h, start, tq))
                continue
            s = jnp.where(diag_mask(), cur[h], NEG_BIG)
            vals = val_fn(h, start, tq)
            if tail is not None:
                t_start, t_width, t_mask = tail
                s_tail = jnp.where(t_mask(), tail_ref[h], NEG_BIG)
                s = jnp.concatenate([s, s_tail], axis=1)
                vals = jnp.concatenate([vals, val_fn(h, t_start, t_width)], axis=0)
            absorb(h, s, vals)

    def by_parity(c, diagonal):
        return lambda: lax.cond(c % 2 == 0, lambda: half_step(c, s_even, s_odd, diagonal),
                                lambda: half_step(c, s_odd, s_even, diagonal))

    def step(c, carry):
        lax.cond(c == n_full, by_parity(c, True), by_parity(c, False))
        return carry

    for h in range(nh):
        m_ref[h] = jnp.full((tq, PAD_TO), NEG_BIG, F32)
        l_ref[h] = jnp.zeros((tq, PAD_TO), F32)
        acc_ref[h] = jnp.zeros((tq, HEAD_DIM), F32)
        s_even[h] = scores(h, 0, tq)
        if tail is not None:
            tail_ref[h] = scores(h, tail[0], tail[1])
    lax.fori_loop(0, n_full + 1, step, 0)
    return [acc_ref[h] / jnp.sum(l_ref[h], axis=-1, keepdims=True) for h in range(nh)]


def _sweep_scratch(tq, tail_width=0):
    bufs = [pltpu.VMEM((ATT_HEADS, tq, tq), F32), pltpu.VMEM((ATT_HEADS, tq, tq), F32),
            pltpu.VMEM((ATT_HEADS, tq, PAD_TO), F32), pltpu.VMEM((ATT_HEADS, tq, PAD_TO), F32),
            pltpu.VMEM((ATT_HEADS, tq, HEAD_DIM), F32)]
    if tail_width:
        bufs.append(pltpu.VMEM((ATT_HEADS, tq, tail_width), F32))
    return bufs


def _mla_attn_kernel(q_ref, k_ref, v_ref, o_ref, *scratch):
    tq, lk = q_ref.shape[0], k_ref.shape[0]
    width = 2 * HEAD_DIM
    i = pl.program_id(2)
    q0 = i * tq
    shift = CHUNK - N_META

    def chunk_mask(k_start, k_width):
        bits = CHUNK.bit_length() - 1
        qc = lax.shift_right_logical(
            q0 + shift + lax.broadcasted_iota(jnp.int32, (tq, k_width), 0), bits)
        kc = lax.shift_right_logical(
            k_start + shift + lax.broadcasted_iota(jnp.int32, (tq, k_width), 1), bits)
        return kc <= qc

    t_start = pl.multiple_of(jnp.minimum(q0 + tq, lk - PAD_TO), PAD_TO)

    def t_mask():
        t_kpos = t_start + lax.broadcasted_iota(jnp.int32, (tq, PAD_TO), 1)
        return (t_kpos >= q0 + tq) & chunk_mask(t_start, PAD_TO)

    qs = [q_ref[:, h * width:(h + 1) * width] for h in range(ATT_HEADS)]
    outs = _softmax_sweep(
        qs,
        lambda h, s0, w: k_ref[pl.ds(s0, w), h * width:(h + 1) * width],
        lambda h, s0, w: v_ref[pl.ds(s0, w), h * HEAD_DIM:(h + 1) * HEAD_DIM],
        lambda h, s, s0, w: s,
        i, tq, lambda: chunk_mask(q0, tq), (t_start, PAD_TO, t_mask), scratch)
    for h in range(ATT_HEADS):
        o_ref[:, h * HEAD_DIM:(h + 1) * HEAD_DIM] = outs[h].astype(o_ref.dtype)


def _mla_attention(q, k, v):
    b, lp, _ = q.shape
    tq = SEQ_TILE
    gw = ATT_HEADS * 2 * HEAD_DIM
    return pl.pallas_call(
        _mla_attn_kernel,
        grid=(b, HEADS // ATT_HEADS, lp // tq),
        in_specs=[pl.BlockSpec((None, tq, gw), lambda bi, g, i: (bi, i, g)),
                  pl.BlockSpec((None, lp, gw), lambda bi, g, i: (bi, 0, g)),
                  pl.BlockSpec((None, lp, ATT_HEADS * HEAD_DIM), lambda bi, g, i: (bi, 0, g))],
        out_specs=pl.BlockSpec((None, tq, ATT_HEADS * HEAD_DIM), lambda bi, g, i: (bi, i, g)),
        out_shape=jax.ShapeDtypeStruct((b, lp, HEADS * HEAD_DIM), BF16),
        scratch_shapes=_sweep_scratch(tq, PAD_TO),
        compiler_params=_params(("parallel", "parallel", "arbitrary"), VMEM_BIG),
        name="mla_attention",
    )(q, k, v)


def _fox_attn_kernel(q_ref, k_ref, v_ref, f_ref, qg_ref, kg_ref, o_ref, kn_ref, va_ref, fb_ref,
                     *scratch, q_scale):
    tq, lk = q_ref.shape[0], k_ref.shape[0]
    i = pl.program_id(2)
    q0 = pl.multiple_of(i * tq, tq)
    heads = [slice(h * HEAD_DIM, (h + 1) * HEAD_DIM) for h in range(ATT_HEADS)]

    @pl.when(i == 0)
    def _():
        def body(c, carry):
            rows = pl.ds(pl.multiple_of(c * tq, tq), tq)
            for sl in heads:
                kn_ref[rows, sl] = _rms(k_ref[rows, sl], kg_ref[...]).astype(BF16)
            va_ref[rows, :] = v_ref[rows, :].astype(BF16)
            return carry
        lax.fori_loop(0, lk // tq, body, 0)
        fb_ref[...] = f_ref[...] * LOG2E

    qs = [(_rms(q_ref[:, sl], qg_ref[...]) * q_scale).astype(BF16) for sl in heads]
    f0 = [jnp.max(fb_ref[h, :, pl.ds(q0, tq)], axis=-1, keepdims=True) for h in range(ATT_HEADS)]

    def causal():
        return (lax.broadcasted_iota(jnp.int32, (tq, tq), 1)
                <= lax.broadcasted_iota(jnp.int32, (tq, tq), 0))

    outs = _softmax_sweep(
        qs,
        lambda h, s0, w: kn_ref[pl.ds(s0, w), heads[h]],
        lambda h, s0, w: va_ref[pl.ds(s0, w), heads[h]],
        lambda h, s, s0, w: s - (fb_ref[h, :, pl.ds(s0, w)] - f0[h]),
        i, tq, causal, None, scratch)
    for h, sl in enumerate(heads):
        o_ref[:, sl] = outs[h].astype(o_ref.dtype)


def _fox_attention(cd3, f, q_gain, k_gain, q_scale):
    b, lp, _ = cd3.shape
    tq = SEQ_TILE
    gw = ATT_HEADS * HEAD_DIM
    groups = HEADS // ATT_HEADS
    kern = functools.partial(_fox_attn_kernel, q_scale=q_scale)
    return pl.pallas_call(
        kern,
        grid=(b, groups, lp // tq),
        in_specs=[pl.BlockSpec((None, tq, gw), lambda bi, g, i: (bi, i, g)),
                  pl.BlockSpec((None, lp, gw), lambda bi, g, i: (bi, 0, groups + g)),
                  pl.BlockSpec((None, lp, gw), lambda bi, g, i: (bi, 0, 2 * groups + g)),
                  pl.BlockSpec((None, ATT_HEADS, 1, lp), lambda bi, g, i: (bi, g, 0, 0)),
                  _const_spec((1, HEAD_DIM)), _const_spec((1, HEAD_DIM))],
        out_specs=pl.BlockSpec((None, tq, gw), lambda bi, g, i: (bi, i, g)),
        out_shape=jax.ShapeDtypeStruct((b, lp, HEADS * HEAD_DIM), BF16),
        scratch_shapes=[pltpu.VMEM((lp, gw), BF16),
                        pltpu.VMEM((lp, gw), BF16),
                        pltpu.VMEM((ATT_HEADS, 1, lp), F32)] + _sweep_scratch(tq),
        compiler_params=_params(("parallel", "parallel", "arbitrary"), VMEM_BIG),
        name="fox_attention",
    )(cd3, cd3, cd3, f, q_gain.reshape(1, HEAD_DIM), k_gain.reshape(1, HEAD_DIM))


def _forget_cumsum_kernel(x_ref, b_ref, o_ref):
    rows, lp = x_ref.shape
    lane = lax.broadcasted_iota(jnp.int32, (rows, PAD_TO), 1)
    carry = jnp.zeros((rows, 1), F32)
    for c in range(lp // PAD_TO):
        sl = slice(c * PAD_TO, (c + 1) * PAD_TO)
        x = _log_sigmoid(x_ref[:, sl] + b_ref[...])
        step = 1
        while step < PAD_TO:
            x = x + jnp.where(lane >= step, pltpu.roll(x, step, axis=1), 0.0)
            step *= 2
        x = x + carry
        o_ref[:, sl] = x
        carry = x[:, PAD_TO - 1:PAD_TO]


def _forget_cumsum(ff_t, bias_col):
    rows, lp = ff_t.shape
    return pl.pallas_call(
        _forget_cumsum_kernel,
        out_shape=jax.ShapeDtypeStruct((rows, lp), F32),
        name="fox_forget_cumsum",
    )(ff_t, bias_col)


def _cumsum_rows(x):
    row = lax.broadcasted_iota(jnp.int32, x.shape, 0)
    step = 1
    while step < x.shape[0]:
        x = x + jnp.where(row >= step, pltpu.roll(x, step, axis=0), 0.0)
        step *= 2
    return x


def _gla_kernel(q_ref, k_ref, v_ref, r_ref, a_ref, wa_ref, ba_ref, on_ref, o_ref, st_ref):
    tm = q_ref.shape[0]

    @pl.when(pl.program_id(1) == 0)
    def _():
        st_ref[...] = jnp.zeros_like(st_ref)

    causal = (lax.broadcasted_iota(jnp.int32, (CHUNK, CHUNK), 1)
              <= lax.broadcasted_iota(jnp.int32, (CHUNK, CHUNK), 0))

    def body(c, carry):
        r0 = pl.multiple_of(c * CHUNK, CHUNK)
        rows = pl.ds(r0, CHUNK)
        gate = jnp.dot(a_ref[rows, :].astype(BF16), wa_ref[...], preferred_element_type=F32)
        for h in range(GLA_HEADS):
            ks = slice(h * GLA_DK, (h + 1) * GLA_DK)
            vs = slice(h * GLA_DV, (h + 1) * GLA_DV)
            g = _log_sigmoid(gate[:, ks] + ba_ref[:, ks]) / GLA_TAU
            bcum = _cumsum_rows(g)
            b_last = bcum[CHUNK - 1:CHUNK, :]
            k = k_ref[rows, ks]
            v = v_ref[rows, vs].astype(BF16)
            q_dec = ((q_ref[rows, ks] * (GLA_DK ** -0.5)) * jnp.exp(bcum)).astype(BF16)
            k_inv = (k * jnp.exp(-bcum)).astype(BF16)
            k_end = (k * jnp.exp(b_last - bcum)).astype(BF16)
            a = lax.dot_general(q_dec, k_inv, NT_DIMS, preferred_element_type=F32)
            a = jnp.where(causal, a, 0.0).astype(BF16)
            st = st_ref[h]
            o = (jnp.dot(a, v, preferred_element_type=F32)
                 + lax.dot_general(q_dec, st.astype(BF16), NT_DIMS, preferred_element_type=F32))
            st_ref[h] = st * jnp.exp(b_last) + lax.dot_general(v, k_end, TN_DIMS,
                                                               preferred_element_type=F32)
            o = _rms(o, on_ref[...])
            o_ref[rows, vs] = (o * _silu(r_ref[rows, vs])).astype(o_ref.dtype)
        return carry

    lax.fori_loop(0, tm // CHUNK, body, 0)


def _gla(cd3, gates3, wa, layer, ba, o_norm):
    b, lp, _ = cd3.shape
    tm = SEQ_TILE
    kw, vw = GLA_HEADS * GLA_DK, GLA_HEADS * GLA_DV
    return pl.pallas_call(
        _gla_kernel,
        grid=(b, lp // tm),
        in_specs=[pl.BlockSpec((None, tm, kw), lambda bi, i: (bi, i, 6144 // kw)),
                  pl.BlockSpec((None, tm, kw), lambda bi, i: (bi, i, 7168 // kw)),
                  pl.BlockSpec((None, tm, vw), lambda bi, i: (bi, i, 8192 // vw)),
                  pl.BlockSpec((None, tm, vw), lambda bi, i: (bi, i, 10240 // vw)),
                  pl.BlockSpec((None, tm, PAD_TO), lambda bi, i: (bi, i, 0)),
                  _const_spec((PAD_TO, kw), layer), _const_spec((1, kw)), _const_spec((1, GLA_DV))],
        out_specs=pl.BlockSpec((None, tm, vw), lambda bi, i: (bi, i, 0)),
        out_shape=jax.ShapeDtypeStruct((b, lp, vw), BF16),
        scratch_shapes=[pltpu.VMEM((GLA_HEADS, GLA_DV, GLA_DK), F32)],
        compiler_params=_params(("parallel", "arbitrary"), VMEM_MID),
        name="gla",
    )(cd3, cd3, cd3, cd3, gates3, wa, ba.reshape(1, -1), o_norm.reshape(1, -1))


def _ab_in_weight(w):
    wt = jnp.swapaxes(w, 1, 2)
    zeros = jnp.zeros((w.shape[0], 3840 - w.shape[2], w.shape[1]), w.dtype)
    return jnp.concatenate([wt, zeros], axis=1).astype(BF16)


CD_MAIN = 12288
CD_ROWS = CD_MAIN + PAD_TO


def _cd_in_weight(w):
    wt = jnp.swapaxes(w, 1, 2)
    pieces = [wt[:, :6144], wt[:, 6160:10256], wt[:, 10272:], wt[:, 6144:6160],
              wt[:, 10256:10272], jnp.zeros((w.shape[0], CD_ROWS - w.shape[2], w.shape[1]), w.dtype)]
    return jnp.concatenate(pieces, axis=1).astype(BF16)


def _mla_q_weight(w):
    n, r, _ = w.shape
    w = jnp.pad(w.astype(BF16).reshape(n, r, HEADS, MLA_QK),
                ((0, 0), (0, 0), (0, 0), (0, 2 * HEAD_DIM - MLA_QK)))
    return w.reshape(n, r, HEADS * 2 * HEAD_DIM)


def _gla_gate_weight(w):
    return jnp.pad(w.astype(BF16), ((0, 0), (GLA_RANK, PAD_TO - 2 * GLA_RANK), (0, 0)))


def _rotary_tables(lp):
    pos = jnp.arange(lp, dtype=F32)
    inv_freq = ROPE_BASE ** (-jnp.arange(0, MLA_ROPE, 2, dtype=F32) / MLA_ROPE)
    ang = pos[:, None] * inv_freq[None, :]
    cos, sin = jnp.cos(ang), jnp.sin(ang)
    z32, z64 = jnp.zeros_like(cos), jnp.zeros((lp, HEAD_DIM - MLA_ROPE), F32)
    return (jnp.concatenate([cos, cos, z64], axis=1),
            jnp.concatenate([-sin, z32, z64], axis=1),
            jnp.concatenate([z32, sin, z64], axis=1))


def _ffn(h, norm, w_gate, w_up, w_down, layer, name):
    hidden = _norm_matmul(h, norm, [w_gate, w_up], layer, _swiglu_epilogue, BF16, ROW_TILE, 512,
                          name + "_up")
    return _matmul_residual([hidden], w_down, layer, h, 0.5, DOWN_ROWS, OUT_COLS, name + "_down")


def _pool_mla_layer(h, b, lp, norm, w_in, pool_w, pool_scale, q_norm, wq, kv_norm, wkv,
                    q_gain, k_gain, w_out, layer, tables):
    ab = _norm_matmul(h, norm, [w_in], layer, _identity, F32, ROW_TILE, 768, "ab_in",
                      w_transposed=True)
    y_pool = _pool_mixer(ab, pool_w, layer, pool_scale, lp)
    q, k, v = _mla_prep(ab.reshape(b, lp, -1), tables, q_norm, kv_norm, wq, wkv, layer,
                        q_gain, k_gain, MLA_QK ** -0.5 * LOG2E)
    y_mla = _mla_attention(q, k, v)
    return _matmul_residual([y_pool, y_mla.reshape(b * lp, -1)], w_out, layer, h, 1.0,
                            RES_ROWS, OUT_COLS, "ab_out")


def _fox_gla_layer(h, b, lp, norm, w_in, fox_q_gain, fox_k_gain, fox_f_bias, wa, gla_b_a,
                   gla_o_norm, w_out, layer):
    cd, gates = _norm_matmul(h, norm, [w_in], layer, _identity, F32, ROW_TILE, 768, "cd_in",
                             w_transposed=True, n=CD_MAIN, side_rows=PAD_TO)
    cd3 = cd.reshape(b, lp, -1)
    ff_t = gates[:, :HEADS].reshape(b, lp, HEADS).transpose(0, 2, 1).reshape(b * HEADS, lp)
    f = _forget_cumsum(ff_t, jnp.tile(fox_f_bias, b).reshape(b * HEADS, 1))
    y_fox = _fox_attention(cd3, f.reshape(b, HEADS, 1, lp), fox_q_gain, fox_k_gain,
                           HEAD_DIM ** -0.5 * LOG2E)
    y_gla = _gla(cd3, gates.reshape(b, lp, -1), wa, layer, gla_b_a, gla_o_norm)
    return _matmul_residual([y_fox.reshape(b * lp, -1), y_gla.reshape(b * lp, -1)], w_out, layer,
                            h, 1.0, RES_ROWS, OUT_COLS, "cd_out")


def kernel(x, meta_tokens, ffn1_norm, ffn1_w_gate, ffn1_w_up, ffn1_w_down, mix_norm, ffn2_norm, ffn2_w_gate, ffn2_w_up, ffn2_w_down, ab_w_in, pool_w, pool_scale, mla_q_norm, mla_w_q_up, mla_kv_norm, mla_w_kv_up, mla_q_gain, mla_k_gain, ab_w_out, cd_w_in, fox_q_gain, fox_k_gain, fox_f_bias, gla_w_a2, gla_b_a, gla_o_norm, cd_w_out):
    b, s, d = x.shape
    length = N_META + s
    lp = -(-length // PAD_TO) * PAD_TO
    meta = jnp.broadcast_to(meta_tokens.astype(x.dtype)[None], (b, N_META, d))
    h = jnp.concatenate([meta, x, jnp.zeros((b, lp - length, d), x.dtype)], axis=1)
    h = h.reshape(b * lp, d)
    tables = _rotary_tables(lp)

    ffn1 = [w.astype(BF16) for w in (ffn1_w_gate, ffn1_w_up, ffn1_w_down)]
    ffn2 = [w.astype(BF16) for w in (ffn2_w_gate, ffn2_w_up, ffn2_w_down)]
    ab_in, cd_in = _ab_in_weight(ab_w_in), _cd_in_weight(cd_w_in)
    ab_out, cd_out = ab_w_out.astype(BF16), cd_w_out.astype(BF16)
    pool_wb, wq, wkv = pool_w.astype(BF16), _mla_q_weight(mla_w_q_up), mla_w_kv_up.astype(BF16)
    wa = _gla_gate_weight(gla_w_a2)

    for layer in range(ffn1_norm.shape[0]):
        i = layer // 2
        h = _ffn(h, ffn1_norm[layer], *ffn1, layer, "ffn1")
        if layer % 2 == 0:
            h = _pool_mla_layer(h, b, lp, mix_norm[layer], ab_in, pool_wb, pool_scale[i],
                                mla_q_norm[i], wq, mla_kv_norm[i], wkv, mla_q_gain[i],
                                mla_k_gain[i], ab_out, i, tables)
        else:
            h = _fox_gla_layer(h, b, lp, mix_norm[layer], cd_in, fox_q_gain[i], fox_k_gain[i],
                               fox_f_bias[i], wa, gla_b_a[i], gla_o_norm[i], cd_out, i)
        h = _ffn(h, ffn2_norm[layer], *ffn2, layer, "ffn2")
    return h.reshape(b, lp, d)[:, N_META:N_META + s]
```

```python
import functools
import math

import jax
import jax.numpy as jnp
from jax import lax
from jax.experimental import pallas as pl
from jax.experimental.pallas import tpu as pltpu

F32 = jnp.float32
BF16 = jnp.bfloat16

N_META = 16
PAD_TO = 128
RMS_EPS = 1e-6
ROPE_BASE = 10000.0
CHUNK = 64

POOL_WINDOWS = (2, 4, 8, 16)
POOL_GROUP = 512
HEADS = 16
HEAD_DIM = 128
MLA_ROPE = 64
MLA_QK = 192
GLA_HEADS = 4
GLA_DK = 256
GLA_DV = 512
GLA_TAU = 16.0
GLA_RANK = 16

VMEM_BIG = 61 * 1024 * 1024
VMEM_MID = 40 * 1024 * 1024

ROW_TILE = 768
RES_ROWS = 1056
OUT_COLS = 1024
DOWN_ROWS = 704
SEQ_TILE = 384
ATT_HEADS = 4
NEG_BIG = -1e30
LOG2E = math.log2(math.e)

NT_DIMS = (((1,), (1,)), ((), ()))
TN_DIMS = (((0,), (0,)), ((), ()))


def _silu(x):
    return x / (1.0 + jnp.exp(-x))


def _log_sigmoid(x):
    return jnp.minimum(x, 0.0) - jnp.log(1.0 + jnp.exp(-jnp.abs(x)))


def _rms(x, gain, n=None):
    n = x.shape[-1] if n is None else n
    ss = jnp.sum(x * x, axis=-1, keepdims=True)
    return x * lax.rsqrt(ss / n + RMS_EPS) * gain


def _params(sem, vmem):
    return pltpu.CompilerParams(dimension_semantics=sem, vmem_limit_bytes=vmem)


def _const_spec(shape, layer=None):
    if layer is None:
        nd = len(shape)
        return pl.BlockSpec(shape, lambda *_: (0,) * nd, pipeline_mode=pl.Buffered(1))
    nd = len(shape)
    return pl.BlockSpec((None,) + tuple(shape), lambda *_: (layer,) + (0,) * nd,
                        pipeline_mode=pl.Buffered(1))


def _norm_mm_kernel(a_ref, g_ref, *rest, n_w, epilogue, rows, w_transposed, has_side):
    if has_side:
        w_side_ref, rest = rest[0], rest[1:]
    w_refs, o_ref = rest[:n_w], rest[n_w]
    side_ref = rest[n_w + 1] if has_side else None
    xn_ref = rest[-1]
    dims = NT_DIMS if w_transposed else (((1,), (0,)), ((), ()))

    @pl.when(pl.program_id(1) == 0)
    def _():
        def body(c, carry):
            r0 = pl.multiple_of(c * rows, rows)
            a = a_ref[pl.ds(r0, rows), :]
            xn_ref[pl.ds(r0, rows), :] = _rms(a, g_ref[...]).astype(BF16)
            return carry
        lax.fori_loop(0, a_ref.shape[0] // rows, body, 0)
        if has_side:
            side_ref[...] = lax.dot_general(xn_ref[...], w_side_ref[...], NT_DIMS,
                                            preferred_element_type=F32)

    xn = xn_ref[...]
    outs = [lax.dot_general(xn, w[...], dims, preferred_element_type=F32) for w in w_refs]
    o_ref[...] = epilogue(*outs).astype(o_ref.dtype)


def _norm_matmul(a, gain, ws, layer, epilogue, out_dtype, tm, tn, name, w_transposed=False,
                 n=None, side_rows=0):
    t, k = a.shape
    n = ws[0].shape[1 if w_transposed else 2] if n is None else n
    has_side = side_rows > 0
    kern = functools.partial(_norm_mm_kernel, n_w=len(ws), epilogue=epilogue, rows=64,
                             w_transposed=w_transposed, has_side=has_side)
    if w_transposed:
        w_spec = pl.BlockSpec((None, tn, k), lambda i, j: (layer, j, 0))
    else:
        w_spec = pl.BlockSpec((None, k, tn), lambda i, j: (layer, 0, j))
    main_spec = pl.BlockSpec((tm, tn), lambda i, j: (i, j))
    main_shape = jax.ShapeDtypeStruct((t, n), out_dtype)
    side_in, side_args, out_specs, out_shape = [], [], main_spec, main_shape
    if has_side:
        assert w_transposed and n % side_rows == 0
        side_in = [pl.BlockSpec((None, side_rows, k), lambda i, j: (layer, n // side_rows, 0),
                                pipeline_mode=pl.Buffered(1))]
        side_args = [ws[0]]
        out_specs = [main_spec, pl.BlockSpec((tm, side_rows), lambda i, j: (i, 0))]
        out_shape = [main_shape, jax.ShapeDtypeStruct((t, side_rows), F32)]
    return pl.pallas_call(
        kern,
        grid=(t // tm, n // tn),
        in_specs=[pl.BlockSpec((tm, k), lambda i, j: (i, 0)),
                  pl.BlockSpec((1, k), lambda i, j: (0, 0))] + side_in + [w_spec for _ in ws],
        out_specs=out_specs,
        out_shape=out_shape,
        scratch_shapes=[pltpu.VMEM((tm, k), BF16)],
        compiler_params=_params(("parallel", "arbitrary"), VMEM_BIG),
        name=name,
    )(a, gain.reshape(1, k), *side_args, *ws)


def _swiglu_epilogue(g, u):
    return _silu(g) * u


def _identity(x):
    return x


def _mm_res_kernel(*refs, k_sizes, scale):
    n_a = len(k_sizes)
    a_refs, w_ref, r_ref, o_ref = refs[:n_a], refs[n_a], refs[n_a + 1], refs[n_a + 2]
    acc, off = None, 0
    for a_ref, ks in zip(a_refs, k_sizes):
        p = jnp.dot(a_ref[...], w_ref[off:off + ks, :], preferred_element_type=F32)
        acc = p if acc is None else acc + p
        off += ks
    if scale != 1.0:
        acc = scale * acc
    o_ref[...] = r_ref[...] + acc


def _matmul_residual(a_list, w, layer, res, scale, tm, tn, name):
    t, n = res.shape
    k_sizes = tuple(a.shape[1] for a in a_list)
    k = sum(k_sizes)
    kern = functools.partial(_mm_res_kernel, k_sizes=k_sizes, scale=scale)
    n_a = len(a_list)
    return pl.pallas_call(
        kern,
        grid=(t // tm, n // tn),
        in_specs=[pl.BlockSpec((tm, ks), lambda i, j: (i, 0)) for ks in k_sizes]
                 + [pl.BlockSpec((None, k, tn), lambda i, j: (layer, 0, j)),
                    pl.BlockSpec((tm, tn), lambda i, j: (i, j))],
        out_specs=pl.BlockSpec((tm, tn), lambda i, j: (i, j)),
        out_shape=jax.ShapeDtypeStruct((t, n), F32),
        input_output_aliases={n_a + 1: 0},
        compiler_params=_params(("parallel", "arbitrary"), VMEM_BIG),
        name=name,
    )(*a_list, w, res)


def _pool_kernel(x_ref, halo_ref, w_ref, s_ref, o_ref, xs_ref, *, tiles_per_seq):
    tm = x_ref.shape[0]
    halo = POOL_WINDOWS[-1]
    it = pl.program_id(0) % tiles_per_seq
    keep = (it > 0).astype(F32)
    xs_ref[0:halo, :] = halo_ref[...] * keep
    xs_ref[halo:halo + tm, :] = x_ref[...]
    pos = it * tm + lax.broadcasted_iota(jnp.int32, (tm, 1), 0)
    for g, win in enumerate(POOL_WINDOWS):
        c0 = g * POOL_GROUP
        x = xs_ref[halo:halo + tm, c0:c0 + POOL_GROUP]
        acc = x
        for j in range(1, win):
            acc = acc + xs_ref[halo - j:halo - j + tm, c0:c0 + POOL_GROUP]
        cnt = jnp.minimum(pos + 1, win).astype(F32)
        d = acc / cnt - x
        y = jnp.dot(d.astype(BF16), w_ref[g], preferred_element_type=F32)
        o_ref[:, c0:c0 + POOL_GROUP] = (y * s_ref[:, c0:c0 + POOL_GROUP]).astype(o_ref.dtype)


def _pool_mixer(ab, pool_w, layer, pool_scale, lp):
    t = ab.shape[0]
    tm, halo = SEQ_TILE, POOL_WINDOWS[-1]
    width = len(POOL_WINDOWS) * POOL_GROUP
    per_halo = tm // halo
    kern = functools.partial(_pool_kernel, tiles_per_seq=lp // tm)
    return pl.pallas_call(
        kern,
        grid=(t // tm,),
        in_specs=[pl.BlockSpec((tm, width), lambda i: (i, 0)),
                  pl.BlockSpec((halo, width), lambda i: (jnp.maximum(i * per_halo - 1, 0), 0)),
                  _const_spec((len(POOL_WINDOWS), POOL_GROUP, POOL_GROUP), layer),
                  _const_spec((1, width))],
        out_specs=pl.BlockSpec((tm, width), lambda i: (i, 0)),
        out_shape=jax.ShapeDtypeStruct((t, width), BF16),
        scratch_shapes=[pltpu.VMEM((tm + halo, width), F32)],
        compiler_params=_params(("parallel",), VMEM_MID),
        name="pool_mixer",
    )(ab, ab, pool_w, pool_scale.reshape(1, width))


def _rotary_slot(x, c_ref, s1_ref, s2_ref):
    return (x * c_ref[...] + pltpu.roll(x, 96, axis=1) * s1_ref[...]
            + pltpu.roll(x, 32, axis=1) * s2_ref[...])


def _mla_prep_kernel(cq_ref, ckv_ref, kpe_ref, c_ref, s1_ref, s2_ref, qn_ref, kvn_ref,
                     wq_ref, wkv_ref, qgn_ref, qgr_ref, kgn_ref, kgr_ref,
                     q_ref, k_ref, v_ref, *, q_scale):
    cq = _rms(cq_ref[...], qn_ref[...]).astype(BF16)
    ckv = _rms(ckv_ref[...], kvn_ref[...]).astype(BF16)
    kr = _rotary_slot(_rms(kpe_ref[...], kgr_ref[...], MLA_ROPE), c_ref, s1_ref, s2_ref).astype(BF16)
    for h in range(HEADS):
        lo = h * 2 * HEAD_DIM
        mid, hi = lo + HEAD_DIM, lo + 2 * HEAD_DIM
        q = jnp.dot(cq, wq_ref[:, lo:hi], preferred_element_type=F32)
        qn = _rms(q[:, :HEAD_DIM], qgn_ref[...])
        qr = _rotary_slot(_rms(q[:, HEAD_DIM:], qgr_ref[...], MLA_ROPE), c_ref, s1_ref, s2_ref)
        q_ref[:, lo:mid] = (qn * q_scale).astype(BF16)
        q_ref[:, mid:hi] = (qr * q_scale).astype(BF16)
        kv = jnp.dot(ckv, wkv_ref[:, lo:hi], preferred_element_type=F32)
        k_ref[:, lo:mid] = _rms(kv[:, :HEAD_DIM], kgn_ref[...]).astype(BF16)
        k_ref[:, mid:hi] = kr
        v_ref[:, h * HEAD_DIM:(h + 1) * HEAD_DIM] = kv[:, HEAD_DIM:].astype(BF16)


def _mla_prep(ab3, tables, q_norm, kv_norm, wq, wkv, layer, q_gain, k_gain, q_scale):
    b, lp, _ = ab3.shape
    tm = SEQ_TILE
    q_rank, kv_rank = wq.shape[1], wkv.shape[1]
    cat = HEADS * 2 * HEAD_DIM
    zeros = jnp.zeros((HEAD_DIM - MLA_ROPE,), F32)
    slot = lambda g: jnp.concatenate([g[HEAD_DIM:], zeros]).reshape(1, HEAD_DIM)
    row = lambda i_, j_: (j_, 0)
    out = jax.ShapeDtypeStruct((b, lp, cat), BF16)
    return pl.pallas_call(
        functools.partial(_mla_prep_kernel, q_scale=q_scale),
        grid=(b, lp // tm),
        in_specs=[pl.BlockSpec((None, tm, q_rank), lambda i, j: (i, j, 2048 // q_rank)),
                  pl.BlockSpec((None, tm, kv_rank), lambda i, j: (i, j, 3072 // kv_rank)),
                  pl.BlockSpec((None, tm, HEAD_DIM), lambda i, j: (i, j, 3584 // HEAD_DIM)),
                  pl.BlockSpec((tm, HEAD_DIM), row),
                  pl.BlockSpec((tm, HEAD_DIM), row),
                  pl.BlockSpec((tm, HEAD_DIM), row),
                  _const_spec((1, q_rank)), _const_spec((1, kv_rank)),
                  _const_spec((q_rank, cat), layer), _const_spec((kv_rank, cat), layer),
                  _const_spec((1, HEAD_DIM)), _const_spec((1, HEAD_DIM)),
                  _const_spec((1, HEAD_DIM)), _const_spec((1, HEAD_DIM))],
        out_specs=[pl.BlockSpec((None, tm, cat), lambda i, j: (i, j, 0))] * 2
                  + [pl.BlockSpec((None, tm, HEADS * HEAD_DIM), lambda i, j: (i, j, 0))],
        out_shape=[out, out, jax.ShapeDtypeStruct((b, lp, HEADS * HEAD_DIM), BF16)],
        compiler_params=_params(("parallel", "parallel"), VMEM_BIG),
        name="mla_prep",
    )(ab3, ab3, ab3, *tables, q_norm.reshape(1, -1), kv_norm.reshape(1, -1), wq, wkv,
      q_gain[:HEAD_DIM].reshape(1, HEAD_DIM), slot(q_gain),
      k_gain[:HEAD_DIM].reshape(1, HEAD_DIM), slot(k_gain))


def _softmax_sweep(qs, key_fn, val_fn, bias_fn, n_full, tq, diag_mask, tail, scratch):
    nh = len(qs)
    s_even, s_odd, m_ref, l_ref, acc_ref = scratch[:5]
    tail_ref = scratch[5] if tail is not None else None

    def scores(h, start, width):
        s = lax.dot_general(qs[h], key_fn(h, start, width), NT_DIMS, preferred_element_type=F32)
        return bias_fn(h, s, start, width)

    def lanes(x, width):
        return jnp.concatenate([x] * (width // PAD_TO), axis=1)

    def absorb(h, s, vals):
        m = m_ref[h]
        m_new = jnp.maximum(m, jnp.max(s, axis=-1, keepdims=True))
        alpha = jnp.exp2(m - m_new)
        p = jnp.exp2(s - lanes(m_new, s.shape[1]))
        p_sum = p[:, :PAD_TO]
        for c in range(1, s.shape[1] // PAD_TO):
            p_sum = p_sum + p[:, c * PAD_TO:(c + 1) * PAD_TO]
        l_ref[h] = alpha * l_ref[h] + p_sum
        acc_ref[h] = alpha * acc_ref[h] + jnp.dot(p.astype(BF16), vals, preferred_element_type=F32)
        m_ref[h] = m_new

    def half_step(c, cur, nxt, diagonal):
        start = pl.multiple_of(c * tq, tq)
        for h in range(nh):
            if not diagonal:
                nxt[h] = scores(h, start + tq, tq)
                absorb(h, cur[h], val_fn(h, start, tq))
                continue
            s = jnp.where(diag_mask(), cur[h], NEG_BIG)
            vals = val_fn(h, start, tq)
            if tail is not None:
                t_start, t_width, t_mask = tail
                s_tail = jnp.where(t_mask(), tail_ref[h], NEG_BIG)
                s = jnp.concatenate([s, s_tail], axis=1)
                vals = jnp.concatenate([vals, val_fn(h, t_start, t_width)], axis=0)
            absorb(h, s, vals)

    def by_parity(c, diagonal):
        return lambda: lax.cond(c % 2 == 0, lambda: half_step(c, s_even, s_odd, diagonal),
                                lambda: half_step(c, s_odd, s_even, diagonal))

    def step(c, carry):
        lax.cond(c == n_full, by_parity(c, True), by_parity(c, False))
        return carry

    for h in range(nh):
        m_ref[h] = jnp.full((tq, PAD_TO), NEG_BIG, F32)
        l_ref[h] = jnp.zeros((tq, PAD_TO), F32)
        acc_ref[h] = jnp.zeros((tq, HEAD_DIM), F32)
        s_even[h] = scores(h, 0, tq)
        if tail is not None:
            tail_ref[h] = scores(h, tail[0], tail[1])
    lax.fori_loop(0, n_full + 1, step, 0)
    return [acc_ref[h] / jnp.sum(l_ref[h], axis=-1, keepdims=True) for h in range(nh)]


def _sweep_scratch(tq, tail_width=0):
    bufs = [pltpu.VMEM((ATT_HEADS, tq, tq), F32), pltpu.VMEM((ATT_HEADS, tq, tq), F32),
            pltpu.VMEM((ATT_HEADS, tq, PAD_TO), F32), pltpu.VMEM((ATT_HEADS, tq, PAD_TO), F32),
            pltpu.VMEM((ATT_HEADS, tq, HEAD_DIM), F32)]
    if tail_width:
        bufs.append(pltpu.VMEM((ATT_HEADS, tq, tail_width), F32))
    return bufs


def _mla_attn_kernel(q_ref, k_ref, v_ref, o_ref, *scratch):
    tq, lk = q_ref.shape[0], k_ref.shape[0]
    width = 2 * HEAD_DIM
    i = pl.program_id(2)
    q0 = i * tq
    shift = CHUNK - N_META

    def chunk_mask(k_start, k_width):
        bits = CHUNK.bit_length() - 1
        qc = lax.shift_right_logical(
            q0 + shift + lax.broadcasted_iota(jnp.int32, (tq, k_width), 0), bits)
        kc = lax.shift_right_logical(
            k_start + shift + lax.broadcasted_iota(jnp.int32, (tq, k_width), 1), bits)
        return kc <= qc

    t_start = pl.multiple_of(jnp.minimum(q0 + tq, lk - PAD_TO), PAD_TO)

    def t_mask():
        t_kpos = t_start + lax.broadcasted_iota(jnp.int32, (tq, PAD_TO), 1)
        return (t_kpos >= q0 + tq) & chunk_mask(t_start, PAD_TO)

    qs = [q_ref[:, h * width:(h + 1) * width] for h in range(ATT_HEADS)]
    outs = _softmax_sweep(
        qs,
        lambda h, s0, w: k_ref[pl.ds(s0, w), h * width:(h + 1) * width],
        lambda h, s0, w: v_ref[pl.ds(s0, w), h * HEAD_DIM:(h + 1) * HEAD_DIM],
        lambda h, s, s0, w: s,
        i, tq, lambda: chunk_mask(q0, tq), (t_start, PAD_TO, t_mask), scratch)
    for h in range(ATT_HEADS):
        o_ref[:, h * HEAD_DIM:(h + 1) * HEAD_DIM] = outs[h].astype(o_ref.dtype)


def _mla_attention(q, k, v):
    b, lp, _ = q.shape
    tq = SEQ_TILE
    gw = ATT_HEADS * 2 * HEAD_DIM
    return pl.pallas_call(
        _mla_attn_kernel,
        grid=(b, HEADS // ATT_HEADS, lp // tq),
        in_specs=[pl.BlockSpec((None, tq, gw), lambda bi, g, i: (bi, i, g)),
                  pl.BlockSpec((None, lp, gw), lambda bi, g, i: (bi, 0, g)),
                  pl.BlockSpec((None, lp, ATT_HEADS * HEAD_DIM), lambda bi, g, i: (bi, 0, g))],
        out_specs=pl.BlockSpec((None, tq, ATT_HEADS * HEAD_DIM), lambda bi, g, i: (bi, i, g)),
        out_shape=jax.ShapeDtypeStruct((b, lp, HEADS * HEAD_DIM), BF16),
        scratch_shapes=_sweep_scratch(tq, PAD_TO),
        compiler_params=_params(("parallel", "parallel", "arbitrary"), VMEM_BIG),
        name="mla_attention",
    )(q, k, v)


def _fox_attn_kernel(q_ref, k_ref, v_ref, f_ref, qg_ref, kg_ref, o_ref, kn_ref, va_ref, fb_ref,
                     *scratch, q_scale):
    tq, lk = q_ref.shape[0], k_ref.shape[0]
    i = pl.program_id(2)
    q0 = pl.multiple_of(i * tq, tq)
    heads = [slice(h * HEAD_DIM, (h + 1) * HEAD_DIM) for h in range(ATT_HEADS)]

    @pl.when(i == 0)
    def _():
        def body(c, carry):
            rows = pl.ds(pl.multiple_of(c * tq, tq), tq)
            for sl in heads:
                kn_ref[rows, sl] = _rms(k_ref[rows, sl], kg_ref[...]).astype(BF16)
            va_ref[rows, :] = v_ref[rows, :].astype(BF16)
            return carry
        lax.fori_loop(0, lk // tq, body, 0)
        fb_ref[...] = f_ref[...] * LOG2E

    qs = [(_rms(q_ref[:, sl], qg_ref[...]) * q_scale).astype(BF16) for sl in heads]
    f0 = [jnp.max(fb_ref[h, :, pl.ds(q0, tq)], axis=-1, keepdims=True) for h in range(ATT_HEADS)]

    def causal():
        return (lax.broadcasted_iota(jnp.int32, (tq, tq), 1)
                <= lax.broadcasted_iota(jnp.int32, (tq, tq), 0))

    outs = _softmax_sweep(
        qs,
        lambda h, s0, w: kn_ref[pl.ds(s0, w), heads[h]],
        lambda h, s0, w: va_ref[pl.ds(s0, w), heads[h]],
        lambda h, s, s0, w: s - (fb_ref[h, :, pl.ds(s0, w)] - f0[h]),
        i, tq, causal, None, scratch)
    for h, sl in enumerate(heads):
        o_ref[:, sl] = outs[h].astype(o_ref.dtype)


def _fox_attention(cd3, f, q_gain, k_gain, q_scale):
    b, lp, _ = cd3.shape
    tq = SEQ_TILE
    gw = ATT_HEADS * HEAD_DIM
    groups = HEADS // ATT_HEADS
    kern = functools.partial(_fox_attn_kernel, q_scale=q_scale)
    return pl.pallas_call(
        kern,
        grid=(b, groups, lp // tq),
        in_specs=[pl.BlockSpec((None, tq, gw), lambda bi, g, i: (bi, i, g)),
                  pl.BlockSpec((None, lp, gw), lambda bi, g, i: (bi, 0, groups + g)),
                  pl.BlockSpec((None, lp, gw), lambda bi, g, i: (bi, 0, 2 * groups + g)),
                  pl.BlockSpec((None, ATT_HEADS, 1, lp), lambda bi, g, i: (bi, g, 0, 0)),
                  _const_spec((1, HEAD_DIM)), _const_spec((1, HEAD_DIM))],
        out_specs=pl.BlockSpec((None, tq, gw), lambda bi, g, i: (bi, i, g)),
        out_shape=jax.ShapeDtypeStruct((b, lp, HEADS * HEAD_DIM), BF16),
        scratch_shapes=[pltpu.VMEM((lp, gw), BF16),
                        pltpu.VMEM((lp, gw), BF16),
                        pltpu.VMEM((ATT_HEADS, 1, lp), F32)] + _sweep_scratch(tq),
        compiler_params=_params(("parallel", "parallel", "arbitrary"), VMEM_BIG),
        name="fox_attention",
    )(cd3, cd3, cd3, f, q_gain.reshape(1, HEAD_DIM), k_gain.reshape(1, HEAD_DIM))


def _forget_cumsum_kernel(x_ref, b_ref, o_ref):
    rows, lp = x_ref.shape
    lane = lax.broadcasted_iota(jnp.int32, (rows, PAD_TO), 1)
    carry = jnp.zeros((rows, 1), F32)
    for c in range(lp // PAD_TO):
        sl = slice(c * PAD_TO, (c + 1) * PAD_TO)
        x = _log_sigmoid(x_ref[:, sl] + b_ref[...])
        step = 1
        while step < PAD_TO:
            x = x + jnp.where(lane >= step, pltpu.roll(x, step, axis=1), 0.0)
            step *= 2
        x = x + carry
        o_ref[:, sl] = x
        carry = x[:, PAD_TO - 1:PAD_TO]


def _forget_cumsum(ff_t, bias_col):
    rows, lp = ff_t.shape
    return pl.pallas_call(
        _forget_cumsum_kernel,
        out_shape=jax.ShapeDtypeStruct((rows, lp), F32),
        name="fox_forget_cumsum",
    )(ff_t, bias_col)


def _cumsum_rows(x):
    row = lax.broadcasted_iota(jnp.int32, x.shape, 0)
    step = 1
    while step < x.shape[0]:
        x = x + jnp.where(row >= step, pltpu.roll(x, step, axis=0), 0.0)
        step *= 2
    return x


def _gla_kernel(q_ref, k_ref, v_ref, r_ref, a_ref, wa_ref, ba_ref, on_ref, o_ref, st_ref):
    tm = q_ref.shape[0]

    @pl.when(pl.program_id(1) == 0)
    def _():
        st_ref[...] = jnp.zeros_like(st_ref)

    causal = (lax.broadcasted_iota(jnp.int32, (CHUNK, CHUNK), 1)
              <= lax.broadcasted_iota(jnp.int32, (CHUNK, CHUNK), 0))

    def body(c, carry):
        r0 = pl.multiple_of(c * CHUNK, CHUNK)
        rows = pl.ds(r0, CHUNK)
        gate = jnp.dot(a_ref[rows, :].astype(BF16), wa_ref[...], preferred_element_type=F32)
        for h in range(GLA_HEADS):
            ks = slice(h * GLA_DK, (h + 1) * GLA_DK)
            vs = slice(h * GLA_DV, (h + 1) * GLA_DV)
            g = _log_sigmoid(gate[:, ks] + ba_ref[:, ks]) / GLA_TAU
            bcum = _cumsum_rows(g)
            b_last = bcum[CHUNK - 1:CHUNK, :]
            k = k_ref[rows, ks]
            v = v_ref[rows, vs].astype(BF16)
            q_dec = ((q_ref[rows, ks] * (GLA_DK ** -0.5)) * jnp.exp(bcum)).astype(BF16)
            k_inv = (k * jnp.exp(-bcum)).astype(BF16)
            k_end = (k * jnp.exp(b_last - bcum)).astype(BF16)
            a = lax.dot_general(q_dec, k_inv, NT_DIMS, preferred_element_type=F32)
            a = jnp.where(causal, a, 0.0).astype(BF16)
            st = st_ref[h]
            o = (jnp.dot(a, v, preferred_element_type=F32)
                 + lax.dot_general(q_dec, st.astype(BF16), NT_DIMS, preferred_element_type=F32))
            st_ref[h] = st * jnp.exp(b_last) + lax.dot_general(v, k_end, TN_DIMS,
                                                               preferred_element_type=F32)
            o = _rms(o, on_ref[...])
            o_ref[rows, vs] = (o * _silu(r_ref[rows, vs])).astype(o_ref.dtype)
        return carry

    lax.fori_loop(0, tm // CHUNK, body, 0)


def _gla(cd3, gates3, wa, layer, ba, o_norm):
    b, lp, _ = cd3.shape
    tm = SEQ_TILE
    kw, vw = GLA_HEADS * GLA_DK, GLA_HEADS * GLA_DV
    return pl.pallas_call(
        _gla_kernel,
        grid=(b, lp // tm),
        in_specs=[pl.BlockSpec((None, tm, kw), lambda bi, i: (bi, i, 6144 // kw)),
                  pl.BlockSpec((None, tm, kw), lambda bi, i: (bi, i, 7168 // kw)),
                  pl.BlockSpec((None, tm, vw), lambda bi, i: (bi, i, 8192 // vw)),
                  pl.BlockSpec((None, tm, vw), lambda bi, i: (bi, i, 10240 // vw)),
                  pl.BlockSpec((None, tm, PAD_TO), lambda bi, i: (bi, i, 0)),
                  _const_spec((PAD_TO, kw), layer), _const_spec((1, kw)), _const_spec((1, GLA_DV))],
        out_specs=pl.BlockSpec((None, tm, vw), lambda bi, i: (bi, i, 0)),
        out_shape=jax.ShapeDtypeStruct((b, lp, vw), BF16),
        scratch_shapes=[pltpu.VMEM((GLA_HEADS, GLA_DV, GLA_DK), F32)],
        compiler_params=_params(("parallel", "arbitrary"), VMEM_MID),
        name="gla",
    )(cd3, cd3, cd3, cd3, gates3, wa, ba.reshape(1, -1), o_norm.reshape(1, -1))


def _ab_in_weight(w):
    wt = jnp.swapaxes(w, 1, 2)
    zeros = jnp.zeros((w.shape[0], 3840 - w.shape[2], w.shape[1]), w.dtype)
    return jnp.concatenate([wt, zeros], axis=1).astype(BF16)


CD_MAIN = 12288
CD_ROWS = CD_MAIN + PAD_TO


def _cd_in_weight(w):
    wt = jnp.swapaxes(w, 1, 2)
    pieces = [wt[:, :6144], wt[:, 6160:10256], wt[:, 10272:], wt[:, 6144:6160],
              wt[:, 10256:10272], jnp.zeros((w.shape[0], CD_ROWS - w.shape[2], w.shape[1]), w.dtype)]
    return jnp.concatenate(pieces, axis=1).astype(BF16)


def _mla_q_weight(w):
    n, r, _ = w.shape
    w = jnp.pad(w.astype(BF16).reshape(n, r, HEADS, MLA_QK),
                ((0, 0), (0, 0), (0, 0), (0, 2 * HEAD_DIM - MLA_QK)))
    return w.reshape(n, r, HEADS * 2 * HEAD_DIM)


def _gla_gate_weight(w):
    return jnp.pad(w.astype(BF16), ((0, 0), (GLA_RANK, PAD_TO - 2 * GLA_RANK), (0, 0)))


def _rotary_tables(lp):
    pos = jnp.arange(lp, dtype=F32)
    inv_freq = ROPE_BASE ** (-jnp.arange(0, MLA_ROPE, 2, dtype=F32) / MLA_ROPE)
    ang = pos[:, None] * inv_freq[None, :]
    cos, sin = jnp.cos(ang), jnp.sin(ang)
    z32, z64 = jnp.zeros_like(cos), jnp.zeros((lp, HEAD_DIM - MLA_ROPE), F32)
    return (jnp.concatenate([cos, cos, z64], axis=1),
            jnp.concatenate([-sin, z32, z64], axis=1),
            jnp.concatenate([z32, sin, z64], axis=1))


def _ffn(h, norm, w_gate, w_up, w_down, layer, name):
    hidden = _norm_matmul(h, norm, [w_gate, w_up], layer, _swiglu_epilogue, BF16, ROW_TILE, 512,
                          name + "_up")
    return _matmul_residual([hidden], w_down, layer, h, 0.5, DOWN_ROWS, OUT_COLS, name + "_down")


def _pool_mla_layer(h, b, lp, norm, w_in, pool_w, pool_scale, q_norm, wq, kv_norm, wkv,
                    q_gain, k_gain, w_out, layer, tables):
    ab = _norm_matmul(h, norm, [w_in], layer, _identity, F32, ROW_TILE, 768, "ab_in",
                      w_transposed=True)
    y_pool = _pool_mixer(ab, pool_w, layer, pool_scale, lp)
    q, k, v = _mla_prep(ab.reshape(b, lp, -1), tables, q_norm, kv_norm, wq, wkv, layer,
                        q_gain, k_gain, MLA_QK ** -0.5 * LOG2E)
    y_mla = _mla_attention(q, k, v)
    return _matmul_residual([y_pool, y_mla.reshape(b * lp, -1)], w_out, layer, h, 1.0,
                            RES_ROWS, OUT_COLS, "ab_out")


def _fox_gla_layer(h, b, lp, norm, w_in, fox_q_gain, fox_k_gain, fox_f_bias, wa, gla_b_a,
                   gla_o_norm, w_out, layer):
    cd, gates = _norm_matmul(h, norm, [w_in], layer, _identity, F32, ROW_TILE, OUT_COLS, "cd_in",
                             w_transposed=True, n=CD_MAIN, side_rows=PAD_TO)
    cd3 = cd.reshape(b, lp, -1)
    ff_t = gates[:, :HEADS].reshape(b, lp, HEADS).transpose(0, 2, 1).reshape(b * HEADS, lp)
    f = _forget_cumsum(ff_t, jnp.tile(fox_f_bias, b).reshape(b * HEADS, 1))
    y_fox = _fox_attention(cd3, f.reshape(b, HEADS, 1, lp), fox_q_gain, fox_k_gain,
                           HEAD_DIM ** -0.5 * LOG2E)
    y_gla = _gla(cd3, gates.reshape(b, lp, -1), wa, layer, gla_b_a, gla_o_norm)
    return _matmul_residual([y_fox.reshape(b * lp, -1), y_gla.reshape(b * lp, -1)], w_out, layer,
                            h, 1.0, RES_ROWS, OUT_COLS, "cd_out")


def kernel(x, meta_tokens, ffn1_norm, ffn1_w_gate, ffn1_w_up, ffn1_w_down, mix_norm, ffn2_norm, ffn2_w_gate, ffn2_w_up, ffn2_w_down, ab_w_in, pool_w, pool_scale, mla_q_norm, mla_w_q_up, mla_kv_norm, mla_w_kv_up, mla_q_gain, mla_k_gain, ab_w_out, cd_w_in, fox_q_gain, fox_k_gain, fox_f_bias, gla_w_a2, gla_b_a, gla_o_norm, cd_w_out):
    b, s, d = x.shape
    length = N_META + s
    lp = -(-length // PAD_TO) * PAD_TO
    meta = jnp.broadcast_to(meta_tokens.astype(x.dtype)[None], (b, N_META, d))
    h = jnp.concatenate([meta, x, jnp.zeros((b, lp - length, d), x.dtype)], axis=1)
    h = h.reshape(b * lp, d)
    tables = _rotary_tables(lp)

    ffn1 = [w.astype(BF16) for w in (ffn1_w_gate, ffn1_w_up, ffn1_w_down)]
    ffn2 = [w.astype(BF16) for w in (ffn2_w_gate, ffn2_w_up, ffn2_w_down)]
    ab_in, cd_in = _ab_in_weight(ab_w_in), _cd_in_weight(cd_w_in)
    ab_out, cd_out = ab_w_out.astype(BF16), cd_w_out.astype(BF16)
    pool_wb, wq, wkv = pool_w.astype(BF16), _mla_q_weight(mla_w_q_up), mla_w_kv_up.astype(BF16)
    wa = _gla_gate_weight(gla_w_a2)

    for layer in range(ffn1_norm.shape[0]):
        i = layer // 2
        h = _ffn(h, ffn1_norm[layer], *ffn1, layer, "ffn1")
        if layer % 2 == 0:
            h = _pool_mla_layer(h, b, lp, mix_norm[layer], ab_in, pool_wb, pool_scale[i],
                                mla_q_norm[i], wq, mla_kv_norm[i], wkv, mla_q_gain[i],
                                mla_k_gain[i], ab_out, i, tables)
        else:
            h = _fox_gla_layer(h, b, lp, mix_norm[layer], cd_in, fox_q_gain[i], fox_k_gain[i],
                               fox_f_bias[i], wa, gla_b_a[i], gla_o_norm[i], cd_out, i)
        h = _ffn(h, ffn2_norm[layer], *ffn2, layer, "ffn2")
    return h.reshape(b, lp, d)[:, N_META:N_META + s]
```
